```python
import math
import functools
import jax
import jax.numpy as jnp
from jax import lax
import numpy as np

D_MODEL = 1024
BATCH = 8
SEQ = 2048
DEPTH = 1
DEC_BATCH = 32
DEC_SEQ = 1
PAST_LEN = 16384
PAGE_SIZE = 128

EPS = 1e-6
Q_BLOCK = 128
MLA_HEADS = 8
MLA_NOPE = 64
MLA_ROPE = 32
MLA_V = 64
Q_LORA = 256
KV_LORA = 128
ROPE_BASE = 10000.0
MLA_SCALE = (MLA_NOPE + MLA_ROPE) ** -0.5
DSA_HEADS = 8
DSA_HEAD_DIM = 64
DSA_SCALE = DSA_HEAD_DIM ** -0.5
IDX_HEADS = 8
IDX_DIM = 64
IDX_TOPK_MAX = 256
REL_BUCKETS = 32
REL_MAX_DIST = 128
N_EXPERTS = 64
TOP_K = 6
N_GROUPS = 8
TOPK_GROUPS = 4
D_EXPERT = 256
D_SHARED = 256
ROUTED_SCALE = 2.5
EXPERT_BLOCK = 128

MLA_OUT = MLA_HEADS * MLA_V
DSA_OUT = DSA_HEADS * DSA_HEAD_DIM
MIX_WIDTH = MLA_OUT + DSA_OUT
IN_SIZES = (Q_LORA, KV_LORA, MLA_ROPE, DSA_OUT, DSA_OUT, DSA_OUT, IDX_HEADS * IDX_DIM, IDX_DIM, IDX_HEADS)
D_IN = Q_LORA + KV_LORA + MLA_ROPE + 3 * DSA_OUT + IDX_HEADS * IDX_DIM + IDX_DIM + IDX_HEADS

kernel_name = 'hymba_mla_dsa_moe_adaln_step'


def rms_norm(x, g):
    xf = x.astype(jnp.float32)
    y = xf * lax.rsqrt(jnp.mean(xf * xf, axis=-1, keepdims=True) + EPS)
    return (y * g.astype(jnp.float32)).astype(x.dtype)


def adaln(c, w_ada, b_ada):
    mod = (jax.nn.silu(c) @ w_ada + b_ada)[:, None, :]
    return jnp.split(mod, 6, axis=-1)


def rope(x, pos):
    half = x.shape[-1] // 2
    inv = ROPE_BASE ** (-jnp.arange(half, dtype=jnp.float32) / half)
    ang = pos.astype(jnp.float32)[..., None] * inv
    cos, sin = jnp.cos(ang), jnp.sin(ang)
    x1 = x[..., :half].astype(jnp.float32)
    x2 = x[..., half:].astype(jnp.float32)
    return jnp.concatenate([x1 * cos - x2 * sin, x2 * cos + x1 * sin], axis=-1).astype(x.dtype)


def t5_bucket(dist):
    exact = REL_BUCKETS // 2
    d = jnp.maximum(dist, 0)
    logd = jnp.log(jnp.maximum(d, 1).astype(jnp.float32) / exact) / math.log(REL_MAX_DIST / exact)
    far = jnp.minimum(exact + (logd * (REL_BUCKETS - exact)).astype(jnp.int32), REL_BUCKETS - 1)
    return jnp.where(d < exact, d, far)


def project_mixers(h, pos, w_in, g_q_lora, w_q_up, g_kv_lora, g_mla_qn, g_mla_qr, g_mla_kr, g_dsa_q, g_dsa_k):
    B, T, _ = h.shape
    q_lat, kv_lat, kr_raw, q_d, k_d, v_d, q_i, k_i, w_i = jnp.split(
        h @ w_in, np.cumsum(IN_SIZES)[:-1].tolist(), axis=-1)
    q = (rms_norm(q_lat, g_q_lora) @ w_q_up).reshape(B, T, MLA_HEADS, MLA_NOPE + MLA_ROPE)
    q_nope = rms_norm(q[..., :MLA_NOPE], g_mla_qn)
    q_rope = rope(rms_norm(q[..., MLA_NOPE:], g_mla_qr), pos[..., None])
    c_kv = rms_norm(kv_lat, g_kv_lora)
    k_rope = rope(rms_norm(kr_raw, g_mla_kr), pos)
    q_d = rms_norm(q_d.reshape(B, T, DSA_HEADS, DSA_HEAD_DIM), g_dsa_q)
    k_d = rms_norm(k_d.reshape(B, T, DSA_HEADS, DSA_HEAD_DIM), g_dsa_k)
    v_d = v_d.reshape(B, T, DSA_HEADS, DSA_HEAD_DIM)
    q_i = q_i.reshape(B, T, IDX_HEADS, IDX_DIM)
    w_i = w_i * IDX_HEADS ** -0.5
    return (q_nope, q_rope, c_kv, k_rope, q_d, k_d, v_d, q_i, k_i, w_i)


def split_kv_up(w_kv_up):
    w = w_kv_up.reshape(KV_LORA, MLA_HEADS, MLA_NOPE + MLA_V)
    return w[..., :MLA_NOPE], w[..., MLA_NOPE:]


def mla_keys(c_kv, w_uk, g_kn):
    return rms_norm(jnp.einsum('...c,chd->...hd', c_kv, w_uk), g_kn)


def mla_attend(q_nope, q_rope, k_nope, k_rope, c_kv, key_pos, q_pos, w_uv):
    s = (jnp.einsum('qhd,shd->hqs', q_nope, k_nope).astype(jnp.float32)
         + jnp.einsum('qhr,sr->hqs', q_rope, k_rope).astype(jnp.float32)) * MLA_SCALE
    s = jnp.where((key_pos[None, :] <= q_pos[:, None])[None], s, -jnp.inf)
    p = jax.nn.softmax(s, axis=-1).astype(c_kv.dtype)
    o_lat = jnp.einsum('hqs,sc->qhc', p, c_kv)
    return jnp.einsum('qhc,chv->qhv', o_lat, w_uv).reshape(q_nope.shape[0], MLA_OUT)


def indexer_select(q_idx, w_idx, k_idx, key_pos, q_pos, n_sel):
    r = jax.nn.relu(jnp.einsum('qhd,sd->qhs', q_idx, k_idx).astype(jnp.float32) * IDX_DIM ** -0.5)
    score = jnp.einsum('qh,qhs->qs', w_idx.astype(jnp.float32), r)
    causal = key_pos[None, :] <= q_pos[:, None]
    _, sel = lax.top_k(jnp.where(causal, score, -jnp.inf), n_sel)
    return sel, key_pos[sel] <= q_pos[:, None]


def dsa_attend(q, k_sel, v_sel, sel_pos, valid, q_pos, rel_bias):
    s = jnp.einsum('qhd,qkhd->hqk', q, k_sel).astype(jnp.float32) * DSA_SCALE
    bias = rel_bias[t5_bucket(q_pos[:, None] - sel_pos)].astype(jnp.float32)
    s = jnp.where(valid[None], s + jnp.transpose(bias, (2, 0, 1)), -jnp.inf)
    p = jax.nn.softmax(s, axis=-1).astype(v_sel.dtype)
    return jnp.einsum('hqk,qkhd->qhd', p, v_sel).reshape(q.shape[0], DSA_OUT)


def gather_rows(cache, pages, new, idx, past):
    p = jnp.minimum(idx, past - 1)
    old = cache[pages[p // PAGE_SIZE], p % PAGE_SIZE]
    fresh = new[jnp.clip(idx - past, 0, new.shape[0] - 1)]
    in_past = (idx < past).reshape(idx.shape + (1,) * (old.ndim - idx.ndim))
    return jnp.where(in_past, old, fresh)


def mixers_prompt(m, w_uk, w_uv, g_mla_kn, rel_bias):
    q_nope, q_rope, c_kv, k_rope, q_d, k_d, v_d, q_i, k_i, w_i = m
    B, T = c_kv.shape[:2]
    n_blocks = T // Q_BLOCK
    n_sel = min(IDX_TOPK_MAX, T // 4)
    key_pos = jnp.arange(T)
    k_nope = mla_keys(c_kv, w_uk, g_mla_kn)

    def mla_block(i):
        q0 = i * Q_BLOCK
        q_pos = q0 + jnp.arange(Q_BLOCK)
        qn = lax.dynamic_slice_in_dim(q_nope, q0, Q_BLOCK, axis=1)
        qr = lax.dynamic_slice_in_dim(q_rope, q0, Q_BLOCK, axis=1)
        return jax.vmap(mla_attend, in_axes=(0, 0, 0, 0, 0, None, None, None))(
            qn, qr, k_nope, k_rope, c_kv, key_pos, q_pos, w_uv)

    o_mla = lax.map(mla_block, jnp.arange(n_blocks))
    o_mla = jnp.moveaxis(o_mla, 0, 1).reshape(B, T, MLA_OUT)

    def dsa_block(j):
        b, i = j // n_blocks, j % n_blocks
        q0 = i * Q_BLOCK
        q_pos = q0 + jnp.arange(Q_BLOCK)
        sl = lambda a: lax.dynamic_slice_in_dim(a[b], q0, Q_BLOCK, axis=0)
        sel, valid = indexer_select(sl(q_i), sl(w_i), k_i[b], key_pos, q_pos, n_sel)
        return dsa_attend(sl(q_d), k_d[b][sel], v_d[b][sel], key_pos[sel], valid, q_pos, rel_bias)

    o_dsa = lax.map(dsa_block, jnp.arange(B * n_blocks)).reshape(B, T, DSA_OUT)
    return jnp.concatenate([o_mla, o_dsa], axis=-1)


def mixers_sample(m, cache_lat, cache_kr, cache_k, cache_v, cache_idx, page_table,
                  w_uk, w_uv, g_mla_kn, rel_bias):
    q_nope, q_rope, c_kv, k_rope, q_d, k_d, v_d, q_i, k_i, w_i = m
    T = c_kv.shape[1]
    past = page_table.shape[1] * PAGE_SIZE
    L = past + T
    key_pos = jnp.arange(L)
    q_pos = past + jnp.arange(T)
    n_sel = min(IDX_TOPK_MAX, L // 4)

    def past_rows(cache, pages):
        return cache[pages].reshape((past,) + cache.shape[2:])

    def mla_seq(args):
        qn, qr, cn, krn, pages = args
        lat = jnp.concatenate([past_rows(cache_lat, pages), cn], axis=0)
        kr = jnp.concatenate([past_rows(cache_kr, pages), krn], axis=0)
        return mla_attend(qn, qr, mla_keys(lat, w_uk, g_mla_kn), kr, lat, key_pos, q_pos, w_uv)

    o_mla = lax.map(mla_seq, (q_nope, q_rope, c_kv, k_rope, page_table))

    def dsa_seq(qd, kd, vd, qi, ki, wi, pages):
        keys_idx = jnp.concatenate([past_rows(cache_idx, pages), ki], axis=0)
        sel, valid = indexer_select(qi, wi, keys_idx, key_pos, q_pos, n_sel)
        k_sel = gather_rows(cache_k, pages, kd, sel, past)
        v_sel = gather_rows(cache_v, pages, vd, sel, past)
        return dsa_attend(qd, k_sel, v_sel, sel, valid, q_pos, rel_bias)

    o_dsa = jax.vmap(dsa_seq)(q_d, k_d, v_d, q_i, k_i, w_i, page_table)
    return jnp.concatenate([o_mla, o_dsa], axis=-1)


def swiglu(x, w_gu, w_down):
    g, u = jnp.split(x @ w_gu, 2, axis=-1)
    return (jax.nn.silu(g) * u) @ w_down


def routed_experts(x, eidx, gate, w_e_gu, w_e_down):
    N, D = x.shape
    A = N * TOP_K
    blk = int(min(EXPERT_BLOCK, max(8, A // N_EXPERTS)))
    flat_e = eidx.reshape(-1)
    order = jnp.argsort(flat_e)
    e_sorted = flat_e[order]
    counts = jnp.zeros((N_EXPERTS,), jnp.int32).at[flat_e].add(1)
    start = jnp.cumsum(counts) - counts
    padded = (counts + blk - 1) // blk * blk
    pad_end = jnp.cumsum(padded)
    slot = (pad_end - padded)[e_sorted] + jnp.arange(A) - start[e_sorted]
    n_blocks = -(-(A + N_EXPERTS * (blk - 1)) // blk)
    n_slots = n_blocks * blk
    slot_token = jnp.full((n_slots,), N, jnp.int32).at[slot].set((order // TOP_K).astype(jnp.int32))
    slot_gate = jnp.zeros((n_slots,), x.dtype).at[slot].set(gate.reshape(-1)[order])
    block_expert = jnp.minimum(
        jnp.searchsorted(pad_end, jnp.arange(n_blocks) * blk, side='right'), N_EXPERTS - 1)
    x_pad = jnp.concatenate([x, jnp.zeros((1, D), x.dtype)], axis=0)
    xb = x_pad[slot_token].reshape(n_blocks, blk, D)
    yb = lax.map(lambda a: swiglu(a[0], w_e_gu[a[1]], w_e_down[a[1]]), (xb, block_expert))
    y = jnp.zeros((N + 1, D), x.dtype).at[slot_token].add(yb.reshape(n_slots, D) * slot_gate[:, None])
    return y[:N]


def moe(h, w_router, b_router, w_e_gu, w_e_down, w_s_gu, w_s_down):
    B, T, D = h.shape
    xt = h.reshape(B * T, D)
    N = xt.shape[0]
    scores = jax.nn.sigmoid((xt @ w_router).astype(jnp.float32))
    biased = scores + b_router.astype(jnp.float32)
    grp_score = lax.top_k(biased.reshape(N, N_GROUPS, N_EXPERTS // N_GROUPS), 2)[0].sum(-1)
    _, top_g = lax.top_k(grp_score, TOPK_GROUPS)
    gmask = jnp.any(top_g[..., None] == jnp.arange(N_GROUPS), axis=1)
    masked = jnp.where(jnp.repeat(gmask, N_EXPERTS // N_GROUPS, axis=1), biased, -jnp.inf)
    _, eidx = lax.top_k(masked, TOP_K)
    w = jnp.take_along_axis(scores, eidx, axis=1)
    w = w / jnp.sum(w, axis=-1, keepdims=True) * ROUTED_SCALE
    routed = routed_experts(xt, eidx, w.astype(xt.dtype), w_e_gu, w_e_down)
    return (routed + swiglu(xt, w_s_gu, w_s_down)).reshape(B, T, D)


def setup_inputs(seed: int = 0) -> dict:
    key = jax.random.key(seed)
    ks = iter(jax.random.split(key, 48))
    nrm = lambda shape, scale=1.0: jax.random.normal(next(ks), shape, jnp.float32) * scale
    gain = lambda shape: 1.0 + 0.05 * jax.random.normal(next(ks), shape, jnp.float32)
    n_pages = PAST_LEN // PAGE_SIZE
    n_used = DEC_BATCH * n_pages
    n_pool = n_used + max(1, n_used // 4)
    pool = lambda *tail: nrm((DEPTH, n_pool, PAGE_SIZE) + tail)
    return {
        'x_prompt': nrm((BATCH, SEQ, D_MODEL)),
        'x_sample': nrm((DEC_BATCH, DEC_SEQ, D_MODEL)),
        'cache_mla_latent': pool(KV_LORA),
        'cache_mla_krope': pool(MLA_ROPE),
        'cache_dsa_k': pool(DSA_HEADS, DSA_HEAD_DIM),
        'cache_dsa_v': pool(DSA_HEADS, DSA_HEAD_DIM),
        'cache_idx_k': pool(IDX_DIM),
        'page_table': jax.random.permutation(next(ks), n_pool)[:n_used].reshape(DEC_BATCH, n_pages).astype(jnp.int32),
        'c_prompt': nrm((BATCH, D_MODEL)),
        'c_sample': nrm((DEC_BATCH, D_MODEL)),
        'rel_bias': nrm((REL_BUCKETS, DSA_HEADS), 0.5),
        'w_ada': nrm((DEPTH, D_MODEL, 6 * D_MODEL), 0.5 * D_MODEL ** -0.5),
        'b_ada': nrm((DEPTH, 6 * D_MODEL), 0.02),
        'g_attn_norm': gain((DEPTH, D_MODEL)),
        'w_in': nrm((DEPTH, D_MODEL, D_IN), D_MODEL ** -0.5),
        'g_q_lora': gain((DEPTH, Q_LORA)),
        'w_q_up': nrm((DEPTH, Q_LORA, MLA_HEADS * (MLA_NOPE + MLA_ROPE)), Q_LORA ** -0.5),
        'g_kv_lora': gain((DEPTH, KV_LORA)),
        'w_kv_up': nrm((DEPTH, KV_LORA, MLA_HEADS * (MLA_NOPE + MLA_V)), KV_LORA ** -0.5),
        'g_mla_qn': gain((DEPTH, MLA_NOPE)),
        'g_mla_qr': gain((DEPTH, MLA_ROPE)),
        'g_mla_kn': gain((DEPTH, MLA_NOPE)),
        'g_mla_kr': gain((DEPTH, MLA_ROPE)),
        'g_dsa_q': gain((DEPTH, DSA_HEAD_DIM)),
        'g_dsa_k': gain((DEPTH, DSA_HEAD_DIM)),
        'w_out': nrm((DEPTH, MIX_WIDTH, D_MODEL), MIX_WIDTH ** -0.5),
        'g_ffn_norm': gain((DEPTH, D_MODEL)),
        'w_router': nrm((DEPTH, D_MODEL, N_EXPERTS), D_MODEL ** -0.5),
        'b_router': nrm((DEPTH, N_EXPERTS), 0.01),
        'w_e_gu': nrm((DEPTH, N_EXPERTS, D_MODEL, 2 * D_EXPERT), D_MODEL ** -0.5),
        'w_e_down': nrm((DEPTH, N_EXPERTS, D_EXPERT, D_MODEL), D_EXPERT ** -0.5),
        'w_s_gu': nrm((DEPTH, D_MODEL, 2 * D_SHARED), D_MODEL ** -0.5),
        'w_s_down': nrm((DEPTH, D_SHARED, D_MODEL), D_SHARED ** -0.5),
    }


def reference(x_prompt, x_sample, cache_mla_latent, cache_mla_krope, cache_dsa_k, cache_dsa_v,
              cache_idx_k, page_table, c_prompt, c_sample, rel_bias, w_ada, b_ada, g_attn_norm,
              w_in, g_q_lora, w_q_up, g_kv_lora, w_kv_up, g_mla_qn, g_mla_qr, g_mla_kn, g_mla_kr,
              g_dsa_q, g_dsa_k, w_out, g_ffn_norm, w_router, b_router, w_e_gu, w_e_down,
              w_s_gu, w_s_down):
    past = page_table.shape[1] * PAGE_SIZE
    pos_p = jnp.arange(x_prompt.shape[1])[None, :]
    pos_s = past + jnp.arange(x_sample.shape[1])[None, :]
    xp, xs = x_prompt, x_sample
    new_p, new_s = [], []
    for l in range(DEPTH):
        w_uk, w_uv = split_kv_up(w_kv_up[l])
        proj = functools.partial(
            project_mixers, w_in=w_in[l], g_q_lora=g_q_lora[l], w_q_up=w_q_up[l], g_kv_lora=g_kv_lora[l],
            g_mla_qn=g_mla_qn[l], g_mla_qr=g_mla_qr[l], g_mla_kr=g_mla_kr[l],
            g_dsa_q=g_dsa_q[l], g_dsa_k=g_dsa_k[l])
        ffn = functools.partial(
            moe, w_router=w_router[l], b_router=b_router[l], w_e_gu=w_e_gu[l], w_e_down=w_e_down[l],
            w_s_gu=w_s_gu[l], w_s_down=w_s_down[l])
        sa_p, ca_p, ga_p, sf_p, cf_p, gf_p = adaln(c_prompt, w_ada[l], b_ada[l])
        sa_s, ca_s, ga_s, sf_s, cf_s, gf_s = adaln(c_sample, w_ada[l], b_ada[l])
        mp = proj(rms_norm(xp, g_attn_norm[l]) * (1 + ca_p) + sa_p, pos_p)
        ms = proj(rms_norm(xs, g_attn_norm[l]) * (1 + ca_s) + sa_s, pos_s)
        o_p = mixers_prompt(mp, w_uk, w_uv, g_mla_kn[l], rel_bias)
        o_s = mixers_sample(ms, cache_mla_latent[l], cache_mla_krope[l], cache_dsa_k[l], cache_dsa_v[l],
                            cache_idx_k[l], page_table, w_uk, w_uv, g_mla_kn[l], rel_bias)
        xp = xp + ga_p * (o_p @ w_out[l])
        xs = xs + ga_s * (o_s @ w_out[l])
        xp = xp + gf_p * ffn(rms_norm(xp, g_ffn_norm[l]) * (1 + cf_p) + sf_p)
        xs = xs + gf_s * ffn(rms_norm(xs, g_ffn_norm[l]) * (1 + cf_s) + sf_s)
        new_p.append((mp[2], mp[3], mp[5], mp[6], mp[8]))
        new_s.append((ms[2], ms[3], ms[5], ms[6], ms[8]))
    sp = [jnp.stack(a) for a in zip(*new_p)]
    ss = [jnp.stack(a) for a in zip(*new_s)]
    return (xp, xs, sp[0], sp[1], sp[2], sp[3], sp[4], ss[0], ss[1], ss[2], ss[3], ss[4])
```

```python
import functools
import math

import jax
import jax.numpy as jnp
import numpy as np
from jax import lax
from jax.experimental import pallas as pl
from jax.experimental.pallas import tpu as pltpu

F32 = jnp.float32
BF16 = jnp.bfloat16
I32 = jnp.int32

D_MODEL = 1024
PAGE_SIZE = 128
EPS = 1e-6
MLA_HEADS = 8
MLA_NOPE = 64
MLA_ROPE = 32
MLA_V = 64
Q_LORA = 256
KV_LORA = 128
ROPE_BASE = 10000.0
MLA_SCALE = (MLA_NOPE + MLA_ROPE) ** -0.5
DSA_HEADS = 8
DSA_HEAD_DIM = 64
DSA_SCALE = DSA_HEAD_DIM ** -0.5
IDX_HEADS = 8
IDX_DIM = 64
IDX_TOPK_MAX = 256
REL_BUCKETS = 32
REL_MAX_DIST = 128
N_EXPERTS = 64
TOP_K = 6
N_GROUPS = 8
TOPK_GROUPS = 4
D_EXPERT = 256
D_SHARED = 256
ROUTED_SCALE = 2.5
MLA_OUT = MLA_HEADS * MLA_V
DSA_OUT = DSA_HEADS * DSA_HEAD_DIM
IN_SIZES = (Q_LORA, KV_LORA, MLA_ROPE, DSA_OUT, DSA_OUT, DSA_OUT, IDX_HEADS * IDX_DIM, IDX_DIM, IDX_HEADS)

LANES = 128
INT_MIN = -(2 ** 31)
NEG_INF = float("-inf")
VMEM_LIMIT = 56 * 1024 * 1024


def _cparams(sem):
    return pltpu.CompilerParams(dimension_semantics=sem, vmem_limit_bytes=VMEM_LIMIT)


def _split_dot(x, m01, passes=3):
    acc = None
    r = x
    for p in range(passes):
        hi = r.astype(BF16)
        part = jnp.dot(hi, m01, preferred_element_type=F32)
        acc = part if acc is None else acc + part
        if p + 1 < passes:
            r = r - hi.astype(F32)
    return acc


def _group_mean(sq, bmat):
    outs = [_split_dot(sq[:, s * LANES:(s + 1) * LANES], bmat) for s in range(sq.shape[1] // LANES)]
    return outs[0] if len(outs) == 1 else jnp.concatenate(outs, axis=1)


def _rope_slabs(x, cos, sin):
    lane = lax.broadcasted_iota(I32, (x.shape[0], LANES), 1)
    first_half = (lane % MLA_ROPE) < (MLA_ROPE // 2)
    outs = []
    for s in range(x.shape[1] // LANES):
        xs = x[:, s * LANES:(s + 1) * LANES]
        rot = jnp.where(first_half, pltpu.roll(xs, LANES - MLA_ROPE // 2, 1), pltpu.roll(xs, MLA_ROPE // 2, 1))
        outs.append(xs * cos + rot * sin)
    return outs[0] if len(outs) == 1 else jnp.concatenate(outs, axis=1)


def _nt_dot(a, b):
    return lax.dot_general(a, b, (((1,), (1,)), ((), ())), preferred_element_type=F32)


def _adaln_kernel(c_ref, w_ref, b_ref, o_ref):
    c = c_ref[...]
    s = (c * jax.nn.sigmoid(c)).astype(BF16)
    o_ref[...] = jnp.dot(s, w_ref[...], preferred_element_type=F32) + b_ref[...]


def _adaln(c, w_bf, b):
    rows = c.shape[0]
    n = w_bf.shape[1]
    tn = 1536
    return pl.pallas_call(
        _adaln_kernel,
        grid=(n // tn,),
        in_specs=[pl.BlockSpec((rows, D_MODEL), lambda j: (0, 0)),
                  pl.BlockSpec((D_MODEL, tn), lambda j: (0, j)),
                  pl.BlockSpec((1, tn), lambda j: (0, j))],
        out_specs=pl.BlockSpec((rows, tn), lambda j: (0, j)),
        out_shape=jax.ShapeDtypeStruct((rows, n), F32),
        compiler_params=_cparams(("arbitrary",)),
        name="adaln",
    )(c, w_bf, b.reshape(1, n))


_C_QLAT = 0
_C_KV = _C_QLAT + Q_LORA
_C_QD = _C_KV + KV_LORA
_C_KD = _C_QD + DSA_OUT
_C_VD = _C_KD + DSA_OUT
_C_QI = _C_VD + DSA_OUT
_C_MISC = _C_QI + IDX_HEADS * IDX_DIM
_C_KIDUP = _C_MISC + LANES
_C_END = _C_KIDUP + LANES
_M_KI = MLA_ROPE
_M_WI = MLA_ROPE + IDX_DIM


def _proj_kernel(x_ref, sh_ref, sc_ref, ga_ref, win_ref, gql_ref, wqu_ref, gkv_ref, wuk_ref,
                 gq_ref, gk_ref, gm_ref, gdq_ref, gdk_ref, cq_ref, sq_ref, cm_ref, sm_ref,
                 bq_ref, b64_ref, bm_ref,
                 qall_ref, kall_ref, ckv_ref, ckvb_ref, misc_ref, qd_ref, kd_ref, kdb_ref, vd_ref, vdb_ref,
                 qi_ref, kid_ref):
    x = x_ref[...]
    xn = x * lax.rsqrt(jnp.mean(x * x, axis=-1, keepdims=True) + EPS) * ga_ref[...]
    h = xn * (1.0 + sc_ref[...]) + sh_ref[...]
    p = jnp.dot(h.astype(BF16), win_ref[...], preferred_element_type=F32)

    ql = p[:, _C_QLAT:_C_KV]
    qln = ql * lax.rsqrt(jnp.mean(ql * ql, axis=-1, keepdims=True) + EPS) * gql_ref[...]
    q = jnp.dot(qln.astype(BF16), wqu_ref[...], preferred_element_type=F32)
    qn = q * lax.rsqrt(_group_mean(q * q, bq_ref[...]) + EPS) * gq_ref[...]
    qall_ref[...] = _rope_slabs(qn, cq_ref[...], sq_ref[...]).astype(BF16)

    kv = p[:, _C_KV:_C_QD]
    ckv = kv * lax.rsqrt(jnp.mean(kv * kv, axis=-1, keepdims=True) + EPS) * gkv_ref[...]
    ckv_ref[...] = ckv
    ckvb = ckv.astype(BF16)
    ckvb_ref[...] = ckvb
    kn = jnp.dot(ckvb, wuk_ref[...], preferred_element_type=F32)
    kn = kn * lax.rsqrt(_group_mean(kn * kn, bq_ref[...]) + EPS) * gk_ref[...]

    m = p[:, _C_MISC:_C_KIDUP]
    lane = lax.broadcasted_iota(I32, m.shape, 1)
    is_kr = lane < MLA_ROPE
    mm = _split_dot(m * m, bm_ref[...])
    mn = jnp.where(is_kr, m * lax.rsqrt(mm + EPS) * gm_ref[...], m)
    mr = _rope_slabs(mn, cm_ref[...], sm_ref[...])
    is_wi = (lane >= _M_WI) & (lane < _M_WI + IDX_HEADS)
    misc_ref[...] = jnp.where(is_wi, mr * (IDX_HEADS ** -0.5), mr)
    kr_placed = jnp.where((lane >= MLA_NOPE) & (lane < MLA_NOPE + MLA_ROPE), pltpu.roll(mr, MLA_NOPE, 1), 0.0)
    kall_ref[...] = jnp.concatenate(
        [kn[:, s * LANES:(s + 1) * LANES] + kr_placed for s in range(MLA_HEADS)], axis=1).astype(BF16)

    qd = p[:, _C_QD:_C_KD]
    qd_ref[...] = (qd * lax.rsqrt(_group_mean(qd * qd, b64_ref[...]) + EPS) * gdq_ref[...]).astype(BF16)
    kd = p[:, _C_KD:_C_VD]
    kdn = kd * lax.rsqrt(_group_mean(kd * kd, b64_ref[...]) + EPS) * gdk_ref[...]
    kd_ref[...] = kdn
    kdb_ref[...] = kdn.astype(BF16)
    vd = p[:, _C_VD:_C_QI]
    vd_ref[...] = vd
    vdb_ref[...] = vd.astype(BF16)
    qi_ref[...] = p[:, _C_QI:_C_MISC].astype(BF16)
    kid_ref[...] = p[:, _C_KIDUP:_C_END].astype(BF16)


def _block_mean_matrix(blocks):
    m = np.zeros((LANES, LANES), np.float32)
    for start, size in blocks:
        m[start:start + size, start:start + size] = 1.0 / size
    return jnp.asarray(m, BF16)


def _prep_proj_weights(w_in, g_q_lora, w_q_up, g_kv_lora, w_uk, g_mla_qn, g_mla_qr, g_mla_kn, g_mla_kr,
                       g_dsa_q, g_dsa_k, g_attn_norm):
    offs = np.cumsum((0,) + IN_SIZES)
    sec = lambda k: w_in[:, offs[k]:offs[k + 1]]
    zeros = lambda n: jnp.zeros((D_MODEL, n), w_in.dtype)
    misc = jnp.concatenate([sec(2), sec(7), sec(8), zeros(LANES - MLA_ROPE - IDX_DIM - IDX_HEADS)], axis=1)
    w_in_r = jnp.concatenate([sec(0), sec(1), sec(3), sec(4), sec(5), sec(6), misc, sec(7), sec(7)], axis=1)
    wq = w_q_up.reshape(Q_LORA, MLA_HEADS, MLA_NOPE + MLA_ROPE)
    wq = jnp.pad(wq, ((0, 0), (0, 0), (0, LANES - MLA_NOPE - MLA_ROPE))).reshape(Q_LORA, MLA_HEADS * LANES)
    wk = jnp.pad(w_uk, ((0, 0), (0, 0), (0, LANES - MLA_NOPE))).reshape(KV_LORA, MLA_HEADS * LANES)
    pad1 = lambda v, n: jnp.pad(v, (0, n - v.shape[0]))
    gq = jnp.tile(pad1(jnp.concatenate([g_mla_qn, g_mla_qr]), LANES), MLA_HEADS).reshape(1, -1)
    gk = jnp.tile(pad1(g_mla_kn, LANES), MLA_HEADS).reshape(1, -1)
    gm = jnp.concatenate([g_mla_kr, jnp.ones((LANES - MLA_ROPE,), F32)]).reshape(1, -1)
    return dict(
        win=w_in_r.astype(BF16), wqu=wq.astype(BF16), wuk=wk.astype(BF16),
        ga=g_attn_norm.reshape(1, -1), gql=g_q_lora.reshape(1, -1), gkv=g_kv_lora.reshape(1, -1),
        gq=gq, gk=gk, gm=gm,
        gdq=jnp.tile(g_dsa_q, DSA_HEADS).reshape(1, -1), gdk=jnp.tile(g_dsa_k, DSA_HEADS).reshape(1, -1),
        bq=_block_mean_matrix([(0, MLA_NOPE), (MLA_NOPE, MLA_ROPE)]),
        b64=_block_mean_matrix([(0, DSA_HEAD_DIM), (DSA_HEAD_DIM, DSA_HEAD_DIM)]),
        bm=_block_mean_matrix([(0, MLA_ROPE)]),
    )


def _rope_tables(pos):
    half = MLA_ROPE // 2
    inv = ROPE_BASE ** (-jnp.arange(half, dtype=F32) / half)
    ang = pos.astype(F32)[:, None] * inv
    cos, sin = jnp.cos(ang), jnp.sin(ang)
    cos32 = jnp.concatenate([cos, cos], axis=1)
    sin32 = jnp.concatenate([-sin, sin], axis=1)
    n = pos.shape[0]
    ones = lambda w: jnp.ones((n, w), F32)
    zeros = lambda w: jnp.zeros((n, w), F32)
    cq = jnp.concatenate([ones(MLA_NOPE), cos32, ones(LANES - MLA_NOPE - MLA_ROPE)], axis=1)
    sq = jnp.concatenate([zeros(MLA_NOPE), sin32, zeros(LANES - MLA_NOPE - MLA_ROPE)], axis=1)
    cm = jnp.concatenate([cos32, ones(LANES - MLA_ROPE)], axis=1)
    sm = jnp.concatenate([sin32, zeros(LANES - MLA_ROPE)], axis=1)
    return cq, sq, cm, sm


def _project(x2d, shift, scale, pw, tables, tm, rows_per_mod):
    n = x2d.shape[0]
    nt = n // tm
    cq, sq, cm, sm = tables
    tpos = cq.shape[0] // tm
    const = lambda shape: pl.BlockSpec(shape, lambda i: (0,) * len(shape))
    row = lambda w: pl.BlockSpec((tm, w), lambda i: (i, 0))
    if rows_per_mod:
        per = rows_per_mod // tm
        mod_spec = pl.BlockSpec((None, 1, D_MODEL), lambda i: (i // per, 0, 0))
    else:
        mod_spec = row(D_MODEL)
    tab = pl.BlockSpec((tm, LANES), lambda i: (i % tpos, 0))
    in_specs = [row(D_MODEL), mod_spec, mod_spec, const((1, D_MODEL)), const((D_MODEL, _C_END)),
                const((1, Q_LORA)), const((Q_LORA, MLA_HEADS * LANES)), const((1, KV_LORA)),
                const((KV_LORA, MLA_HEADS * LANES)), const((1, MLA_HEADS * LANES)), const((1, MLA_HEADS * LANES)),
                const((1, LANES)), const((1, DSA_OUT)), const((1, DSA_OUT)), tab, tab, tab, tab,
                const((LANES, LANES)), const((LANES, LANES)), const((LANES, LANES))]
    widths = [(MLA_HEADS * LANES, BF16), (MLA_HEADS * LANES, BF16), (KV_LORA, F32), (KV_LORA, BF16), (LANES, F32),
              (DSA_OUT, BF16), (DSA_OUT, F32), (DSA_OUT, BF16), (DSA_OUT, F32), (DSA_OUT, BF16),
              (IDX_HEADS * IDX_DIM, BF16), (LANES, BF16)]
    outs = pl.pallas_call(
        _proj_kernel,
        grid=(nt,),
        in_specs=in_specs,
        out_specs=[row(w) for w, _ in widths],
        out_shape=[jax.ShapeDtypeStruct((n, w), dt) for w, dt in widths],
        compiler_params=_cparams(("arbitrary",)),
        name="project",
    )(x2d, shift, scale, pw["ga"], pw["win"], pw["gql"], pw["wqu"], pw["gkv"], pw["wuk"],
      pw["gq"], pw["gk"], pw["gm"], pw["gdq"], pw["gdk"], cq, sq, cm, sm, pw["bq"], pw["b64"], pw["bm"])
    names = ("qall", "kall", "ckv", "ckvb", "misc", "qd", "kd", "kdb", "vd", "vdb", "qi", "kid")
    return dict(zip(names, outs))


_MLA_TQ = 256
_MLA_TK = 256


def _mla_prompt_kernel(q_ref, k_ref, c_ref, wuv_ref, o_ref):
    tq, tk = _MLA_TQ, _MLA_TK
    i = pl.program_id(1)
    q0 = i * tq
    nkb = (q0 + tq + tk - 1) // tk
    row = q0 + lax.broadcasted_iota(I32, (tq, tk), 0)
    col0 = lax.broadcasted_iota(I32, (tq, tk), 1)
    heads = []
    for h in range(MLA_HEADS):
        qh = q_ref[:, h * LANES:(h + 1) * LANES]

        def body(j, carry, qh=qh, h=h):
            m, l, acc = carry
            k0 = pl.multiple_of(j * tk, tk)
            kh = k_ref[pl.ds(k0, tk), h * LANES:(h + 1) * LANES]
            s = _nt_dot(qh, kh) * MLA_SCALE
            s = jnp.where(col0 + k0 <= row, s, NEG_INF)
            m_new = jnp.maximum(m, jnp.max(s, axis=1, keepdims=True))
            alpha = jnp.exp(m - m_new)
            p = jnp.exp(s - m_new)
            l = alpha * l + jnp.sum(p, axis=1, keepdims=True)
            acc = alpha * acc + jnp.dot(p.astype(BF16), c_ref[pl.ds(k0, tk), :], preferred_element_type=F32)
            return m_new, l, acc

        init = (jnp.full((tq, 1), NEG_INF, F32), jnp.zeros((tq, 1), F32), jnp.zeros((tq, KV_LORA), F32))
        m, l, acc = lax.fori_loop(0, nkb, body, init)
        heads.append((acc / l).astype(BF16))
    for pr in range(MLA_HEADS // 2):
        o = (jnp.dot(heads[2 * pr], wuv_ref[2 * pr], preferred_element_type=F32)
             + jnp.dot(heads[2 * pr + 1], wuv_ref[2 * pr + 1], preferred_element_type=F32))
        o_ref[:, pr * LANES:(pr + 1) * LANES] = o.astype(BF16)


def _pad_wuv(w_uv):
    w = jnp.transpose(w_uv, (1, 0, 2))
    even = jnp.pad(w, ((0, 0), (0, 0), (0, LANES - MLA_V)))
    odd = jnp.pad(w, ((0, 0), (0, 0), (LANES - MLA_V, 0)))
    is_odd = (jnp.arange(MLA_HEADS) % 2 == 1)[:, None, None]
    return jnp.where(is_odd, odd, even).astype(BF16)


def _mla_prompt(qall, kall, ckvb, wuv_pad):
    b, t, _ = qall.shape
    tq = _MLA_TQ
    return pl.pallas_call(
        _mla_prompt_kernel,
        grid=(b, t // tq),
        in_specs=[pl.BlockSpec((None, tq, MLA_HEADS * LANES), lambda bi, i: (bi, i, 0)),
                  pl.BlockSpec((None, t, MLA_HEADS * LANES), lambda bi, i: (bi, 0, 0)),
                  pl.BlockSpec((None, t, KV_LORA), lambda bi, i: (bi, 0, 0)),
                  pl.BlockSpec((MLA_HEADS, KV_LORA, LANES), lambda bi, i: (0, 0, 0))],
        out_specs=pl.BlockSpec((None, tq, MLA_OUT), lambda bi, i: (bi, i, 0)),
        out_shape=jax.ShapeDtypeStruct((b, t, MLA_OUT), BF16),
        compiler_params=_cparams(("arbitrary", "arbitrary")),
        name="mla_prompt",
    )(qall, kall, ckvb, wuv_pad)


_DSA_T = 128


def _sortable_key(score):
    bits = pltpu.bitcast(score, I32)
    key = jnp.where(bits < 0, bits ^ jnp.int32(0x7FFFFFFF), bits)
    return jnp.where(score == 0.0, 0, key)


def _bias_tiles(rb_ref, tz_ref):
    t = _DSA_T
    r = lax.broadcasted_iota(I32, (t, t), 0)
    c = lax.broadcasted_iota(I32, (t, t), 1)
    exact = REL_BUCKETS // 2
    for which in range(2):
        d = jnp.maximum(r - c + t * which, 0)
        logd = jnp.log(jnp.maximum(d, 1).astype(F32) / exact) / math.log(REL_MAX_DIST / exact)
        far = jnp.minimum(exact + (logd * (REL_BUCKETS - exact)).astype(I32), REL_BUCKETS - 1)
        bucket = jnp.where(d < exact, d, far)
        for h in range(DSA_HEADS):
            tile = jnp.zeros((t, t), F32)
            for bk in range(REL_BUCKETS):
                tile = jnp.where(bucket == bk, rb_ref[bk, h], tile)
            tz_ref[h, which] = tile


def _select_mask(keys_ref, mask_ref, tri_ref, nb, n_sel, rows):
    t = _DSA_T

    def count(pred):
        def body(j, acc):
            kb = keys_ref[:, pl.ds(pl.multiple_of(j * t, t), t)]
            return acc + jnp.where(pred(kb), 1.0, 0.0)
        acc = lax.fori_loop(0, nb, body, jnp.zeros((rows, t), F32))
        return jnp.sum(acc, axis=1, keepdims=True)

    kf = jnp.float32(n_sel)
    zero = jnp.zeros((rows, 1), I32)
    thr = jnp.where(count(lambda kb: kb >= zero) >= kf, zero, jnp.full((rows, 1), INT_MIN, I32))

    def bit_body(bi, thr):
        cand = thr + lax.shift_left(jnp.int32(1), 30 - bi)
        return jnp.where(count(lambda kb: kb >= cand) >= kf, cand, thr)

    thr = lax.fori_loop(0, 31, bit_body, thr)
    need = kf - count(lambda kb: kb > thr)

    def mask_body(j, carry):
        k0 = pl.multiple_of(j * t, t)
        kb = keys_ref[:, pl.ds(k0, t)]
        eq = kb == thr
        eqf = jnp.where(eq, 1.0, 0.0)
        before = jnp.dot(eqf.astype(BF16), tri_ref[...], preferred_element_type=F32) + carry
        keep = (kb > thr) | (eq & (before < need))
        keep = keep & (kb > INT_MIN)
        mask_ref[:, pl.ds(k0, t)] = jnp.where(keep, 0.0, NEG_INF)
        return carry + jnp.sum(eqf, axis=1, keepdims=True)

    lax.fori_loop(0, nb, mask_body, jnp.zeros((rows, 1), F32))


def _dsa_prompt_kernel(rb_ref, qi_ref, misc_ref, kid_ref, qd_ref, kd_ref, vd_ref, tri_ref, o_ref,
                       keys_ref, mask_ref, tz_ref, *, n_sel):
    t = _DSA_T
    bi = pl.program_id(0)
    i = pl.program_id(1)
    nb = i + 1

    @pl.when((bi == 0) & (i == 0))
    def _():
        _bias_tiles(rb_ref, tz_ref)

    row = i * t + lax.broadcasted_iota(I32, (t, t), 0)
    col0 = lax.broadcasted_iota(I32, (t, t), 1)
    lane = lax.broadcasted_iota(I32, (t, LANES), 1)
    low = lane < DSA_HEAD_DIM
    wi = misc_ref[...]

    def idx_body(j, _):
        k0 = pl.multiple_of(j * t, t)
        kk = kid_ref[pl.ds(k0, t), :]
        acc = jnp.zeros((t, t), F32)
        for h in range(IDX_HEADS):
            qs = qi_ref[:, (h // 2) * LANES:(h // 2 + 1) * LANES]
            qm = jnp.where(low if h % 2 == 0 else ~low, qs, jnp.zeros_like(qs))
            r = jnp.maximum(_nt_dot(qm, kk) * (IDX_DIM ** -0.5), 0.0)
            acc = acc + wi[:, _M_WI + h:_M_WI + h + 1] * r
        keys_ref[:, pl.ds(k0, t)] = jnp.where(col0 + k0 <= row, _sortable_key(acc), INT_MIN)
        return 0

    lax.fori_loop(0, nb, idx_body, 0)
    _select_mask(keys_ref, mask_ref, tri_ref, nb, n_sel, t)

    for pr in range(DSA_HEADS // 2):
        qs = qd_ref[:, pr * LANES:(pr + 1) * LANES]
        halves = []
        for half in range(2):
            h = 2 * pr + half
            qm = jnp.where(low if half == 0 else ~low, qs, jnp.zeros_like(qs))
            far_bias = rb_ref[REL_BUCKETS - 1, h]

            def body(j, carry, qm=qm, h=h, pr=pr, far_bias=far_bias):
                m, l, acc = carry
                k0 = pl.multiple_of(j * t, t)
                s = _nt_dot(qm, kd_ref[pl.ds(k0, t), pr * LANES:(pr + 1) * LANES]) * DSA_SCALE
                bias = jnp.where(j == i, tz_ref[h, 0], jnp.where(j == i - 1, tz_ref[h, 1], far_bias))
                s = s + bias + mask_ref[:, pl.ds(k0, t)]
                m_new = jnp.maximum(m, jnp.max(s, axis=1, keepdims=True))
                m_safe = jnp.where(m_new == NEG_INF, 0.0, m_new)
                alpha = jnp.exp(m - m_safe)
                p = jnp.exp(s - m_safe)
                l = alpha * l + jnp.sum(p, axis=1, keepdims=True)
                acc = alpha * acc + jnp.dot(p.astype(BF16), vd_ref[pl.ds(k0, t), pr * LANES:(pr + 1) * LANES],
                                            preferred_element_type=F32)
                return m_new, l, acc

            init = (jnp.full((t, 1), NEG_INF, F32), jnp.zeros((t, 1), F32), jnp.zeros((t, LANES), F32))
            m, l, acc = lax.fori_loop(0, nb, body, init)
            halves.append(acc / l)
        o_ref[:, pr * LANES:(pr + 1) * LANES] = jnp.where(low, halves[0], halves[1]).astype(BF16)


def _strict_upper(n):
    return jnp.asarray(np.triu(np.ones((n, n), np.float32), 1), BF16)


def _dsa_prompt(rel_bias, qi, misc, kid, qd, kdb, vdb):
    b, t, _ = qi.shape
    tq = _DSA_T
    n_sel = min(IDX_TOPK_MAX, t // 4)
    blk = lambda w: pl.BlockSpec((None, tq, w), lambda bi, i: (bi, i, 0))
    full = lambda w: pl.BlockSpec((None, t, w), lambda bi, i: (bi, 0, 0))
    return pl.pallas_call(
        functools.partial(_dsa_prompt_kernel, n_sel=n_sel),
        grid=(b, t // tq),
        in_specs=[pl.BlockSpec(memory_space=pltpu.SMEM),
                  blk(IDX_HEADS * IDX_DIM), blk(LANES), full(LANES), blk(DSA_OUT), full(DSA_OUT), full(DSA_OUT),
                  pl.BlockSpec((tq, tq), lambda bi, i: (0, 0))],
        out_specs=blk(DSA_OUT),
        out_shape=jax.ShapeDtypeStruct((b, t, DSA_OUT), BF16),
        scratch_shapes=[pltpu.VMEM((tq, t), I32), pltpu.VMEM((tq, t), F32),
                        pltpu.VMEM((DSA_HEADS, 2, tq, tq), F32)],
        compiler_params=_cparams(("arbitrary", "arbitrary")),
        name="dsa_prompt",
    )(rel_bias, qi, misc, kid, qd, kdb, vdb, _strict_upper(tq))


def _x_rms_norm(x, g):
    xf = x.astype(F32)
    y = xf * lax.rsqrt(jnp.mean(xf * xf, axis=-1, keepdims=True) + EPS)
    return (y * g.astype(F32)).astype(x.dtype)


def _x_t5_bucket(dist):
    exact = REL_BUCKETS // 2
    d = jnp.maximum(dist, 0)
    logd = jnp.log(jnp.maximum(d, 1).astype(F32) / exact) / math.log(REL_MAX_DIST / exact)
    far = jnp.minimum(exact + (logd * (REL_BUCKETS - exact)).astype(I32), REL_BUCKETS - 1)
    return jnp.where(d < exact, d, far)


def _x_mla_keys(c_kv, w_uk, g_kn):
    return _x_rms_norm(jnp.einsum('...c,chd->...hd', c_kv, w_uk), g_kn)


def _x_mla_attend(q_nope, q_rope, k_nope, k_rope, c_kv, key_pos, q_pos, w_uv):
    s = (jnp.einsum('qhd,shd->hqs', q_nope, k_nope).astype(F32)
         + jnp.einsum('qhr,sr->hqs', q_rope, k_rope).astype(F32)) * MLA_SCALE
    s = jnp.where((key_pos[None, :] <= q_pos[:, None])[None], s, -jnp.inf)
    p = jax.nn.softmax(s, axis=-1).astype(c_kv.dtype)
    o_lat = jnp.einsum('hqs,sc->qhc', p, c_kv)
    return jnp.einsum('qhc,chv->qhv', o_lat, w_uv).reshape(q_nope.shape[0], MLA_OUT)


def _x_indexer_select(q_idx, w_idx, k_idx, key_pos, q_pos, n_sel):
    r = jax.nn.relu(jnp.einsum('qhd,sd->qhs', q_idx, k_idx).astype(F32) * IDX_DIM ** -0.5)
    score = jnp.einsum('qh,qhs->qs', w_idx.astype(F32), r)
    causal = key_pos[None, :] <= q_pos[:, None]
    _, sel = lax.top_k(jnp.where(causal, score, -jnp.inf), n_sel)
    return sel, key_pos[sel] <= q_pos[:, None]


def _x_dsa_attend(q, k_sel, v_sel, sel_pos, valid, q_pos, rel_bias):
    s = jnp.einsum('qhd,qkhd->hqk', q, k_sel).astype(F32) * DSA_SCALE
    bias = rel_bias[_x_t5_bucket(q_pos[:, None] - sel_pos)].astype(F32)
    s = jnp.where(valid[None], s + jnp.transpose(bias, (2, 0, 1)), -jnp.inf)
    p = jax.nn.softmax(s, axis=-1).astype(v_sel.dtype)
    return jnp.einsum('hqk,qkhd->qhd', p, v_sel).reshape(q.shape[0], DSA_OUT)


def _x_gather_rows(cache, pages, new, idx, past):
    p = jnp.minimum(idx, past - 1)
    old = cache[pages[p // PAGE_SIZE], p % PAGE_SIZE]
    fresh = new[jnp.clip(idx - past, 0, new.shape[0] - 1)]
    in_past = (idx < past).reshape(idx.shape + (1,) * (old.ndim - idx.ndim))
    return jnp.where(in_past, old, fresh)


def _x_mixers_sample(m, cache_lat, cache_kr, cache_k, cache_v, cache_idx, page_table, w_uk, w_uv, g_mla_kn, rel_bias):
    q_nope, q_rope, c_kv, k_rope, q_d, k_d, v_d, q_i, k_i, w_i = m
    T = c_kv.shape[1]
    past = page_table.shape[1] * PAGE_SIZE
    L = past + T
    key_pos = jnp.arange(L)
    q_pos = past + jnp.arange(T)
    n_sel = min(IDX_TOPK_MAX, L // 4)

    def past_rows(cache, pages):
        return cache[pages].reshape((past,) + cache.shape[2:])

    def mla_seq(args):
        qn, qr, cn, krn, pages = args
        lat = jnp.concatenate([past_rows(cache_lat, pages), cn], axis=0)
        kr = jnp.concatenate([past_rows(cache_kr, pages), krn], axis=0)
        return _x_mla_attend(qn, qr, _x_mla_keys(lat, w_uk, g_mla_kn), kr, lat, key_pos, q_pos, w_uv)

    o_mla = lax.map(mla_seq, (q_nope, q_rope, c_kv, k_rope, page_table))

    def dsa_seq(qd, kd, vd, qi, ki, wi, pages):
        keys_idx = jnp.concatenate([past_rows(cache_idx, pages), ki], axis=0)
        sel, valid = _x_indexer_select(qi, wi, keys_idx, key_pos, q_pos, n_sel)
        k_sel = _x_gather_rows(cache_k, pages, kd, sel, past)
        v_sel = _x_gather_rows(cache_v, pages, vd, sel, past)
        return _x_dsa_attend(qd, k_sel, v_sel, sel, valid, q_pos, rel_bias)

    o_dsa = jax.vmap(dsa_seq)(q_d, k_d, v_d, q_i, k_i, w_i, page_table)
    return jnp.concatenate([o_mla, o_dsa], axis=-1)


def _x_swiglu(x, w_gu, w_down):
    g, u = jnp.split(x @ w_gu, 2, axis=-1)
    return (jax.nn.silu(g) * u) @ w_down


def _x_routed_experts(x, eidx, gate, w_e_gu, w_e_down):
    N, D = x.shape
    A = N * TOP_K
    blk = int(min(128, max(8, A // N_EXPERTS)))
    flat_e = eidx.reshape(-1)
    order = jnp.argsort(flat_e)
    e_sorted = flat_e[order]
    counts = jnp.zeros((N_EXPERTS,), I32).at[flat_e].add(1)
    start = jnp.cumsum(counts) - counts
    padded = (counts + blk - 1) // blk * blk
    pad_end = jnp.cumsum(padded)
    slot = (pad_end - padded)[e_sorted] + jnp.arange(A) - start[e_sorted]
    n_blocks = -(-(A + N_EXPERTS * (blk - 1)) // blk)
    n_slots = n_blocks * blk
    slot_token = jnp.full((n_slots,), N, I32).at[slot].set((order // TOP_K).astype(I32))
    slot_gate = jnp.zeros((n_slots,), x.dtype).at[slot].set(gate.reshape(-1)[order])
    block_expert = jnp.minimum(
        jnp.searchsorted(pad_end, jnp.arange(n_blocks) * blk, side='right'), N_EXPERTS - 1)
    x_pad = jnp.concatenate([x, jnp.zeros((1, D), x.dtype)], axis=0)
    xb = x_pad[slot_token].reshape(n_blocks, blk, D)
    yb = lax.map(lambda a: _x_swiglu(a[0], w_e_gu[a[1]], w_e_down[a[1]]), (xb, block_expert))
    y = jnp.zeros((N + 1, D), x.dtype).at[slot_token].add(yb.reshape(n_slots, D) * slot_gate[:, None])
    return y[:N]


def _x_moe(h, w_router, b_router, w_e_gu, w_e_down, w_s_gu, w_s_down):
    B, T, D = h.shape
    xt = h.reshape(B * T, D)
    N = xt.shape[0]
    scores = jax.nn.sigmoid((xt @ w_router).astype(F32))
    biased = scores + b_router.astype(F32)
    grp_score = lax.top_k(biased.reshape(N, N_GROUPS, N_EXPERTS // N_GROUPS), 2)[0].sum(-1)
    _, top_g = lax.top_k(grp_score, TOPK_GROUPS)
    gmask = jnp.any(top_g[..., None] == jnp.arange(N_GROUPS), axis=1)
    masked = jnp.where(jnp.repeat(gmask, N_EXPERTS // N_GROUPS, axis=1), biased, -jnp.inf)
    _, eidx = lax.top_k(masked, TOP_K)
    w = jnp.take_along_axis(scores, eidx, axis=1)
    w = w / jnp.sum(w, axis=-1, keepdims=True) * ROUTED_SCALE
    routed = _x_routed_experts(xt, eidx, w.astype(xt.dtype), w_e_gu, w_e_down)
    return (routed + _x_swiglu(xt, w_s_gu, w_s_down)).reshape(B, T, D)


def _unpack_heads(a, n, width, lo, hi):
    return a.reshape(n, -1, width)[:, :, lo:hi]


def kernel(x_prompt, x_sample, cache_mla_latent, cache_mla_krope, cache_dsa_k, cache_dsa_v, cache_idx_k, page_table, c_prompt, c_sample, rel_bias, w_ada, b_ada, g_attn_norm, w_in, g_q_lora, w_q_up, g_kv_lora, w_kv_up, g_mla_qn, g_mla_qr, g_mla_kn, g_mla_kr, g_dsa_q, g_dsa_k, w_out, g_ffn_norm, w_router, b_router, w_e_gu, w_e_down, w_s_gu, w_s_down):
    depth = w_ada.shape[0]
    assert depth == 1, "single-layer trunk"
    l = 0
    B, T, D = x_prompt.shape
    DB, TS, _ = x_sample.shape
    past = page_table.shape[1] * PAGE_SIZE

    w_kv = w_kv_up[l].reshape(KV_LORA, MLA_HEADS, MLA_NOPE + MLA_V)
    w_uk, w_uv = w_kv[..., :MLA_NOPE], w_kv[..., MLA_NOPE:]

    mod = _adaln(jnp.concatenate([c_prompt, c_sample], axis=0), w_ada[l].astype(BF16), b_ada[l])
    mod_p = [m.reshape(B, 1, D) for m in jnp.split(mod[:B], 6, axis=-1)]
    mod_s = jnp.split(mod[B:], 6, axis=-1)

    pw = _prep_proj_weights(w_in[l], g_q_lora[l], w_q_up[l], g_kv_lora[l], w_uk, g_mla_qn[l], g_mla_qr[l],
                            g_mla_kn[l], g_mla_kr[l], g_dsa_q[l], g_dsa_k[l], g_attn_norm[l])
    pp = _project(x_prompt.reshape(B * T, D), mod_p[0], mod_p[1], pw, _rope_tables(jnp.arange(T)), 256, T)
    ps = _project(x_sample.reshape(DB * TS, D), mod_s[0], mod_s[1], pw,
                  _rope_tables(jnp.tile(past + jnp.arange(TS), DB)), DB * TS, 0)

    r3 = lambda a: a.reshape(B, T, a.shape[-1])
    o_mla_p = _mla_prompt(r3(pp["qall"]), r3(pp["kall"]), r3(pp["ckvb"]), _pad_wuv(w_uv))
    o_dsa_p = _dsa_prompt(rel_bias, r3(pp["qi"]), r3(pp["misc"]), r3(pp["kid"]), r3(pp["qd"]), r3(pp["kdb"]),
                          r3(pp["vdb"]))
    o_p = jnp.concatenate([o_mla_p, o_dsa_p], axis=-1).astype(F32)

    ns = DB * TS
    qs = ps["qall"].astype(F32)
    ms = (_unpack_heads(qs, ns, LANES, 0, MLA_NOPE).reshape(DB, TS, MLA_HEADS, MLA_NOPE),
          _unpack_heads(qs, ns, LANES, MLA_NOPE, MLA_NOPE + MLA_ROPE).reshape(DB, TS, MLA_HEADS, MLA_ROPE),
          ps["ckv"].reshape(DB, TS, KV_LORA), ps["misc"][:, :MLA_ROPE].reshape(DB, TS, MLA_ROPE),
          ps["qd"].astype(F32).reshape(DB, TS, DSA_HEADS, DSA_HEAD_DIM),
          ps["kd"].reshape(DB, TS, DSA_HEADS, DSA_HEAD_DIM), ps["vd"].reshape(DB, TS, DSA_HEADS, DSA_HEAD_DIM),
          ps["qi"].astype(F32).reshape(DB, TS, IDX_HEADS, IDX_DIM),
          ps["misc"][:, _M_KI:_M_KI + IDX_DIM].reshape(DB, TS, IDX_DIM),
          ps["misc"][:, _M_WI:_M_WI + IDX_HEADS].reshape(DB, TS, IDX_HEADS))
    o_s = _x_mixers_sample(ms, cache_mla_latent[l], cache_mla_krope[l], cache_dsa_k[l], cache_dsa_v[l],
                           cache_idx_k[l], page_table, w_uk, w_uv, g_mla_kn[l], rel_bias)

    ffn = functools.partial(_x_moe, w_router=w_router[l], b_router=b_router[l], w_e_gu=w_e_gu[l],
                            w_e_down=w_e_down[l], w_s_gu=w_s_gu[l], w_s_down=w_s_down[l])
    xp = x_prompt + mod_p[2] * (o_p @ w_out[l])
    xs = x_sample + mod_s[2][:, None, :] * (o_s @ w_out[l])
    xp = xp + mod_p[5] * ffn(_x_rms_norm(xp, g_ffn_norm[l]) * (1 + mod_p[4]) + mod_p[3])
    xs = xs + mod_s[5][:, None, :] * ffn(_x_rms_norm(xs, g_ffn_norm[l]) * (1 + mod_s[4][:, None, :])
                                         + mod_s[3][:, None, :])

    def caches(p, nb, nt):
        return (p["ckv"].reshape(1, nb, nt, KV_LORA),
                p["misc"][:, :MLA_ROPE].reshape(1, nb, nt, MLA_ROPE),
                p["kd"].reshape(1, nb, nt, DSA_HEADS, DSA_HEAD_DIM),
                p["vd"].reshape(1, nb, nt, DSA_HEADS, DSA_HEAD_DIM),
                p["misc"][:, _M_KI:_M_KI + IDX_DIM].reshape(1, nb, nt, IDX_DIM))

    return (xp, xs) + caches(pp, B, T) + caches(ps, DB, TS)
```

```python
import functools
import math

import jax
import jax.numpy as jnp
import numpy as np
from jax import lax
from jax.experimental import pallas as pl
from jax.experimental.pallas import tpu as pltpu

F32 = jnp.float32
BF16 = jnp.bfloat16
I32 = jnp.int32

D_MODEL = 1024
PAGE_SIZE = 128
EPS = 1e-6
MLA_HEADS = 8
MLA_NOPE = 64
MLA_ROPE = 32
MLA_V = 64
Q_LORA = 256
KV_LORA = 128
ROPE_BASE = 10000.0
MLA_SCALE = (MLA_NOPE + MLA_ROPE) ** -0.5
DSA_HEADS = 8
DSA_HEAD_DIM = 64
DSA_SCALE = DSA_HEAD_DIM ** -0.5
IDX_HEADS = 8
IDX_DIM = 64
IDX_TOPK_MAX = 256
REL_BUCKETS = 32
REL_MAX_DIST = 128
N_EXPERTS = 64
TOP_K = 6
N_GROUPS = 8
TOPK_GROUPS = 4
D_EXPERT = 256
D_SHARED = 256
ROUTED_SCALE = 2.5
MLA_OUT = MLA_HEADS * MLA_V
DSA_OUT = DSA_HEADS * DSA_HEAD_DIM
IN_SIZES = (Q_LORA, KV_LORA, MLA_ROPE, DSA_OUT, DSA_OUT, DSA_OUT, IDX_HEADS * IDX_DIM, IDX_DIM, IDX_HEADS)

LANES = 128
INT_MIN = -(2 ** 31)
NEG_INF = float("-inf")
VMEM_LIMIT = 56 * 1024 * 1024


def _cparams(sem):
    return pltpu.CompilerParams(dimension_semantics=sem, vmem_limit_bytes=VMEM_LIMIT)


def _split_dot(x, m01, passes=3):
    acc = None
    r = x
    for p in range(passes):
        hi = r.astype(BF16)
        part = jnp.dot(hi, m01, preferred_element_type=F32)
        acc = part if acc is None else acc + part
        if p + 1 < passes:
            r = r - hi.astype(F32)
    return acc


def _group_mean(sq, bmat):
    outs = [_split_dot(sq[:, s * LANES:(s + 1) * LANES], bmat) for s in range(sq.shape[1] // LANES)]
    return outs[0] if len(outs) == 1 else jnp.concatenate(outs, axis=1)


def _rope_slabs(x, cos, sin):
    lane = lax.broadcasted_iota(I32, (x.shape[0], LANES), 1)
    first_half = (lane % MLA_ROPE) < (MLA_ROPE // 2)
    outs = []
    for s in range(x.shape[1] // LANES):
        xs = x[:, s * LANES:(s + 1) * LANES]
        rot = jnp.where(first_half, pltpu.roll(xs, LANES - MLA_ROPE // 2, 1), pltpu.roll(xs, MLA_ROPE // 2, 1))
        outs.append(xs * cos + rot * sin)
    return outs[0] if len(outs) == 1 else jnp.concatenate(outs, axis=1)


def _nt_dot(a, b):
    return lax.dot_general(a, b, (((1,), (1,)), ((), ())), preferred_element_type=F32)


def _adaln_kernel(c_ref, w_ref, b_ref, o_ref):
    c = c_ref[...]
    s = (c * jax.nn.sigmoid(c)).astype(BF16)
    o_ref[...] = jnp.dot(s, w_ref[...], preferred_element_type=F32) + b_ref[...]


def _adaln(c, w_bf, b):
    rows = c.shape[0]
    n = w_bf.shape[1]
    tn = 1536
    return pl.pallas_call(
        _adaln_kernel,
        grid=(n // tn,),
        in_specs=[pl.BlockSpec((rows, D_MODEL), lambda j: (0, 0)),
                  pl.BlockSpec((D_MODEL, tn), lambda j: (0, j)),
                  pl.BlockSpec((1, tn), lambda j: (0, j))],
        out_specs=pl.BlockSpec((rows, tn), lambda j: (0, j)),
        out_shape=jax.ShapeDtypeStruct((rows, n), F32),
        compiler_params=_cparams(("arbitrary",)),
        name="adaln",
    )(c, w_bf, b.reshape(1, n))


_C_QLAT = 0
_C_KV = _C_QLAT + Q_LORA
_C_QD = _C_KV + KV_LORA
_C_KD = _C_QD + DSA_OUT
_C_VD = _C_KD + DSA_OUT
_C_QI = _C_VD + DSA_OUT
_C_MISC = _C_QI + IDX_HEADS * IDX_DIM
_C_KIDUP = _C_MISC + LANES
_C_END = _C_KIDUP + LANES
_M_KI = MLA_ROPE
_M_WI = MLA_ROPE + IDX_DIM


def _proj_kernel(x_ref, sh_ref, sc_ref, ga_ref, win_ref, gql_ref, wqu_ref, gkv_ref, wuk_ref,
                 gq_ref, gk_ref, gm_ref, gdq_ref, gdk_ref, cq_ref, sq_ref, cm_ref, sm_ref,
                 bq_ref, b64_ref, bm_ref,
                 qall_ref, kall_ref, ckv_ref, ckvb_ref, misc_ref, qd_ref, kd_ref, kdb_ref, vd_ref, vdb_ref,
                 qi_ref, kid_ref):
    x = x_ref[...]
    xn = x * lax.rsqrt(jnp.mean(x * x, axis=-1, keepdims=True) + EPS) * ga_ref[...]
    h = xn * (1.0 + sc_ref[...]) + sh_ref[...]
    p = jnp.dot(h.astype(BF16), win_ref[...], preferred_element_type=F32)

    ql = p[:, _C_QLAT:_C_KV]
    qln = ql * lax.rsqrt(jnp.mean(ql * ql, axis=-1, keepdims=True) + EPS) * gql_ref[...]
    q = jnp.dot(qln.astype(BF16), wqu_ref[...], preferred_element_type=F32)
    qn = q * lax.rsqrt(_group_mean(q * q, bq_ref[...]) + EPS) * gq_ref[...]
    qall_ref[...] = _rope_slabs(qn, cq_ref[...], sq_ref[...]).astype(BF16)

    kv = p[:, _C_KV:_C_QD]
    ckv = kv * lax.rsqrt(jnp.mean(kv * kv, axis=-1, keepdims=True) + EPS) * gkv_ref[...]
    ckv_ref[...] = ckv
    ckvb = ckv.astype(BF16)
    ckvb_ref[...] = ckvb
    kn = jnp.dot(ckvb, wuk_ref[...], preferred_element_type=F32)
    kn = kn * lax.rsqrt(_group_mean(kn * kn, bq_ref[...]) + EPS) * gk_ref[...]

    m = p[:, _C_MISC:_C_KIDUP]
    lane = lax.broadcasted_iota(I32, m.shape, 1)
    is_kr = lane < MLA_ROPE
    mm = _split_dot(m * m, bm_ref[...])
    mn = jnp.where(is_kr, m * lax.rsqrt(mm + EPS) * gm_ref[...], m)
    mr = _rope_slabs(mn, cm_ref[...], sm_ref[...])
    is_wi = (lane >= _M_WI) & (lane < _M_WI + IDX_HEADS)
    misc_ref[...] = jnp.where(is_wi, mr * (IDX_HEADS ** -0.5), mr)
    kr_placed = jnp.where((lane >= MLA_NOPE) & (lane < MLA_NOPE + MLA_ROPE), pltpu.roll(mr, MLA_NOPE, 1), 0.0)
    kall_ref[...] = jnp.concatenate(
        [kn[:, s * LANES:(s + 1) * LANES] + kr_placed for s in range(MLA_HEADS)], axis=1).astype(BF16)

    qd = p[:, _C_QD:_C_KD]
    qd_ref[...] = (qd * lax.rsqrt(_group_mean(qd * qd, b64_ref[...]) + EPS) * gdq_ref[...]).astype(BF16)
    kd = p[:, _C_KD:_C_VD]
    kdn = kd * lax.rsqrt(_group_mean(kd * kd, b64_ref[...]) + EPS) * gdk_ref[...]
    kd_ref[...] = kdn
    kdb_ref[...] = kdn.astype(BF16)
    vd = p[:, _C_VD:_C_QI]
    vd_ref[...] = vd
    vdb_ref[...] = vd.astype(BF16)
    qi_ref[...] = p[:, _C_QI:_C_MISC].astype(BF16)
    kid_ref[...] = p[:, _C_KIDUP:_C_END].astype(BF16)


def _block_mean_matrix(blocks):
    m = np.zeros((LANES, LANES), np.float32)
    for start, size in blocks:
        m[start:start + size, start:start + size] = 1.0 / size
    return jnp.asarray(m, BF16)


def _prep_proj_weights(w_in, g_q_lora, w_q_up, g_kv_lora, w_uk, g_mla_qn, g_mla_qr, g_mla_kn, g_mla_kr,
                       g_dsa_q, g_dsa_k, g_attn_norm):
    offs = np.cumsum((0,) + IN_SIZES)
    sec = lambda k: w_in[:, offs[k]:offs[k + 1]]
    zeros = lambda n: jnp.zeros((D_MODEL, n), w_in.dtype)
    misc = jnp.concatenate([sec(2), sec(7), sec(8), zeros(LANES - MLA_ROPE - IDX_DIM - IDX_HEADS)], axis=1)
    w_in_r = jnp.concatenate([sec(0), sec(1), sec(3), sec(4), sec(5), sec(6), misc, sec(7), sec(7)], axis=1)
    wq = w_q_up.reshape(Q_LORA, MLA_HEADS, MLA_NOPE + MLA_ROPE)
    wq = jnp.pad(wq, ((0, 0), (0, 0), (0, LANES - MLA_NOPE - MLA_ROPE))).reshape(Q_LORA, MLA_HEADS * LANES)
    wk = jnp.pad(w_uk, ((0, 0), (0, 0), (0, LANES - MLA_NOPE))).reshape(KV_LORA, MLA_HEADS * LANES)
    pad1 = lambda v, n: jnp.pad(v, (0, n - v.shape[0]))
    gq = jnp.tile(pad1(jnp.concatenate([g_mla_qn, g_mla_qr]), LANES), MLA_HEADS).reshape(1, -1)
    gk = jnp.tile(pad1(g_mla_kn, LANES), MLA_HEADS).reshape(1, -1)
    gm = jnp.concatenate([g_mla_kr, jnp.ones((LANES - MLA_ROPE,), F32)]).reshape(1, -1)
    return dict(
        win=w_in_r.astype(BF16), wqu=wq.astype(BF16), wuk=wk.astype(BF16),
        ga=g_attn_norm.reshape(1, -1), gql=g_q_lora.reshape(1, -1), gkv=g_kv_lora.reshape(1, -1),
        gq=gq, gk=gk, gm=gm,
        gdq=jnp.tile(g_dsa_q, DSA_HEADS).reshape(1, -1), gdk=jnp.tile(g_dsa_k, DSA_HEADS).reshape(1, -1),
        bq=_block_mean_matrix([(0, MLA_NOPE), (MLA_NOPE, MLA_ROPE)]),
        b64=_block_mean_matrix([(0, DSA_HEAD_DIM), (DSA_HEAD_DIM, DSA_HEAD_DIM)]),
        bm=_block_mean_matrix([(0, MLA_ROPE)]),
    )


def _rope_tables(pos):
    half = MLA_ROPE // 2
    inv = ROPE_BASE ** (-jnp.arange(half, dtype=F32) / half)
    ang = pos.astype(F32)[:, None] * inv
    cos, sin = jnp.cos(ang), jnp.sin(ang)
    cos32 = jnp.concatenate([cos, cos], axis=1)
    sin32 = jnp.concatenate([-sin, sin], axis=1)
    n = pos.shape[0]
    ones = lambda w: jnp.ones((n, w), F32)
    zeros = lambda w: jnp.zeros((n, w), F32)
    cq = jnp.concatenate([ones(MLA_NOPE), cos32, ones(LANES - MLA_NOPE - MLA_ROPE)], axis=1)
    sq = jnp.concatenate([zeros(MLA_NOPE), sin32, zeros(LANES - MLA_NOPE - MLA_ROPE)], axis=1)
    cm = jnp.concatenate([cos32, ones(LANES - MLA_ROPE)], axis=1)
    sm = jnp.concatenate([sin32, zeros(LANES - MLA_ROPE)], axis=1)
    return cq, sq, cm, sm


def _project(x2d, shift, scale, pw, tables, tm, rows_per_mod):
    n = x2d.shape[0]
    nt = n // tm
    cq, sq, cm, sm = tables
    tpos = cq.shape[0] // tm
    const = lambda shape: pl.BlockSpec(shape, lambda i: (0,) * len(shape))
    row = lambda w: pl.BlockSpec((tm, w), lambda i: (i, 0))
    if rows_per_mod:
        per = rows_per_mod // tm
        mod_spec = pl.BlockSpec((None, 1, D_MODEL), lambda i: (i // per, 0, 0))
    else:
        mod_spec = row(D_MODEL)
    tab = pl.BlockSpec((tm, LANES), lambda i: (i % tpos, 0))
    in_specs = [row(D_MODEL), mod_spec, mod_spec, const((1, D_MODEL)), const((D_MODEL, _C_END)),
                const((1, Q_LORA)), const((Q_LORA, MLA_HEADS * LANES)), const((1, KV_LORA)),
                const((KV_LORA, MLA_HEADS * LANES)), const((1, MLA_HEADS * LANES)), const((1, MLA_HEADS * LANES)),
                const((1, LANES)), const((1, DSA_OUT)), const((1, DSA_OUT)), tab, tab, tab, tab,
                const((LANES, LANES)), const((LANES, LANES)), const((LANES, LANES))]
    widths = [(MLA_HEADS * LANES, BF16), (MLA_HEADS * LANES, BF16), (KV_LORA, F32), (KV_LORA, BF16), (LANES, F32),
              (DSA_OUT, BF16), (DSA_OUT, F32), (DSA_OUT, BF16), (DSA_OUT, F32), (DSA_OUT, BF16),
              (IDX_HEADS * IDX_DIM, BF16), (LANES, BF16)]
    outs = pl.pallas_call(
        _proj_kernel,
        grid=(nt,),
        in_specs=in_specs,
        out_specs=[row(w) for w, _ in widths],
        out_shape=[jax.ShapeDtypeStruct((n, w), dt) for w, dt in widths],
        compiler_params=_cparams(("arbitrary",)),
        name="project",
    )(x2d, shift, scale, pw["ga"], pw["win"], pw["gql"], pw["wqu"], pw["gkv"], pw["wuk"],
      pw["gq"], pw["gk"], pw["gm"], pw["gdq"], pw["gdk"], cq, sq, cm, sm, pw["bq"], pw["b64"], pw["bm"])
    names = ("qall", "kall", "ckv", "ckvb", "misc", "qd", "kd", "kdb", "vd", "vdb", "qi", "kid")
    return dict(zip(names, outs))


_MLA_TQ = 256
_MLA_TK = 256


def _mla_prompt_kernel(q_ref, k_ref, c_ref, wuv_ref, o_ref):
    tq, tk = _MLA_TQ, _MLA_TK
    i = pl.program_id(1)
    q0 = i * tq
    nkb = (q0 + tq + tk - 1) // tk
    row = q0 + lax.broadcasted_iota(I32, (tq, tk), 0)
    col0 = lax.broadcasted_iota(I32, (tq, tk), 1)
    heads = []
    for h in range(MLA_HEADS):
        qh = q_ref[:, h * LANES:(h + 1) * LANES]

        def body(j, carry, qh=qh, h=h):
            m, l, acc = carry
            k0 = pl.multiple_of(j * tk, tk)
            kh = k_ref[pl.ds(k0, tk), h * LANES:(h + 1) * LANES]
            s = _nt_dot(qh, kh) * MLA_SCALE
            s = jnp.where(col0 + k0 <= row, s, NEG_INF)
            m_new = jnp.maximum(m, jnp.max(s, axis=1, keepdims=True))
            alpha = jnp.exp(m - m_new)
            p = jnp.exp(s - m_new)
            l = alpha * l + jnp.sum(p, axis=1, keepdims=True)
            acc = alpha * acc + jnp.dot(p.astype(BF16), c_ref[pl.ds(k0, tk), :], preferred_element_type=F32)
            return m_new, l, acc

        init = (jnp.full((tq, 1), NEG_INF, F32), jnp.zeros((tq, 1), F32), jnp.zeros((tq, KV_LORA), F32))
        m, l, acc = lax.fori_loop(0, nkb, body, init)
        heads.append((acc / l).astype(BF16))
    for pr in range(MLA_HEADS // 2):
        o = (jnp.dot(heads[2 * pr], wuv_ref[2 * pr], preferred_element_type=F32)
             + jnp.dot(heads[2 * pr + 1], wuv_ref[2 * pr + 1], preferred_element_type=F32))
        o_ref[:, pr * LANES:(pr + 1) * LANES] = o.astype(BF16)


def _pad_wuv(w_uv):
    w = jnp.transpose(w_uv, (1, 0, 2))
    even = jnp.pad(w, ((0, 0), (0, 0), (0, LANES - MLA_V)))
    odd = jnp.pad(w, ((0, 0), (0, 0), (LANES - MLA_V, 0)))
    is_odd = (jnp.arange(MLA_HEADS) % 2 == 1)[:, None, None]
    return jnp.where(is_odd, odd, even).astype(BF16)


def _mla_prompt(qall, kall, ckvb, wuv_pad):
    b, t, _ = qall.shape
    tq = _MLA_TQ
    return pl.pallas_call(
        _mla_prompt_kernel,
        grid=(b, t // tq),
        in_specs=[pl.BlockSpec((None, tq, MLA_HEADS * LANES), lambda bi, i: (bi, i, 0)),
                  pl.BlockSpec((None, t, MLA_HEADS * LANES), lambda bi, i: (bi, 0, 0)),
                  pl.BlockSpec((None, t, KV_LORA), lambda bi, i: (bi, 0, 0)),
                  pl.BlockSpec((MLA_HEADS, KV_LORA, LANES), lambda bi, i: (0, 0, 0))],
        out_specs=pl.BlockSpec((None, tq, MLA_OUT), lambda bi, i: (bi, i, 0)),
        out_shape=jax.ShapeDtypeStruct((b, t, MLA_OUT), BF16),
        compiler_params=_cparams(("arbitrary", "arbitrary")),
        name="mla_prompt",
    )(qall, kall, ckvb, wuv_pad)


_DSA_T = 128


def _sortable_key(score):
    bits = pltpu.bitcast(score, I32)
    key = jnp.where(bits < 0, bits ^ jnp.int32(0x7FFFFFFF), bits)
    return jnp.where(score == 0.0, 0, key)


def _bias_tiles(rb_ref, tz_ref):
    t = _DSA_T
    r = lax.broadcasted_iota(I32, (t, t), 0)
    c = lax.broadcasted_iota(I32, (t, t), 1)
    exact = REL_BUCKETS // 2
    for which in range(2):
        d = jnp.maximum(r - c + t * which, 0)
        logd = jnp.log(jnp.maximum(d, 1).astype(F32) / exact) / math.log(REL_MAX_DIST / exact)
        far = jnp.minimum(exact + (logd * (REL_BUCKETS - exact)).astype(I32), REL_BUCKETS - 1)
        bucket = jnp.where(d < exact, d, far)
        for h in range(DSA_HEADS):
            tile = jnp.zeros((t, t), F32)
            for bk in range(REL_BUCKETS):
                tile = jnp.where(bucket == bk, rb_ref[bk, h], tile)
            tz_ref[h, which] = tile


def _select_mask(keys_ref, mask_ref, tri_ref, nb, n_sel, rows):
    t = _DSA_T

    def count(pred):
        def body(j, acc):
            kb = keys_ref[:, pl.ds(pl.multiple_of(j * t, t), t)]
            return acc + jnp.where(pred(kb), 1.0, 0.0)
        acc = lax.fori_loop(0, nb, body, jnp.zeros((rows, t), F32))
        return jnp.sum(acc, axis=1, keepdims=True)

    kf = jnp.float32(n_sel)
    zero = jnp.zeros((rows, 1), I32)
    thr = jnp.where(count(lambda kb: kb >= zero) >= kf, zero, jnp.full((rows, 1), INT_MIN, I32))

    def bit_body(bi, thr):
        cand = thr + lax.shift_left(jnp.int32(1), 30 - bi)
        return jnp.where(count(lambda kb: kb >= cand) >= kf, cand, thr)

    thr = lax.fori_loop(0, 31, bit_body, thr)
    need = kf - count(lambda kb: kb > thr)

    def mask_body(j, carry):
        k0 = pl.multiple_of(j * t, t)
        kb = keys_ref[:, pl.ds(k0, t)]
        eq = kb == thr
        eqf = jnp.where(eq, 1.0, 0.0)
        before = jnp.dot(eqf.astype(BF16), tri_ref[...], preferred_element_type=F32) + carry
        keep = (kb > thr) | (eq & (before < need))
        keep = keep & (kb > INT_MIN)
        mask_ref[:, pl.ds(k0, t)] = jnp.where(keep, 0.0, NEG_INF)
        return carry + jnp.sum(eqf, axis=1, keepdims=True)

    lax.fori_loop(0, nb, mask_body, jnp.zeros((rows, 1), F32))


def _dsa_prompt_kernel(rb_ref, qi_ref, misc_ref, kid_ref, qd_ref, kd_ref, vd_ref, tri_ref, o_ref,
                       keys_ref, mask_ref, tz_ref, *, n_sel):
    t = _DSA_T
    bi = pl.program_id(0)
    i = pl.program_id(1)
    nb = i + 1

    @pl.when((bi == 0) & (i == 0))
    def _():
        _bias_tiles(rb_ref, tz_ref)

    row = i * t + lax.broadcasted_iota(I32, (t, t), 0)
    col0 = lax.broadcasted_iota(I32, (t, t), 1)
    lane = lax.broadcasted_iota(I32, (t, LANES), 1)
    low = lane < DSA_HEAD_DIM
    wi = misc_ref[...]

    def idx_body(j, _):
        k0 = pl.multiple_of(j * t, t)
        kk = kid_ref[pl.ds(k0, t), :]
        acc = jnp.zeros((t, t), F32)
        for h in range(IDX_HEADS):
            qs = qi_ref[:, (h // 2) * LANES:(h // 2 + 1) * LANES]
            qm = jnp.where(low if h % 2 == 0 else ~low, qs, jnp.zeros_like(qs))
            r = jnp.maximum(_nt_dot(qm, kk) * (IDX_DIM ** -0.5), 0.0)
            acc = acc + wi[:, _M_WI + h:_M_WI + h + 1] * r
        keys_ref[:, pl.ds(k0, t)] = jnp.where(col0 + k0 <= row, _sortable_key(acc), INT_MIN)
        return 0

    lax.fori_loop(0, nb, idx_body, 0)
    _select_mask(keys_ref, mask_ref, tri_ref, nb, n_sel, t)

    for pr in range(DSA_HEADS // 2):
        qs = qd_ref[:, pr * LANES:(pr + 1) * LANES]
        halves = []
        for half in range(2):
            h = 2 * pr + half
            qm = jnp.where(low if half == 0 else ~low, qs, jnp.zeros_like(qs))
            far_bias = rb_ref[REL_BUCKETS - 1, h]

            def body(j, carry, qm=qm, h=h, pr=pr, far_bias=far_bias):
                m, l, acc = carry
                k0 = pl.multiple_of(j * t, t)
                s = _nt_dot(qm, kd_ref[pl.ds(k0, t), pr * LANES:(pr + 1) * LANES]) * DSA_SCALE
                bias = jnp.where(j == i, tz_ref[h, 0], jnp.where(j == i - 1, tz_ref[h, 1], far_bias))
                s = s + bias + mask_ref[:, pl.ds(k0, t)]
                m_new = jnp.maximum(m, jnp.max(s, axis=1, keepdims=True))
                m_safe = jnp.where(m_new == NEG_INF, 0.0, m_new)
                alpha = jnp.exp(m - m_safe)
                p = jnp.exp(s - m_safe)
                l = alpha * l + jnp.sum(p, axis=1, keepdims=True)
                acc = alpha * acc + jnp.dot(p.astype(BF16), vd_ref[pl.ds(k0, t), pr * LANES:(pr + 1) * LANES],
                                            preferred_element_type=F32)
                return m_new, l, acc

            init = (jnp.full((t, 1), NEG_INF, F32), jnp.zeros((t, 1), F32), jnp.zeros((t, LANES), F32))
            m, l, acc = lax.fori_loop(0, nb, body, init)
            halves.append(acc / l)
        o_ref[:, pr * LANES:(pr + 1) * LANES] = jnp.where(low, halves[0], halves[1]).astype(BF16)


def _strict_upper(n):
    return jnp.asarray(np.triu(np.ones((n, n), np.float32), 1), BF16)


def _dsa_prompt(rel_bias, qi, misc, kid, qd, kdb, vdb):
    b, t, _ = qi.shape
    tq = _DSA_T
    n_sel = min(IDX_TOPK_MAX, t // 4)
    blk = lambda w: pl.BlockSpec((None, tq, w), lambda bi, i: (bi, i, 0))
    full = lambda w: pl.BlockSpec((None, t, w), lambda bi, i: (bi, 0, 0))
    return pl.pallas_call(
        functools.partial(_dsa_prompt_kernel, n_sel=n_sel),
        grid=(b, t // tq),
        in_specs=[pl.BlockSpec(memory_space=pltpu.SMEM),
                  blk(IDX_HEADS * IDX_DIM), blk(LANES), full(LANES), blk(DSA_OUT), full(DSA_OUT), full(DSA_OUT),
                  pl.BlockSpec((tq, tq), lambda bi, i: (0, 0))],
        out_specs=blk(DSA_OUT),
        out_shape=jax.ShapeDtypeStruct((b, t, DSA_OUT), BF16),
        scratch_shapes=[pltpu.VMEM((tq, t), I32), pltpu.VMEM((tq, t), F32),
                        pltpu.VMEM((DSA_HEADS, 2, tq, tq), F32)],
        compiler_params=_cparams(("arbitrary", "arbitrary")),
        name="dsa_prompt",
    )(rel_bias, qi, misc, kid, qd, kdb, vdb, _strict_upper(tq))


_EXPERTS_PER_GROUP = N_EXPERTS // N_GROUPS


def _first_index_of_max(v, idx, axis, sentinel):
    mx = jnp.max(v, axis=axis, keepdims=True)
    first = jnp.min(jnp.where(v == mx, idx, sentinel), axis=axis, keepdims=True)
    return mx, first


def _route(logits_t, bias_col):
    n_tok = logits_t.shape[1]
    scores = jax.nn.sigmoid(logits_t)
    biased = scores + bias_col
    b3 = biased.reshape(N_GROUPS, _EXPERTS_PER_GROUP, n_tok)
    j3 = lax.broadcasted_iota(I32, b3.shape, 1)
    m1, f1 = _first_index_of_max(b3, j3, 1, _EXPERTS_PER_GROUP)
    m2 = jnp.max(jnp.where(j3 == f1, NEG_INF, b3), axis=1, keepdims=True)
    gs = (m1 + m2).reshape(N_GROUPS, n_tok)
    gi = lax.broadcasted_iota(I32, gs.shape, 0)
    gsel = jnp.zeros(gs.shape, jnp.bool_)
    for _ in range(TOPK_GROUPS):
        _, first = _first_index_of_max(gs, gi, 0, N_GROUPS)
        hit = gi == first
        gsel = gsel | hit
        gs = jnp.where(hit, NEG_INF, gs)
    gsel3 = jnp.broadcast_to(gsel.reshape(N_GROUPS, 1, n_tok), b3.shape)
    masked = jnp.where(gsel3, b3, NEG_INF).reshape(N_EXPERTS, n_tok)
    ei = lax.broadcasted_iota(I32, masked.shape, 0)
    sel = jnp.zeros(masked.shape, jnp.bool_)
    for _ in range(TOP_K):
        _, first = _first_index_of_max(masked, ei, 0, N_EXPERTS)
        hit = ei == first
        sel = sel | hit
        masked = jnp.where(hit, NEG_INF, masked)
    w = jnp.where(sel, scores, 0.0)
    gate = w / jnp.sum(w, axis=0, keepdims=True) * ROUTED_SCALE
    return sel, gate


def _swiglu_bf(x_bf, wgu_ref, wdown_ref, d_hidden):
    gu = jnp.dot(x_bf, wgu_ref[...], preferred_element_type=F32)
    g, u = gu[:, :d_hidden], gu[:, d_hidden:]
    act = (g * jax.nn.sigmoid(g)) * u
    return jnp.dot(act.astype(BF16), wdown_ref[...], preferred_element_type=F32)


def _moe_kernel(x_ref, oa_ref, ob_ref, ga_ref, sf_ref, cf_ref, gf_ref, gn_ref, wo_ref, wr_ref, br_ref,
                wsg_ref, wsd_ref, tri_ref, weg_ref, wed_ref, y_ref, xt_ref, acc_ref, gate_ref, rank_ref,
                *, cap, n_valid):
    tm = x_ref.shape[0]
    e = pl.program_id(1)

    @pl.when(e == 0)
    def _():
        half = oa_ref.shape[1]
        attn = (jnp.dot(oa_ref[...], wo_ref[:half, :], preferred_element_type=F32)
                + jnp.dot(ob_ref[...], wo_ref[half:, :], preferred_element_type=F32))
        x1 = x_ref[...] + ga_ref[...] * attn
        y_ref[...] = x1
        hn = x1 * lax.rsqrt(jnp.mean(x1 * x1, axis=-1, keepdims=True) + EPS) * gn_ref[...]
        xt = (hn * (1.0 + cf_ref[...]) + sf_ref[...]).astype(BF16)
        xt_ref[...] = xt
        sel, gate = _route(_nt_dot(wr_ref[...], xt), br_ref[...])
        if n_valid < tm:
            sel = sel & (lax.broadcasted_iota(I32, sel.shape, 1) < n_valid)
        ind = jnp.where(sel, 1.0, 0.0)
        before = jnp.dot(ind.astype(BF16), tri_ref[...], preferred_element_type=F32)
        rank_ref[...] = jnp.where(sel, before, -1.0)
        gate_ref[...] = jnp.where(sel, gate, 0.0)
        acc_ref[...] = jnp.zeros_like(acc_ref)

    rank_row = rank_ref[pl.ds(e, 1), :]
    gate_row = gate_ref[pl.ds(e, 1), :]
    count = (jnp.max(rank_row) + 1.0).astype(I32)
    n_chunks = (count + cap - 1) // cap
    slot = lax.broadcasted_iota(I32, (cap, tm), 0).astype(F32)

    def chunk(c, _):
        pick = slot + (c * cap).astype(F32) == rank_row
        pick_f = jnp.where(pick, 1.0, 0.0)
        pick_b = pick_f.astype(BF16)
        xe = jnp.dot(pick_b, xt_ref[...], preferred_element_type=F32).astype(BF16)
        ye = _swiglu_bf(xe, weg_ref, wed_ref, D_EXPERT)
        ge = jnp.sum(pick_f * gate_row, axis=1, keepdims=True)
        ys = (ye * ge).astype(BF16)
        acc_ref[...] += lax.dot_general(pick_b, ys, (((0,), (0,)), ((), ())), preferred_element_type=F32)
        return 0

    lax.fori_loop(0, n_chunks, chunk, 0)

    @pl.when(e == N_EXPERTS - 1)
    def _():
        shared = _swiglu_bf(xt_ref[...], wsg_ref, wsd_ref, D_SHARED)
        y_ref[...] = y_ref[...] + gf_ref[...] * (acc_ref[...] + shared)


def _moe(x2d, oa, ob, mods, g_ffn, wo_bf, wr_t_bf, b_router, wsg_bf, wsd_bf, weg_bf, wed_bf, tm, rows_per_mod):
    n = x2d.shape[0]
    nt = -(-n // tm)
    cap = min(LANES, tm)
    const = lambda shape: pl.BlockSpec(shape, lambda t, e: (0,) * len(shape))
    row = lambda w: pl.BlockSpec((tm, w), lambda t, e: (t, 0))
    if rows_per_mod:
        per = rows_per_mod // tm
        mod_spec = pl.BlockSpec((None, 1, D_MODEL), lambda t, e: (t // per, 0, 0))
    else:
        mod_spec = row(D_MODEL)
    half = oa.shape[1]
    return pl.pallas_call(
        functools.partial(_moe_kernel, cap=cap, n_valid=min(n, tm)),
        grid=(nt, N_EXPERTS),
        in_specs=[row(D_MODEL), row(half), row(half), mod_spec, mod_spec, mod_spec, mod_spec,
                  const((1, D_MODEL)), const((2 * half, D_MODEL)), const((N_EXPERTS, D_MODEL)),
                  const((N_EXPERTS, 1)), const((D_MODEL, 2 * D_SHARED)), const((D_SHARED, D_MODEL)),
                  const((tm, tm)),
                  pl.BlockSpec((None, D_MODEL, 2 * D_EXPERT), lambda t, e: (e, 0, 0)),
                  pl.BlockSpec((None, D_EXPERT, D_MODEL), lambda t, e: (e, 0, 0))],
        out_specs=row(D_MODEL),
        out_shape=jax.ShapeDtypeStruct((n, D_MODEL), F32),
        scratch_shapes=[pltpu.VMEM((tm, D_MODEL), BF16), pltpu.VMEM((tm, D_MODEL), F32),
                        pltpu.VMEM((N_EXPERTS, tm), F32), pltpu.VMEM((N_EXPERTS, tm), F32)],
        compiler_params=_cparams(("arbitrary", "arbitrary")),
        name="moe",
    )(x2d, oa, ob, *mods, g_ffn.reshape(1, -1), wo_bf, wr_t_bf, b_router.reshape(-1, 1), wsg_bf, wsd_bf,
      _strict_upper(tm), weg_bf, wed_bf)


def _tn_dot(a, b):
    return lax.dot_general(a, b, (((0,), (0,)), ((), ())), preferred_element_type=F32)


def _col_blocks(q):
    nh = q.shape[0]
    r = lax.broadcasted_iota(I32, (nh, LANES), 0)
    c = lax.broadcasted_iota(I32, (nh, LANES), 1)
    blocks = [_tn_dot(q, jnp.where((r == h) & (c == h), 1.0, 0.0).astype(q.dtype)) for h in range(nh)]
    return jnp.concatenate(blocks, axis=0)


def _rows_to_cols(row):
    r8 = jnp.broadcast_to(row, (8, row.shape[1]))
    e0 = jnp.where(lax.broadcasted_iota(I32, (8, LANES), 0) == 0, 1.0, 0.0).astype(row.dtype)
    return _tn_dot(r8, e0)


def _head_diag(acc, width):
    r = lax.broadcasted_iota(I32, acc.shape, 0)
    c = lax.broadcasted_iota(I32, acc.shape, 1)
    return jnp.sum(jnp.where(r == c // width, acc, 0.0), axis=0, keepdims=True)


class _PageStream:
    def __init__(self, hbm_ref, buf_ref, sem_ref, pt_ref, seq, pages_per_chunk):
        self.hbm, self.buf, self.sem, self.pt, self.seq, self.ppc = hbm_ref, buf_ref, sem_ref, pt_ref, seq, pages_per_chunk

    def _copy(self, page, slot, r):
        return pltpu.make_async_copy(self.hbm.at[page], self.buf.at[slot, pl.ds(r * PAGE_SIZE, PAGE_SIZE)],
                                     self.sem.at[slot])

    def start(self, chunk, slot):
        for r in range(self.ppc):
            self._copy(self.pt[self.seq, chunk * self.ppc + r], slot, r).start()

    def wait(self, slot):
        for r in range(self.ppc):
            self._copy(0, slot, r).wait()


def _stream_loop(streams, n_chunks, body):
    for st in streams:
        st.start(0, 0)

    def step(c, _):
        slot = c % 2

        @pl.when(c + 1 < n_chunks)
        def _():
            for st in streams:
                st.start(c + 1, 1 - slot)

        for st in streams:
            st.wait(slot)
        body(c, slot)
        return 0

    lax.fori_loop(0, n_chunks, step, 0)


def _softmax_stats(sc_ref, n_rows, blk):
    nb = n_rows // blk
    tail = n_rows - nb * blk

    def mx_body(j, m):
        return jnp.maximum(m, jnp.max(sc_ref[pl.ds(pl.multiple_of(j * blk, blk), blk), :], axis=0, keepdims=True))

    m = lax.fori_loop(0, nb, mx_body, jnp.full((1, LANES), NEG_INF, F32))
    if tail:
        m = jnp.maximum(m, jnp.max(sc_ref[pl.ds(nb * blk, tail), :], axis=0, keepdims=True))

    def sum_body(j, l):
        return l + jnp.sum(jnp.exp(sc_ref[pl.ds(pl.multiple_of(j * blk, blk), blk), :] - m), axis=0, keepdims=True)

    l = lax.fori_loop(0, nb, sum_body, jnp.zeros((1, LANES), F32))
    if tail:
        l = l + jnp.sum(jnp.exp(sc_ref[pl.ds(nb * blk, tail), :] - m), axis=0, keepdims=True)
    return m, l


_MLA_S_PAGES = 4


def _mla_sample_kernel(pt_ref, q_ref, knew_ref, cnew_ref, wuk_ref, gk_ref, wuv_ref, lat_hbm, kr_hbm, o_ref,
                       latbuf, krbuf, sems, sc_ref, acc_ref, *, n_pages):
    s_id = pl.program_id(0)
    ppc = _MLA_S_PAGES
    ck = ppc * PAGE_SIZE
    n_chunks = n_pages // ppc
    past = n_pages * PAGE_SIZE
    row8 = lax.broadcasted_iota(I32, (8, LANES), 0)

    qblk = _col_blocks(q_ref[...]).astype(BF16)
    qr = qblk[MLA_NOPE:MLA_NOPE + MLA_ROPE, :].astype(F32)
    for h in range(1, MLA_HEADS):
        qr = qr + qblk[h * LANES + MLA_NOPE:h * LANES + MLA_NOPE + MLA_ROPE, :].astype(F32)
    qr = qr.astype(BF16)
    lat_stream = _PageStream(lat_hbm, latbuf, sems.at[0], pt_ref, s_id, ppc)
    kr_stream = _PageStream(kr_hbm, krbuf, sems.at[1], pt_ref, s_id, ppc)

    def score_chunk(c, slot):
        lat = latbuf[slot].astype(BF16)
        kraw = jnp.dot(lat, wuk_ref[...], preferred_element_type=F32)
        slabs = []
        for h in range(MLA_HEADS):
            x = kraw[:, h * LANES:(h + 1) * LANES]
            ms = jnp.sum(x * x, axis=1, keepdims=True) * (1.0 / MLA_NOPE)
            slabs.append((x * lax.rsqrt(ms + EPS) * gk_ref[:, h * LANES:(h + 1) * LANES]).astype(BF16))
        kn = jnp.concatenate(slabs, axis=1)
        s = (jnp.dot(kn, qblk, preferred_element_type=F32)
             + jnp.dot(krbuf[slot].astype(BF16), qr, preferred_element_type=F32)) * MLA_SCALE
        sc_ref[pl.ds(pl.multiple_of(c * ck, ck), ck), :] = s

    _stream_loop((lat_stream, kr_stream), n_chunks, score_chunk)
    s_new = jnp.dot(jnp.broadcast_to(knew_ref[...], (8, knew_ref.shape[1])), qblk,
                    preferred_element_type=F32) * MLA_SCALE
    sc_ref[pl.ds(past, 8), :] = jnp.where(row8 == 0, s_new, NEG_INF)
    m, l = _softmax_stats(sc_ref, past + 8, ck)

    acc_ref[...] = jnp.zeros_like(acc_ref)

    def pv_chunk(c, slot):
        p = jnp.exp(sc_ref[pl.ds(pl.multiple_of(c * ck, ck), ck), :] - m) / l
        acc_ref[...] += _tn_dot(p.astype(BF16), latbuf[slot].astype(BF16))

    _stream_loop((lat_stream,), n_chunks, pv_chunk)
    p_new = jnp.exp(sc_ref[pl.ds(past, 8), :] - m) / l
    acc = acc_ref[...] + _tn_dot(p_new.astype(BF16), jnp.broadcast_to(cnew_ref[...], (8, KV_LORA)))
    out = jnp.dot(acc.astype(BF16), wuv_ref[...], preferred_element_type=F32)
    o_ref[...] = _head_diag(out, MLA_V).astype(BF16)


def _mla_sample(page_table, q8, knew, cnew, wuk_pad, gk, wuv_flat, cache_lat, cache_kr):
    db, n_pages = page_table.shape
    ck = _MLA_S_PAGES * PAGE_SIZE
    past = n_pages * PAGE_SIZE
    per_seq = lambda shape: pl.BlockSpec((None,) + shape, lambda s, pt: (s,) + (0,) * len(shape))
    const = lambda shape: pl.BlockSpec(shape, lambda s, pt: (0,) * len(shape))
    grid_spec = pltpu.PrefetchScalarGridSpec(
        num_scalar_prefetch=1, grid=(db,),
        in_specs=[per_seq((MLA_HEADS, LANES)), per_seq((1, MLA_HEADS * LANES)), per_seq((1, KV_LORA)),
                  const((KV_LORA, MLA_HEADS * LANES)), const((1, MLA_HEADS * LANES)), const((KV_LORA, MLA_OUT)),
                  pl.BlockSpec(memory_space=pl.ANY), pl.BlockSpec(memory_space=pl.ANY)],
        out_specs=per_seq((1, MLA_OUT)),
        scratch_shapes=[pltpu.VMEM((2, ck, KV_LORA), F32), pltpu.VMEM((2, ck, MLA_ROPE), F32),
                        pltpu.SemaphoreType.DMA((2, 2)), pltpu.VMEM((past + 8, LANES), F32),
                        pltpu.VMEM((LANES, KV_LORA), F32)])
    return pl.pallas_call(
        functools.partial(_mla_sample_kernel, n_pages=n_pages),
        grid_spec=grid_spec,
        out_shape=jax.ShapeDtypeStruct((db, 1, MLA_OUT), BF16),
        compiler_params=_cparams(("arbitrary",)),
        name="mla_sample",
    )(page_table, q8, knew, cnew, wuk_pad, gk, wuv_flat, cache_lat, cache_kr)


_DSA_S_IDX_PAGES = 8
_DSA_S_KV_PAGES = 4


def _select_flat(keys, tri_u, tri_l, n_sel):
    def count(pred):
        c = jnp.sum(jnp.where(pred, 1.0, 0.0), axis=1, keepdims=True)
        return jnp.sum(c, axis=0, keepdims=True)

    kf = jnp.float32(n_sel)
    zero = jnp.zeros((1, 1), I32)
    thr = jnp.where(count(keys >= zero) >= kf, zero, jnp.full((1, 1), INT_MIN, I32))

    def bit_body(bi, thr):
        cand = thr + lax.shift_left(jnp.int32(1), 30 - bi)
        return jnp.where(count(keys >= cand) >= kf, cand, thr)

    thr = lax.fori_loop(0, 31, bit_body, thr)
    need = kf - count(keys > thr)
    eq = keys == thr
    eqf = jnp.where(eq, 1.0, 0.0)
    within = jnp.dot(eqf.astype(BF16), tri_u, preferred_element_type=F32)
    rowcount = jnp.broadcast_to(jnp.sum(eqf, axis=1, keepdims=True), eqf.shape)
    carry = jnp.dot(tri_l, rowcount.astype(BF16), preferred_element_type=F32)
    keep = ((keys > thr) | (eq & (within + carry < need))) & (keys > INT_MIN)
    return jnp.where(keep, 1.0, 0.0)


def _dsa_sample_kernel(pt_ref, rb_ref, rbv_ref, qi_ref, wi_ref, kinew_ref, qd_ref, kdnew_ref, vdnew_ref,
                       triu_ref, tril_ref, idx_hbm, k_hbm, v_hbm, o_ref,
                       idxbuf, kvbuf, sems, keys_ref, mask_ref, sc_ref, acc_ref, *, n_pages, n_sel):
    s_id = pl.program_id(0)
    past = n_pages * PAGE_SIZE
    row8 = lax.broadcasted_iota(I32, (8, LANES), 0)
    lane1 = lax.broadcasted_iota(I32, (1, LANES), 1)
    qi = qi_ref[...]
    wi = wi_ref[...]

    keys_ref[...] = jnp.full(keys_ref.shape, INT_MIN, I32)
    idx_stream = _PageStream(idx_hbm, idxbuf, sems.at[0], pt_ref, s_id, _DSA_S_IDX_PAGES)

    def idx_chunk(c, slot):
        for r in range(_DSA_S_IDX_PAGES):
            kk = idxbuf[slot, r * PAGE_SIZE:(r + 1) * PAGE_SIZE, :].astype(BF16)
            rr = jnp.maximum(_nt_dot(qi, kk) * (IDX_DIM ** -0.5), 0.0)
            sc = jnp.sum(wi * rr, axis=0, keepdims=True)
            keys_ref[pl.ds(c * _DSA_S_IDX_PAGES + r, 1), :] = _sortable_key(sc)

    _stream_loop((idx_stream,), n_pages // _DSA_S_IDX_PAGES, idx_chunk)
    r_new = jnp.maximum(jnp.sum(qi.astype(F32) * kinew_ref[...].astype(F32), axis=1, keepdims=True)
                        * (IDX_DIM ** -0.5), 0.0)
    sc_new = jnp.sum(wi * r_new, axis=0, keepdims=True)
    keys_ref[pl.ds(n_pages, 1), :] = jnp.where(lane1 == 0, _sortable_key(jnp.broadcast_to(sc_new, (1, LANES))),
                                               INT_MIN)
    mask_ref[...] = _select_flat(keys_ref[...], triu_ref[...], tril_ref[...], n_sel)

    qblk = _col_blocks(qd_ref[...]).astype(BF16)
    far_row = rbv_ref[REL_BUCKETS - 1:REL_BUCKETS, :]
    d_col = PAGE_SIZE - lax.broadcasted_iota(I32, (PAGE_SIZE, 1), 0)
    exact = REL_BUCKETS // 2
    logd = jnp.log(jnp.maximum(d_col, 1).astype(F32) / exact) / math.log(REL_MAX_DIST / exact)
    far = jnp.minimum(exact + (logd * (REL_BUCKETS - exact)).astype(I32), REL_BUCKETS - 1)
    bucket = jnp.where(d_col < exact, d_col, far)
    near_tile = jnp.zeros((PAGE_SIZE, LANES), F32)
    for bk in range(REL_BUCKETS):
        near_tile = jnp.where(bucket == bk, rbv_ref[bk:bk + 1, :], near_tile)

    k_stream = _PageStream(k_hbm, kvbuf, sems.at[1], pt_ref, s_id, _DSA_S_KV_PAGES)
    v_stream = _PageStream(v_hbm, kvbuf, sems.at[1], pt_ref, s_id, _DSA_S_KV_PAGES)

    def k_chunk(c, slot):
        for r in range(_DSA_S_KV_PAGES):
            page = c * _DSA_S_KV_PAGES + r
            kb = kvbuf[slot, r * PAGE_SIZE:(r + 1) * PAGE_SIZE, :].astype(BF16)
            s = jnp.dot(kb, qblk, preferred_element_type=F32) * DSA_SCALE
            bias = jnp.where(page == n_pages - 1, near_tile, far_row)
            keep = _rows_to_cols(mask_ref[pl.ds(page, 1), :].astype(BF16))
            sc_ref[pl.ds(pl.multiple_of(page * PAGE_SIZE, PAGE_SIZE), PAGE_SIZE), :] = jnp.where(
                keep > 0.5, s + bias, NEG_INF)

    _stream_loop((k_stream,), n_pages // _DSA_S_KV_PAGES, k_chunk)
    s_new = (jnp.dot(jnp.broadcast_to(kdnew_ref[...], (8, DSA_OUT)), qblk, preferred_element_type=F32) * DSA_SCALE
             + rbv_ref[0:1, :])
    keep_new = mask_ref[pl.ds(n_pages, 1), :][:, 0:1] > 0.5
    sc_ref[pl.ds(past, 8), :] = jnp.where((row8 == 0) & keep_new, s_new, NEG_INF)
    m, l = _softmax_stats(sc_ref, past + 8, PAGE_SIZE * _DSA_S_KV_PAGES)

    acc_ref[...] = jnp.zeros_like(acc_ref)

    def v_chunk(c, slot):
        ck = PAGE_SIZE * _DSA_S_KV_PAGES
        p = jnp.exp(sc_ref[pl.ds(pl.multiple_of(c * ck, ck), ck), :] - m) / l
        acc_ref[...] += _tn_dot(p.astype(BF16), kvbuf[slot].astype(BF16))

    _stream_loop((v_stream,), n_pages // _DSA_S_KV_PAGES, v_chunk)
    p_new = jnp.exp(sc_ref[pl.ds(past, 8), :] - m) / l
    acc = acc_ref[...] + _tn_dot(p_new.astype(BF16), jnp.broadcast_to(vdnew_ref[...], (8, DSA_OUT)))
    o_ref[...] = _head_diag(acc, DSA_HEAD_DIM).astype(BF16)


def _dsa_sample(page_table, rel_bias, qi8, wi8, kinew, qd8, kdnew, vdnew, cache_idx, cache_k, cache_v):
    db, n_pages = page_table.shape
    past = n_pages * PAGE_SIZE
    n_sel = min(IDX_TOPK_MAX, (past + 1) // 4)
    rows = -(-(n_pages + 1) // LANES) * LANES
    rbv = jnp.pad(rel_bias, ((0, 0), (0, LANES - DSA_HEADS)))
    tri_l = jnp.asarray(np.tril(np.ones((rows, rows), np.float32), -1), BF16)
    per_seq = lambda shape: pl.BlockSpec((None,) + shape, lambda s, pt: (s,) + (0,) * len(shape))
    const = lambda shape: pl.BlockSpec(shape, lambda s, pt: (0,) * len(shape))
    any_spec = pl.BlockSpec(memory_space=pl.ANY)
    ck = PAGE_SIZE * _DSA_S_KV_PAGES
    grid_spec = pltpu.PrefetchScalarGridSpec(
        num_scalar_prefetch=1, grid=(db,),
        in_specs=[pl.BlockSpec(memory_space=pltpu.SMEM), const((REL_BUCKETS, LANES)),
                  per_seq((IDX_HEADS, IDX_DIM)), per_seq((IDX_HEADS, 1)), per_seq((1, IDX_DIM)),
                  per_seq((DSA_HEADS, DSA_HEAD_DIM)), per_seq((1, DSA_OUT)), per_seq((1, DSA_OUT)),
                  const((LANES, LANES)), const((rows, rows)), any_spec, any_spec, any_spec],
        out_specs=per_seq((1, DSA_OUT)),
        scratch_shapes=[pltpu.VMEM((2, PAGE_SIZE * _DSA_S_IDX_PAGES, IDX_DIM), F32),
                        pltpu.VMEM((2, ck, DSA_OUT), F32), pltpu.SemaphoreType.DMA((2, 2)),
                        pltpu.VMEM((rows, LANES), I32), pltpu.VMEM((rows, LANES), F32),
                        pltpu.VMEM((past + 8, LANES), F32), pltpu.VMEM((LANES, DSA_OUT), F32)])
    return pl.pallas_call(
        functools.partial(_dsa_sample_kernel, n_pages=n_pages, n_sel=n_sel),
        grid_spec=grid_spec,
        out_shape=jax.ShapeDtypeStruct((db, 1, DSA_OUT), BF16),
        compiler_params=_cparams(("arbitrary",)),
        name="dsa_sample",
    )(page_table, rel_bias, rbv, qi8, wi8, kinew, qd8, kdnew, vdnew, _strict_upper(LANES), tri_l,
      cache_idx, cache_k, cache_v)


def _x_rms_norm(x, g):
    xf = x.astype(F32)
    y = xf * lax.rsqrt(jnp.mean(xf * xf, axis=-1, keepdims=True) + EPS)
    return (y * g.astype(F32)).astype(x.dtype)


def _x_t5_bucket(dist):
    exact = REL_BUCKETS // 2
    d = jnp.maximum(dist, 0)
    logd = jnp.log(jnp.maximum(d, 1).astype(F32) / exact) / math.log(REL_MAX_DIST / exact)
    far = jnp.minimum(exact + (logd * (REL_BUCKETS - exact)).astype(I32), REL_BUCKETS - 1)
    return jnp.where(d < exact, d, far)


def _x_mla_keys(c_kv, w_uk, g_kn):
    return _x_rms_norm(jnp.einsum('...c,chd->...hd', c_kv, w_uk), g_kn)


def _x_mla_attend(q_nope, q_rope, k_nope, k_rope, c_kv, key_pos, q_pos, w_uv):
    s = (jnp.einsum('qhd,shd->hqs', q_nope, k_nope).astype(F32)
         + jnp.einsum('qhr,sr->hqs', q_rope, k_rope).astype(F32)) * MLA_SCALE
    s = jnp.where((key_pos[None, :] <= q_pos[:, None])[None], s, -jnp.inf)
    p = jax.nn.softmax(s, axis=-1).astype(c_kv.dtype)
    o_lat = jnp.einsum('hqs,sc->qhc', p, c_kv)
    return jnp.einsum('qhc,chv->qhv', o_lat, w_uv).reshape(q_nope.shape[0], MLA_OUT)


def _x_indexer_select(q_idx, w_idx, k_idx, key_pos, q_pos, n_sel):
    r = jax.nn.relu(jnp.einsum('qhd,sd->qhs', q_idx, k_idx).astype(F32) * IDX_DIM ** -0.5)
    score = jnp.einsum('qh,qhs->qs', w_idx.astype(F32), r)
    causal = key_pos[None, :] <= q_pos[:, None]
    _, sel = lax.top_k(jnp.where(causal, score, -jnp.inf), n_sel)
    return sel, key_pos[sel] <= q_pos[:, None]


def _x_dsa_attend(q, k_sel, v_sel, sel_pos, valid, q_pos, rel_bias):
    s = jnp.einsum('qhd,qkhd->hqk', q, k_sel).astype(F32) * DSA_SCALE
    bias = rel_bias[_x_t5_bucket(q_pos[:, None] - sel_pos)].astype(F32)
    s = jnp.where(valid[None], s + jnp.transpose(bias, (2, 0, 1)), -jnp.inf)
    p = jax.nn.softmax(s, axis=-1).astype(v_sel.dtype)
    return jnp.einsum('hqk,qkhd->qhd', p, v_sel).reshape(q.shape[0], DSA_OUT)


def _x_gather_rows(cache, pages, new, idx, past):
    p = jnp.minimum(idx, past - 1)
    old = cache[pages[p // PAGE_SIZE], p % PAGE_SIZE]
    fresh = new[jnp.clip(idx - past, 0, new.shape[0] - 1)]
    in_past = (idx < past).reshape(idx.shape + (1,) * (old.ndim - idx.ndim))
    return jnp.where(in_past, old, fresh)


def _x_mixers_sample(m, cache_lat, cache_kr, cache_k, cache_v, cache_idx, page_table, w_uk, w_uv, g_mla_kn, rel_bias):
    q_nope, q_rope, c_kv, k_rope, q_d, k_d, v_d, q_i, k_i, w_i = m
    T = c_kv.shape[1]
    past = page_table.shape[1] * PAGE_SIZE
    L = past + T
    key_pos = jnp.arange(L)
    q_pos = past + jnp.arange(T)
    n_sel = min(IDX_TOPK_MAX, L // 4)

    def past_rows(cache, pages):
        return cache[pages].reshape((past,) + cache.shape[2:])

    def mla_seq(args):
        qn, qr, cn, krn, pages = args
        lat = jnp.concatenate([past_rows(cache_lat, pages), cn], axis=0)
        kr = jnp.concatenate([past_rows(cache_kr, pages), krn], axis=0)
        return _x_mla_attend(qn, qr, _x_mla_keys(lat, w_uk, g_mla_kn), kr, lat, key_pos, q_pos, w_uv)

    o_mla = lax.map(mla_seq, (q_nope, q_rope, c_kv, k_rope, page_table))

    def dsa_seq(qd, kd, vd, qi, ki, wi, pages):
        keys_idx = jnp.concatenate([past_rows(cache_idx, pages), ki], axis=0)
        sel, valid = _x_indexer_select(qi, wi, keys_idx, key_pos, q_pos, n_sel)
        k_sel = _x_gather_rows(cache_k, pages, kd, sel, past)
        v_sel = _x_gather_rows(cache_v, pages, vd, sel, past)
        return _x_dsa_attend(qd, k_sel, v_sel, sel, valid, q_pos, rel_bias)

    o_dsa = jax.vmap(dsa_seq)(q_d, k_d, v_d, q_i, k_i, w_i, page_table)
    return jnp.concatenate([o_mla, o_dsa], axis=-1)


def _x_swiglu(x, w_gu, w_down):
    g, u = jnp.split(x @ w_gu, 2, axis=-1)
    return (jax.nn.silu(g) * u) @ w_down


def _x_routed_experts(x, eidx, gate, w_e_gu, w_e_down):
    N, D = x.shape
    A = N * TOP_K
    blk = int(min(128, max(8, A // N_EXPERTS)))
    flat_e = eidx.reshape(-1)
    order = jnp.argsort(flat_e)
    e_sorted = flat_e[order]
    counts = jnp.zeros((N_EXPERTS,), I32).at[flat_e].add(1)
    start = jnp.cumsum(counts) - counts
    padded = (counts + blk - 1) // blk * blk
    pad_end = jnp.cumsum(padded)
    slot = (pad_end - padded)[e_sorted] + jnp.arange(A) - start[e_sorted]
    n_blocks = -(-(A + N_EXPERTS * (blk - 1)) // blk)
    n_slots = n_blocks * blk
    slot_token = jnp.full((n_slots,), N, I32).at[slot].set((order // TOP_K).astype(I32))
    slot_gate = jnp.zeros((n_slots,), x.dtype).at[slot].set(gate.reshape(-1)[order])
    block_expert = jnp.minimum(
        jnp.searchsorted(pad_end, jnp.arange(n_blocks) * blk, side='right'), N_EXPERTS - 1)
    x_pad = jnp.concatenate([x, jnp.zeros((1, D), x.dtype)], axis=0)
    xb = x_pad[slot_token].reshape(n_blocks, blk, D)
    yb = lax.map(lambda a: _x_swiglu(a[0], w_e_gu[a[1]], w_e_down[a[1]]), (xb, block_expert))
    y = jnp.zeros((N + 1, D), x.dtype).at[slot_token].add(yb.reshape(n_slots, D) * slot_gate[:, None])
    return y[:N]


def _x_moe(h, w_router, b_router, w_e_gu, w_e_down, w_s_gu, w_s_down):
    B, T, D = h.shape
    xt = h.reshape(B * T, D)
    N = xt.shape[0]
    scores = jax.nn.sigmoid((xt @ w_router).astype(F32))
    biased = scores + b_router.astype(F32)
    grp_score = lax.top_k(biased.reshape(N, N_GROUPS, N_EXPERTS // N_GROUPS), 2)[0].sum(-1)
    _, top_g = lax.top_k(grp_score, TOPK_GROUPS)
    gmask = jnp.any(top_g[..., None] == jnp.arange(N_GROUPS), axis=1)
    masked = jnp.where(jnp.repeat(gmask, N_EXPERTS // N_GROUPS, axis=1), biased, -jnp.inf)
    _, eidx = lax.top_k(masked, TOP_K)
    w = jnp.take_along_axis(scores, eidx, axis=1)
    w = w / jnp.sum(w, axis=-1, keepdims=True) * ROUTED_SCALE
    routed = _x_routed_experts(xt, eidx, w.astype(xt.dtype), w_e_gu, w_e_down)
    return (routed + _x_swiglu(xt, w_s_gu, w_s_down)).reshape(B, T, D)


def _unpack_heads(a, n, width, lo, hi):
    return a.reshape(n, -1, width)[:, :, lo:hi]


def kernel(x_prompt, x_sample, cache_mla_latent, cache_mla_krope, cache_dsa_k, cache_dsa_v, cache_idx_k, page_table, c_prompt, c_sample, rel_bias, w_ada, b_ada, g_attn_norm, w_in, g_q_lora, w_q_up, g_kv_lora, w_kv_up, g_mla_qn, g_mla_qr, g_mla_kn, g_mla_kr, g_dsa_q, g_dsa_k, w_out, g_ffn_norm, w_router, b_router, w_e_gu, w_e_down, w_s_gu, w_s_down):
    depth = w_ada.shape[0]
    assert depth == 1, "single-layer trunk"
    l = 0
    B, T, D = x_prompt.shape
    DB, TS, _ = x_sample.shape
    past = page_table.shape[1] * PAGE_SIZE

    w_kv = w_kv_up[l].reshape(KV_LORA, MLA_HEADS, MLA_NOPE + MLA_V)
    w_uk, w_uv = w_kv[..., :MLA_NOPE], w_kv[..., MLA_NOPE:]

    mod = _adaln(jnp.concatenate([c_prompt, c_sample], axis=0), w_ada[l].astype(BF16), b_ada[l])
    mod_p = [m.reshape(B, 1, D) for m in jnp.split(mod[:B], 6, axis=-1)]
    mod_s = jnp.split(mod[B:], 6, axis=-1)

    pw = _prep_proj_weights(w_in[l], g_q_lora[l], w_q_up[l], g_kv_lora[l], w_uk, g_mla_qn[l], g_mla_qr[l],
                            g_mla_kn[l], g_mla_kr[l], g_dsa_q[l], g_dsa_k[l], g_attn_norm[l])
    pp = _project(x_prompt.reshape(B * T, D), mod_p[0], mod_p[1], pw, _rope_tables(jnp.arange(T)), 256, T)
    ps = _project(x_sample.reshape(DB * TS, D), mod_s[0], mod_s[1], pw,
                  _rope_tables(jnp.tile(past + jnp.arange(TS), DB)), DB * TS, 0)

    r3 = lambda a: a.reshape(B, T, a.shape[-1])
    o_mla_p = _mla_prompt(r3(pp["qall"]), r3(pp["kall"]), r3(pp["ckvb"]), _pad_wuv(w_uv))
    o_dsa_p = _dsa_prompt(rel_bias, r3(pp["qi"]), r3(pp["misc"]), r3(pp["kid"]), r3(pp["qd"]), r3(pp["kdb"]),
                          r3(pp["vdb"]))

    ns = DB * TS
    assert TS == 1, "one new token per sampled sequence"
    n_pool = cache_mla_latent.shape[1]
    o_mla_s = _mla_sample(page_table, ps["qall"].reshape(ns, MLA_HEADS, LANES), ps["kall"].reshape(ns, 1, -1),
                          ps["ckvb"].reshape(ns, 1, KV_LORA), pw["wuk"], pw["gk"],
                          w_uv.reshape(KV_LORA, MLA_OUT).astype(BF16), cache_mla_latent[l], cache_mla_krope[l])
    o_dsa_s = _dsa_sample(page_table, rel_bias, ps["qi"].reshape(ns, IDX_HEADS, IDX_DIM),
                          ps["misc"][:, _M_WI:_M_WI + IDX_HEADS].reshape(ns, IDX_HEADS, 1),
                          ps["kid"][:, :IDX_DIM].reshape(ns, 1, IDX_DIM),
                          ps["qd"].reshape(ns, DSA_HEADS, DSA_HEAD_DIM), ps["kdb"].reshape(ns, 1, DSA_OUT),
                          ps["vdb"].reshape(ns, 1, DSA_OUT), cache_idx_k[l],
                          cache_dsa_k[l].reshape(n_pool, PAGE_SIZE, DSA_OUT),
                          cache_dsa_v[l].reshape(n_pool, PAGE_SIZE, DSA_OUT))

    moe_w = (g_ffn_norm[l], w_out[l].astype(BF16), w_router[l].T.astype(BF16), b_router[l],
             w_s_gu[l].astype(BF16), w_s_down[l].astype(BF16), w_e_gu[l].astype(BF16), w_e_down[l].astype(BF16))
    xp = _moe(x_prompt.reshape(B * T, D), o_mla_p.reshape(B * T, MLA_OUT), o_dsa_p.reshape(B * T, DSA_OUT),
              (mod_p[2], mod_p[3], mod_p[4], mod_p[5]), *moe_w, min(1024, T), T).reshape(B, T, D)
    xs = _moe(x_sample.reshape(ns, D), o_mla_s.reshape(ns, MLA_OUT), o_dsa_s.reshape(ns, DSA_OUT),
              (mod_s[2], mod_s[3], mod_s[4], mod_s[5]), *moe_w, ns, 0).reshape(DB, TS, D)

    def caches(p, nb, nt):
        return (p["ckv"].reshape(1, nb, nt, KV_LORA),
                p["misc"][:, :MLA_ROPE].reshape(1, nb, nt, MLA_ROPE),
                p["kd"].reshape(1, nb, nt, DSA_HEADS, DSA_HEAD_DIM),
                p["vd"].reshape(1, nb, nt, DSA_HEADS, DSA_HEAD_DIM),
                p["misc"][:, _M_KI:_M_KI + IDX_DIM].reshape(1, nb, nt, IDX_DIM))

    return (xp, xs) + caches(pp, B, T) + caches(ps, DB, TS)
```

```python
import functools
import math

import jax
import jax.numpy as jnp
import numpy as np
from jax import lax
from jax.experimental import pallas as pl
from jax.experimental.pallas import tpu as pltpu

F32 = jnp.float32
BF16 = jnp.bfloat16
I32 = jnp.int32

D_MODEL = 1024
PAGE_SIZE = 128
EPS = 1e-6
MLA_HEADS = 8
MLA_NOPE = 64
MLA_ROPE = 32
MLA_V = 64
Q_LORA = 256
KV_LORA = 128
ROPE_BASE = 10000.0
MLA_SCALE = (MLA_NOPE + MLA_ROPE) ** -0.5
DSA_HEADS = 8
DSA_HEAD_DIM = 64
DSA_SCALE = DSA_HEAD_DIM ** -0.5
IDX_HEADS = 8
IDX_DIM = 64
IDX_TOPK_MAX = 256
REL_BUCKETS = 32
REL_MAX_DIST = 128
N_EXPERTS = 64
TOP_K = 6
N_GROUPS = 8
TOPK_GROUPS = 4
D_EXPERT = 256
D_SHARED = 256
ROUTED_SCALE = 2.5
MLA_OUT = MLA_HEADS * MLA_V
DSA_OUT = DSA_HEADS * DSA_HEAD_DIM
IN_SIZES = (Q_LORA, KV_LORA, MLA_ROPE, DSA_OUT, DSA_OUT, DSA_OUT, IDX_HEADS * IDX_DIM, IDX_DIM, IDX_HEADS)

LANES = 128
INT_MIN = -(2 ** 31)
NEG_INF = float("-inf")
VMEM_LIMIT = 56 * 1024 * 1024


def _cparams(sem):
    return pltpu.CompilerParams(dimension_semantics=sem, vmem_limit_bytes=VMEM_LIMIT)


def _split_dot(x, m01, passes=3):
    acc = None
    r = x
    for p in range(passes):
        hi = r.astype(BF16)
        part = jnp.dot(hi, m01, preferred_element_type=F32)
        acc = part if acc is None else acc + part
        if p + 1 < passes:
            r = r - hi.astype(F32)
    return acc


def _group_mean(sq, bmat):
    outs = [_split_dot(sq[:, s * LANES:(s + 1) * LANES], bmat) for s in range(sq.shape[1] // LANES)]
    return outs[0] if len(outs) == 1 else jnp.concatenate(outs, axis=1)


def _rope_slabs(x, cos, sin):
    lane = lax.broadcasted_iota(I32, (x.shape[0], LANES), 1)
    first_half = (lane % MLA_ROPE) < (MLA_ROPE // 2)
    outs = []
    for s in range(x.shape[1] // LANES):
        xs = x[:, s * LANES:(s + 1) * LANES]
        rot = jnp.where(first_half, pltpu.roll(xs, LANES - MLA_ROPE // 2, 1), pltpu.roll(xs, MLA_ROPE // 2, 1))
        outs.append(xs * cos + rot * sin)
    return outs[0] if len(outs) == 1 else jnp.concatenate(outs, axis=1)


def _nt_dot(a, b):
    return lax.dot_general(a, b, (((1,), (1,)), ((), ())), preferred_element_type=F32)


def _tn_dot(a, b):
    return lax.dot_general(a, b, (((0,), (0,)), ((), ())), preferred_element_type=F32)


def _fold_lanes(x, op):
    acc = x[:, :LANES]
    for s in range(1, x.shape[1] // LANES):
        acc = op(acc, x[:, s * LANES:(s + 1) * LANES])
    return acc


def _adaln_kernel(c_ref, w_ref, b_ref, o_ref):
    c = c_ref[...]
    s = (c * jax.nn.sigmoid(c)).astype(BF16)
    o_ref[...] = jnp.dot(s, w_ref[...], preferred_element_type=F32) + b_ref[...]


def _adaln(c, w_bf, b):
    rows = c.shape[0]
    n = w_bf.shape[1]
    tn = 1536
    return pl.pallas_call(
        _adaln_kernel,
        grid=(n // tn,),
        in_specs=[pl.BlockSpec((rows, D_MODEL), lambda j: (0, 0)),
                  pl.BlockSpec((D_MODEL, tn), lambda j: (0, j)),
                  pl.BlockSpec((1, tn), lambda j: (0, j))],
        out_specs=pl.BlockSpec((rows, tn), lambda j: (0, j)),
        out_shape=jax.ShapeDtypeStruct((rows, n), F32),
        compiler_params=_cparams(("arbitrary",)),
        name="adaln",
    )(c, w_bf, b.reshape(1, n))


_C_QLAT = 0
_C_KV = _C_QLAT + Q_LORA
_C_QD = _C_KV + KV_LORA
_C_KD = _C_QD + DSA_OUT
_C_VD = _C_KD + DSA_OUT
_C_QI = _C_VD + DSA_OUT
_C_MISC = _C_QI + IDX_HEADS * IDX_DIM
_C_KIDUP = _C_MISC + LANES
_C_END = _C_KIDUP + LANES
_M_KI = MLA_ROPE
_M_WI = MLA_ROPE + IDX_DIM


def _proj_kernel(x_ref, sh_ref, sc_ref, ga_ref, win_ref, gql_ref, wqu_ref, gkv_ref, wuk_ref,
                 gq_ref, gk_ref, gm_ref, gdq_ref, gdk_ref, cq_ref, sq_ref, cm_ref, sm_ref,
                 bq_ref, b64_ref, bm_ref, ex_ref,
                 qall_ref, kall_ref, ckv_ref, ckvb_ref, misc_ref, qd_ref, kd_ref, kdb_ref, vd_ref, vdb_ref,
                 qi_ref, kid_ref, wib_ref):
    x = x_ref[...]
    xn = x * lax.rsqrt(jnp.mean(x * x, axis=-1, keepdims=True) + EPS) * ga_ref[...]
    h = xn * (1.0 + sc_ref[...]) + sh_ref[...]
    p = jnp.dot(h.astype(BF16), win_ref[...], preferred_element_type=F32)

    ql = p[:, _C_QLAT:_C_KV]
    qln = ql * lax.rsqrt(jnp.mean(ql * ql, axis=-1, keepdims=True) + EPS) * gql_ref[...]
    q = jnp.dot(qln.astype(BF16), wqu_ref[...], preferred_element_type=F32)
    qn = q * lax.rsqrt(_group_mean(q * q, bq_ref[...]) + EPS) * gq_ref[...]
    qall_ref[...] = _rope_slabs(qn, cq_ref[...], sq_ref[...]).astype(BF16)

    kv = p[:, _C_KV:_C_QD]
    ckv = kv * lax.rsqrt(jnp.mean(kv * kv, axis=-1, keepdims=True) + EPS) * gkv_ref[...]
    ckv_ref[...] = ckv
    ckvb = ckv.astype(BF16)
    ckvb_ref[...] = ckvb
    kn = jnp.dot(ckvb, wuk_ref[...], preferred_element_type=F32)
    kn = kn * lax.rsqrt(_group_mean(kn * kn, bq_ref[...]) + EPS) * gk_ref[...]

    m = p[:, _C_MISC:_C_KIDUP]
    lane = lax.broadcasted_iota(I32, m.shape, 1)
    is_kr = lane < MLA_ROPE
    mm = _split_dot(m * m, bm_ref[...])
    mn = jnp.where(is_kr, m * lax.rsqrt(mm + EPS) * gm_ref[...], m)
    mr = _rope_slabs(mn, cm_ref[...], sm_ref[...])
    is_wi = (lane >= _M_WI) & (lane < _M_WI + IDX_HEADS)
    misc = jnp.where(is_wi, mr * (IDX_HEADS ** -0.5), mr)
    misc_ref[...] = misc
    wib_ref[...] = _split_dot(misc, ex_ref[...])
    kr_placed = jnp.where((lane >= MLA_NOPE) & (lane < MLA_NOPE + MLA_ROPE), pltpu.roll(mr, MLA_NOPE, 1), 0.0)
    kall_ref[...] = jnp.concatenate(
        [kn[:, s * LANES:(s + 1) * LANES] + kr_placed for s in range(MLA_HEADS)], axis=1).astype(BF16)

    qd = p[:, _C_QD:_C_KD]
    qd_ref[...] = (qd * lax.rsqrt(_group_mean(qd * qd, b64_ref[...]) + EPS) * gdq_ref[...]).astype(BF16)
    kd = p[:, _C_KD:_C_VD]
    kdn = kd * lax.rsqrt(_group_mean(kd * kd, b64_ref[...]) + EPS) * gdk_ref[...]
    kd_ref[...] = kdn
    kdb_ref[...] = kdn.astype(BF16)
    vd = p[:, _C_VD:_C_QI]
    vd_ref[...] = vd
    vdb_ref[...] = vd.astype(BF16)
    qi_ref[...] = p[:, _C_QI:_C_MISC].astype(BF16)
    kid_ref[...] = p[:, _C_KIDUP:_C_END].astype(BF16)


def _block_mean_matrix(blocks):
    m = np.zeros((LANES, LANES), np.float32)
    for start, size in blocks:
        m[start:start + size, start:start + size] = 1.0 / size
    return jnp.asarray(m, BF16)


def _head_weight_expander():
    m = np.zeros((LANES, IDX_HEADS * LANES), np.float32)
    for h in range(IDX_HEADS):
        m[_M_WI + h, h * LANES:(h + 1) * LANES] = 1.0
    return jnp.asarray(m, BF16)


def _prep_proj_weights(w_in, g_q_lora, w_q_up, g_kv_lora, w_uk, g_mla_qn, g_mla_qr, g_mla_kn, g_mla_kr,
                       g_dsa_q, g_dsa_k, g_attn_norm):
    offs = np.cumsum((0,) + IN_SIZES)
    sec = lambda k: w_in[:, offs[k]:offs[k + 1]]
    zeros = lambda n: jnp.zeros((D_MODEL, n), w_in.dtype)
    misc = jnp.concatenate([sec(2), sec(7), sec(8), zeros(LANES - MLA_ROPE - IDX_DIM - IDX_HEADS)], axis=1)
    w_in_r = jnp.concatenate([sec(0), sec(1), sec(3), sec(4), sec(5), sec(6), misc, sec(7), sec(7)], axis=1)
    wq = w_q_up.reshape(Q_LORA, MLA_HEADS, MLA_NOPE + MLA_ROPE)
    wq = jnp.pad(wq, ((0, 0), (0, 0), (0, LANES - MLA_NOPE - MLA_ROPE))).reshape(Q_LORA, MLA_HEADS * LANES)
    wk = jnp.pad(w_uk, ((0, 0), (0, 0), (0, LANES - MLA_NOPE))).reshape(KV_LORA, MLA_HEADS * LANES)
    pad1 = lambda v, n: jnp.pad(v, (0, n - v.shape[0]))
    gq = jnp.tile(pad1(jnp.concatenate([g_mla_qn, g_mla_qr]), LANES), MLA_HEADS).reshape(1, -1)
    gk = jnp.tile(pad1(g_mla_kn, LANES), MLA_HEADS).reshape(1, -1)
    gm = jnp.concatenate([g_mla_kr, jnp.ones((LANES - MLA_ROPE,), F32)]).reshape(1, -1)
    return dict(
        win=w_in_r.astype(BF16), wqu=wq.astype(BF16), wuk=wk.astype(BF16),
        ga=g_attn_norm.reshape(1, -1), gql=g_q_lora.reshape(1, -1), gkv=g_kv_lora.reshape(1, -1),
        gq=gq, gk=gk, gm=gm,
        gdq=jnp.tile(g_dsa_q, DSA_HEADS).reshape(1, -1), gdk=jnp.tile(g_dsa_k, DSA_HEADS).reshape(1, -1),
        bq=_block_mean_matrix([(0, MLA_NOPE), (MLA_NOPE, MLA_ROPE)]),
        b64=_block_mean_matrix([(0, DSA_HEAD_DIM), (DSA_HEAD_DIM, DSA_HEAD_DIM)]),
        bm=_block_mean_matrix([(0, MLA_ROPE)]),
        ex=_head_weight_expander(),
    )


def _rope_tables(pos):
    half = MLA_ROPE // 2
    inv = ROPE_BASE ** (-jnp.arange(half, dtype=F32) / half)
    ang = pos.astype(F32)[:, None] * inv
    cos, sin = jnp.cos(ang), jnp.sin(ang)
    cos32 = jnp.concatenate([cos, cos], axis=1)
    sin32 = jnp.concatenate([-sin, sin], axis=1)
    n = pos.shape[0]
    ones = lambda w: jnp.ones((n, w), F32)
    zeros = lambda w: jnp.zeros((n, w), F32)
    cq = jnp.concatenate([ones(MLA_NOPE), cos32, ones(LANES - MLA_NOPE - MLA_ROPE)], axis=1)
    sq = jnp.concatenate([zeros(MLA_NOPE), sin32, zeros(LANES - MLA_NOPE - MLA_ROPE)], axis=1)
    cm = jnp.concatenate([cos32, ones(LANES - MLA_ROPE)], axis=1)
    sm = jnp.concatenate([sin32, zeros(LANES - MLA_ROPE)], axis=1)
    return cq, sq, cm, sm


def _project(x2d, shift, scale, pw, tables, tm, rows_per_mod):
    n = x2d.shape[0]
    nt = n // tm
    cq, sq, cm, sm = tables
    tpos = cq.shape[0] // tm
    const = lambda shape: pl.BlockSpec(shape, lambda i: (0,) * len(shape))
    row = lambda w: pl.BlockSpec((tm, w), lambda i: (i, 0))
    if rows_per_mod:
        per = rows_per_mod // tm
        mod_spec = pl.BlockSpec((None, 1, D_MODEL), lambda i: (i // per, 0, 0))
    else:
        mod_spec = row(D_MODEL)
    tab = pl.BlockSpec((tm, LANES), lambda i: (i % tpos, 0))
    in_specs = [row(D_MODEL), mod_spec, mod_spec, const((1, D_MODEL)), const((D_MODEL, _C_END)),
                const((1, Q_LORA)), const((Q_LORA, MLA_HEADS * LANES)), const((1, KV_LORA)),
                const((KV_LORA, MLA_HEADS * LANES)), const((1, MLA_HEADS * LANES)), const((1, MLA_HEADS * LANES)),
                const((1, LANES)), const((1, DSA_OUT)), const((1, DSA_OUT)), tab, tab, tab, tab,
                const((LANES, LANES)), const((LANES, LANES)), const((LANES, LANES)),
                const((LANES, IDX_HEADS * LANES))]
    widths = [(MLA_HEADS * LANES, BF16), (MLA_HEADS * LANES, BF16), (KV_LORA, F32), (KV_LORA, BF16), (LANES, F32),
              (DSA_OUT, BF16), (DSA_OUT, F32), (DSA_OUT, BF16), (DSA_OUT, F32), (DSA_OUT, BF16),
              (IDX_HEADS * IDX_DIM, BF16), (LANES, BF16), (IDX_HEADS * LANES, F32)]
    outs = pl.pallas_call(
        _proj_kernel,
        grid=(nt,),
        in_specs=in_specs,
        out_specs=[row(w) for w, _ in widths],
        out_shape=[jax.ShapeDtypeStruct((n, w), dt) for w, dt in widths],
        compiler_params=_cparams(("arbitrary",)),
        name="project",
    )(x2d, shift, scale, pw["ga"], pw["win"], pw["gql"], pw["wqu"], pw["gkv"], pw["wuk"],
      pw["gq"], pw["gk"], pw["gm"], pw["gdq"], pw["gdk"], cq, sq, cm, sm, pw["bq"], pw["b64"], pw["bm"],
      pw["ex"])
    names = ("qall", "kall", "ckv", "ckvb", "misc", "qd", "kd", "kdb", "vd", "vdb", "qi", "kid", "wib")
    return dict(zip(names, outs))


_MLA_TQ = 256
_MLA_TK = 512


def _mla_prompt_kernel(q_ref, k_ref, c_ref, wuv_ref, o_ref):
    tq, tk = _MLA_TQ, _MLA_TK
    i = pl.program_id(1)
    q0 = i * tq
    nkb = (q0 + tq + tk - 1) // tk
    row = q0 + lax.broadcasted_iota(I32, (tq, tk), 0)
    col0 = lax.broadcasted_iota(I32, (tq, tk), 1)
    heads = []
    for h in range(MLA_HEADS):
        qh = q_ref[:, h * LANES:(h + 1) * LANES]

        def body(j, carry, qh=qh, h=h):
            m, l, acc = carry
            k0 = pl.multiple_of(j * tk, tk)
            kh = k_ref[pl.ds(k0, tk), h * LANES:(h + 1) * LANES]
            s = _nt_dot(qh, kh) * MLA_SCALE
            s = jnp.where(col0 + k0 <= row, s, NEG_INF)
            m_new = jnp.maximum(m, jnp.max(s, axis=1, keepdims=True))
            alpha = jnp.exp(m - m_new)
            p = jnp.exp(s - m_new)
            l = alpha * l + jnp.sum(p, axis=1, keepdims=True)
            acc = alpha * acc + jnp.dot(p.astype(BF16), c_ref[pl.ds(k0, tk), :], preferred_element_type=F32)
            return m_new, l, acc

        init = (jnp.full((tq, 1), NEG_INF, F32), jnp.zeros((tq, 1), F32), jnp.zeros((tq, KV_LORA), F32))
        m, l, acc = lax.fori_loop(0, nkb, body, init)
        heads.append((acc / l).astype(BF16))
    for pr in range(MLA_HEADS // 2):
        o = (jnp.dot(heads[2 * pr], wuv_ref[2 * pr], preferred_element_type=F32)
             + jnp.dot(heads[2 * pr + 1], wuv_ref[2 * pr + 1], preferred_element_type=F32))
        o_ref[:, pr * LANES:(pr + 1) * LANES] = o.astype(BF16)


def _pad_wuv(w_uv):
    w = jnp.transpose(w_uv, (1, 0, 2))
    even = jnp.pad(w, ((0, 0), (0, 0), (0, LANES - MLA_V)))
    odd = jnp.pad(w, ((0, 0), (0, 0), (LANES - MLA_V, 0)))
    is_odd = (jnp.arange(MLA_HEADS) % 2 == 1)[:, None, None]
    return jnp.where(is_odd, odd, even).astype(BF16)


def _mla_prompt(qall, kall, ckvb, wuv_pad):
    b, t, _ = qall.shape
    tq = _MLA_TQ
    assert t % _MLA_TK == 0
    return pl.pallas_call(
        _mla_prompt_kernel,
        grid=(b, t // tq),
        in_specs=[pl.BlockSpec((None, tq, MLA_HEADS * LANES), lambda bi, i: (bi, i, 0)),
                  pl.BlockSpec((None, t, MLA_HEADS * LANES), lambda bi, i: (bi, 0, 0)),
                  pl.BlockSpec((None, t, KV_LORA), lambda bi, i: (bi, 0, 0)),
                  pl.BlockSpec((MLA_HEADS, KV_LORA, LANES), lambda bi, i: (0, 0, 0))],
        out_specs=pl.BlockSpec((None, tq, MLA_OUT), lambda bi, i: (bi, i, 0)),
        out_shape=jax.ShapeDtypeStruct((b, t, MLA_OUT), BF16),
        compiler_params=_cparams(("arbitrary", "arbitrary")),
        name="mla_prompt",
    )(qall, kall, ckvb, wuv_pad)


_DSA_T = 128
_DSA_CW = 512


def _sortable_key(score):
    bits = pltpu.bitcast(score, I32)
    key = jnp.where(bits < 0, bits ^ jnp.int32(0x7FFFFFFF), bits)
    return jnp.where(score == 0.0, 0, key)


def _bias_tiles(rb_ref, tz_ref):
    t = _DSA_T
    r = lax.broadcasted_iota(I32, (t, t), 0)
    c = lax.broadcasted_iota(I32, (t, t), 1)
    exact = REL_BUCKETS // 2
    for which in range(2):
        d = jnp.maximum(r - c + t * which, 0)
        logd = jnp.log(jnp.maximum(d, 1).astype(F32) / exact) / math.log(REL_MAX_DIST / exact)
        far = jnp.minimum(exact + (logd * (REL_BUCKETS - exact)).astype(I32), REL_BUCKETS - 1)
        bucket = jnp.where(d < exact, d, far)
        for h in range(DSA_HEADS):
            tile = jnp.zeros((t, t), F32)
            for bk in range(REL_BUCKETS):
                tile = jnp.where(bucket == bk, rb_ref[bk, h], tile)
            tz_ref[h, which] = tile


def _select_mask(keys_ref, mask_ref, tri_ref, nch, n_sel, rows):
    cw = _DSA_CW

    def count(pred):
        def body(j, acc):
            kb = keys_ref[:, pl.ds(pl.multiple_of(j * cw, cw), cw)]
            return acc + _fold_lanes(jnp.where(pred(kb), 1.0, 0.0), jnp.add)
        acc = lax.fori_loop(0, nch, body, jnp.zeros((rows, LANES), F32))
        return jnp.sum(acc, axis=1, keepdims=True)

    kf = jnp.float32(n_sel)
    zero = jnp.zeros((rows, 1), I32)
    thr = jnp.where(count(lambda kb: kb >= zero) >= kf, zero, jnp.full((rows, 1), INT_MIN, I32))

    def bit_body(bi, thr):
        cand = thr + lax.shift_left(jnp.int32(1), 30 - bi)
        return jnp.where(count(lambda kb: kb >= cand) >= kf, cand, thr)

    thr = lax.fori_loop(0, 31, bit_body, thr)
    need = kf - count(lambda kb: kb > thr)

    def mask_body(j, carry):
        k0 = pl.multiple_of(j * cw, cw)
        kb = keys_ref[:, pl.ds(k0, cw)]
        eq = kb == thr
        eqf = jnp.where(eq, 1.0, 0.0)
        before = jnp.dot(eqf.astype(BF16), tri_ref[...], preferred_element_type=F32) + carry
        keep = ((kb > thr) | (eq & (before < need))) & (kb > INT_MIN)
        mask_ref[:, pl.ds(k0, cw)] = jnp.where(keep, 0.0, NEG_INF)
        return carry + jnp.sum(eqf, axis=1, keepdims=True)

    lax.fori_loop(0, nch, mask_body, jnp.zeros((rows, 1), F32))


def _dsa_prompt_kernel(rb_ref, qi_ref, wib_ref, kid_ref, qd_ref, kd_ref, vd_ref, tri_ref, o_ref,
                       keys_ref, mask_ref, tz_ref, s_ref, *, n_sel):
    t, cw = _DSA_T, _DSA_CW
    sub_n = cw // t
    bi = pl.program_id(0)
    i = pl.program_id(1)
    nch = i // sub_n + 1

    @pl.when((bi == 0) & (i == 0))
    def _():
        _bias_tiles(rb_ref, tz_ref)

    row = i * t + lax.broadcasted_iota(I32, (t, cw), 0)
    col0 = lax.broadcasted_iota(I32, (t, cw), 1)
    lane = lax.broadcasted_iota(I32, (t, LANES), 1)
    low = lane < DSA_HEAD_DIM

    def head_halves(ref, pr):
        qs = ref[:, pr * LANES:(pr + 1) * LANES]
        zero = jnp.zeros_like(qs)
        return jnp.where(low, qs, zero), jnp.where(low, zero, qs)

    qim = [q for pr in range(IDX_HEADS // 2) for q in head_halves(qi_ref, pr)]

    def idx_body(c, _):
        k0 = pl.multiple_of(c * cw, cw)
        kk = kid_ref[pl.ds(k0, cw), :]
        acc = jnp.zeros((t, cw), F32)
        for h in range(IDX_HEADS):
            r = jnp.maximum(_nt_dot(qim[h], kk) * (IDX_DIM ** -0.5), 0.0)
            w = wib_ref[:, h * LANES:(h + 1) * LANES]
            acc = acc + jnp.concatenate([w] * sub_n, axis=1) * r
        keys_ref[:, pl.ds(k0, cw)] = jnp.where(col0 + k0 <= row, _sortable_key(acc), INT_MIN)
        return 0

    lax.fori_loop(0, nch, idx_body, 0)
    _select_mask(keys_ref, mask_ref, tri_ref, nch, n_sel, t)

    for pr in range(DSA_HEADS // 2):
        qms = head_halves(qd_ref, pr)
        fars = [rb_ref[REL_BUCKETS - 1, 2 * pr + half] for half in range(2)]

        def pass_a(c, mrun, pr=pr, qms=qms, fars=fars):
            k0 = pl.multiple_of(c * cw, cw)
            kb = kd_ref[pl.ds(k0, cw), pr * LANES:(pr + 1) * LANES]
            mk = mask_ref[:, pl.ds(k0, cw)]
            out = []
            for half in range(2):
                h = 2 * pr + half
                s = _nt_dot(qms[half], kb) * DSA_SCALE
                parts = []
                for sub in range(sub_n):
                    blk = c * sub_n + sub
                    bias = jnp.where(blk == i, tz_ref[h, 0], jnp.where(blk == i - 1, tz_ref[h, 1], fars[half]))
                    parts.append(s[:, sub * t:(sub + 1) * t] + bias)
                s = jnp.concatenate(parts, axis=1) + mk
                s_ref[half, :, pl.ds(k0, cw)] = s
                out.append(jnp.maximum(mrun[half], _fold_lanes(s, jnp.maximum)))
            return tuple(out)

        ninf = jnp.full((t, LANES), NEG_INF, F32)
        mrun = lax.fori_loop(0, nch, pass_a, (ninf, ninf))
        ms = [jnp.max(m, axis=1, keepdims=True) for m in mrun]

        def pass_b(c, carry, pr=pr, ms=ms):
            k0 = pl.multiple_of(c * cw, cw)
            vb = vd_ref[pl.ds(k0, cw), pr * LANES:(pr + 1) * LANES]
            out = []
            for half in range(2):
                l, acc = carry[half]
                p = jnp.exp(s_ref[half, :, pl.ds(k0, cw)] - ms[half])
                out.append((l + _fold_lanes(p, jnp.add),
                            acc + jnp.dot(p.astype(BF16), vb, preferred_element_type=F32)))
            return tuple(out)

        zero = jnp.zeros((t, LANES), F32)
        (l0, a0), (l1, a1) = lax.fori_loop(0, nch, pass_b, ((zero, zero), (zero, zero)))
        o0 = a0 / jnp.sum(l0, axis=1, keepdims=True)
        o1 = a1 / jnp.sum(l1, axis=1, keepdims=True)
        o_ref[:, pr * LANES:(pr + 1) * LANES] = jnp.where(low, o0, o1).astype(BF16)


def _strict_upper(n):
    return jnp.asarray(np.triu(np.ones((n, n), np.float32), 1), BF16)


def _dsa_prompt(rel_bias, qi, wib, kid, qd, kdb, vdb):
    b, t, _ = qi.shape
    tq, cw = _DSA_T, _DSA_CW
    assert t % cw == 0
    n_sel = min(IDX_TOPK_MAX, t // 4)
    blk = lambda w: pl.BlockSpec((None, tq, w), lambda bi, i: (bi, i, 0))
    full = lambda w: pl.BlockSpec((None, t, w), lambda bi, i: (bi, 0, 0))
    return pl.pallas_call(
        functools.partial(_dsa_prompt_kernel, n_sel=n_sel),
        grid=(b, t // tq),
        in_specs=[pl.BlockSpec(memory_space=pltpu.SMEM),
                  blk(IDX_HEADS * IDX_DIM), blk(IDX_HEADS * LANES), full(LANES), blk(DSA_OUT), full(DSA_OUT),
                  full(DSA_OUT), pl.BlockSpec((cw, cw), lambda bi, i: (0, 0))],
        out_specs=blk(DSA_OUT),
        out_shape=jax.ShapeDtypeStruct((b, t, DSA_OUT), BF16),
        scratch_shapes=[pltpu.VMEM((tq, t), I32), pltpu.VMEM((tq, t), F32),
                        pltpu.VMEM((DSA_HEADS, 2, tq, tq), F32), pltpu.VMEM((2, tq, t), F32)],
        compiler_params=_cparams(("arbitrary", "arbitrary")),
        name="dsa_prompt",
    )(rel_bias, qi, wib, kid, qd, kdb, vdb, _strict_upper(cw))


_EXPERTS_PER_GROUP = N_EXPERTS // N_GROUPS


def _first_index_of_max(v, idx, axis, sentinel):
    mx = jnp.max(v, axis=axis, keepdims=True)
    first = jnp.min(jnp.where(v == mx, idx, sentinel), axis=axis, keepdims=True)
    return mx, first


def _route(logits_t, bias_col):
    n_tok = logits_t.shape[1]
    scores = jax.nn.sigmoid(logits_t)
    biased = scores + bias_col
    b3 = biased.reshape(N_GROUPS, _EXPERTS_PER_GROUP, n_tok)
    j3 = lax.broadcasted_iota(I32, b3.shape, 1)
    m1, f1 = _first_index_of_max(b3, j3, 1, _EXPERTS_PER_GROUP)
    m2 = jnp.max(jnp.where(j3 == f1, NEG_INF, b3), axis=1, keepdims=True)
    gs = (m1 + m2).reshape(N_GROUPS, n_tok)
    gi = lax.broadcasted_iota(I32, gs.shape, 0)
    gsel = jnp.zeros(gs.shape, jnp.bool_)
    for _ in range(TOPK_GROUPS):
        _, first = _first_index_of_max(gs, gi, 0, N_GROUPS)
        hit = gi == first
        gsel = gsel | hit
        gs = jnp.where(hit, NEG_INF, gs)
    gsel3 = jnp.broadcast_to(gsel.reshape(N_GROUPS, 1, n_tok), b3.shape)
    masked = jnp.where(gsel3, b3, NEG_INF).reshape(N_EXPERTS, n_tok)
    ei = lax.broadcasted_iota(I32, masked.shape, 0)
    sel = jnp.zeros(masked.shape, jnp.bool_)
    for _ in range(TOP_K):
        _, first = _first_index_of_max(masked, ei, 0, N_EXPERTS)
        hit = ei == first
        sel = sel | hit
        masked = jnp.where(hit, NEG_INF, masked)
    w = jnp.where(sel, scores, 0.0)
    gate = w / jnp.sum(w, axis=0, keepdims=True) * ROUTED_SCALE
    return sel, gate


def _swiglu_bf(x_bf, wgu_ref, wdown_ref, d_hidden):
    gu = jnp.dot(x_bf, wgu_ref[...], preferred_element_type=F32)
    g, u = gu[:, :d_hidden], gu[:, d_hidden:]
    act = (g * jax.nn.sigmoid(g)) * u
    return jnp.dot(act.astype(BF16), wdown_ref[...], preferred_element_type=F32)


def _moe_kernel(x_ref, oa_ref, ob_ref, ga_ref, sf_ref, cf_ref, gf_ref, gn_ref, wo_ref, wr_ref, br_ref,
                wsg_ref, wsd_ref, tri_ref, weg_ref, wed_ref, y_ref, xt_ref, acc_ref, gate_ref, rank_ref,
                *, cap):
    tm = x_ref.shape[0]
    e = pl.program_id(1)

    @pl.when(e == 0)
    def _():
        half = oa_ref.shape[1]
        attn = (jnp.dot(oa_ref[...], wo_ref[:half, :], preferred_element_type=F32)
                + jnp.dot(ob_ref[...], wo_ref[half:, :], preferred_element_type=F32))
        x1 = x_ref[...] + ga_ref[...] * attn
        y_ref[...] = x1
        hn = x1 * lax.rsqrt(jnp.mean(x1 * x1, axis=-1, keepdims=True) + EPS) * gn_ref[...]
        xt = (hn * (1.0 + cf_ref[...]) + sf_ref[...]).astype(BF16)
        xt_ref[...] = xt
        sel, gate = _route(_nt_dot(wr_ref[...], xt), br_ref[...])
        ind = jnp.where(sel, 1.0, 0.0)
        before = jnp.dot(ind.astype(BF16), tri_ref[...], preferred_element_type=F32)
        rank_ref[...] = jnp.where(sel, before, -1.0)
        gate_ref[...] = jnp.where(sel, gate, 0.0)
        acc_ref[...] = jnp.zeros_like(acc_ref)

    rank_row = rank_ref[pl.ds(e, 1), :]
    gate_row = gate_ref[pl.ds(e, 1), :]
    count = (jnp.max(rank_row) + 1.0).astype(I32)
    n_chunks = (count + cap - 1) // cap
    slot = lax.broadcasted_iota(I32, (cap, tm), 0).astype(F32)

    def chunk(c, _):
        pick = slot + (c * cap).astype(F32) == rank_row
        pick_f = jnp.where(pick, 1.0, 0.0)
        pick_b = pick_f.astype(BF16)
        xe = jnp.dot(pick_b, xt_ref[...], preferred_element_type=F32).astype(BF16)
        ye = _swiglu_bf(xe, weg_ref, wed_ref, D_EXPERT)
        ge = jnp.sum(pick_f * gate_row, axis=1, keepdims=True)
        ys = (ye * ge).astype(BF16)
        acc_ref[...] += _tn_dot(pick_b, ys)
        return 0

    lax.fori_loop(0, n_chunks, chunk, 0)

    @pl.when(e == N_EXPERTS - 1)
    def _():
        shared = _swiglu_bf(xt_ref[...], wsg_ref, wsd_ref, D_SHARED)
        y_ref[...] = y_ref[...] + gf_ref[...] * (acc_ref[...] + shared)


def _moe(x2d, oa, ob, mods, g_ffn, wo_bf, wr_t_bf, b_router, wsg_bf, wsd_bf, weg_bf, wed_bf, tm, rows_per_mod):
    n = x2d.shape[0]
    assert n % tm == 0 and tm % LANES == 0
    nt = n // tm
    cap = LANES
    const = lambda shape: pl.BlockSpec(shape, lambda t, e: (0,) * len(shape))
    row = lambda w: pl.BlockSpec((tm, w), lambda t, e: (t, 0))
    if rows_per_mod:
        per = rows_per_mod // tm
        mod_spec = pl.BlockSpec((None, 1, D_MODEL), lambda t, e: (t // per, 0, 0))
    else:
        mod_spec = row(D_MODEL)
    half = oa.shape[1]
    return pl.pallas_call(
        functools.partial(_moe_kernel, cap=cap),
        grid=(nt, N_EXPERTS),
        in_specs=[row(D_MODEL), row(half), row(half), mod_spec, mod_spec, mod_spec, mod_spec,
                  const((1, D_MODEL)), const((2 * half, D_MODEL)), const((N_EXPERTS, D_MODEL)),
                  const((N_EXPERTS, 1)), const((D_MODEL, 2 * D_SHARED)), const((D_SHARED, D_MODEL)),
                  const((tm, tm)),
                  pl.BlockSpec((None, D_MODEL, 2 * D_EXPERT), lambda t, e: (e, 0, 0)),
                  pl.BlockSpec((None, D_EXPERT, D_MODEL), lambda t, e: (e, 0, 0))],
        out_specs=row(D_MODEL),
        out_shape=jax.ShapeDtypeStruct((n, D_MODEL), F32),
        scratch_shapes=[pltpu.VMEM((tm, D_MODEL), BF16), pltpu.VMEM((tm, D_MODEL), F32),
                        pltpu.VMEM((N_EXPERTS, tm), F32), pltpu.VMEM((N_EXPERTS, tm), F32)],
        compiler_params=_cparams(("arbitrary", "arbitrary")),
        name="moe",
    )(x2d, oa, ob, *mods, g_ffn.reshape(1, -1), wo_bf, wr_t_bf, b_router.reshape(-1, 1), wsg_bf, wsd_bf,
      _strict_upper(tm), weg_bf, wed_bf)


def _col_blocks(q):
    nh = q.shape[0]
    r = lax.broadcasted_iota(I32, (nh, LANES), 0)
    c = lax.broadcasted_iota(I32, (nh, LANES), 1)
    blocks = [_tn_dot(q, jnp.where((r == h) & (c == h), 1.0, 0.0).astype(q.dtype)) for h in range(nh)]
    return jnp.concatenate(blocks, axis=0)


def _rows_to_cols(row):
    r8 = jnp.broadcast_to(row, (8, row.shape[1]))
    e0 = jnp.where(lax.broadcasted_iota(I32, (8, LANES), 0) == 0, 1.0, 0.0).astype(row.dtype)
    return _tn_dot(r8, e0)


def _head_diag(acc, width):
    r = lax.broadcasted_iota(I32, acc.shape, 0)
    c = lax.broadcasted_iota(I32, acc.shape, 1)
    return jnp.sum(jnp.where(r == c // width, acc, 0.0), axis=0, keepdims=True)


class _PageStream:
    def __init__(self, hbm_ref, buf_ref, sem_ref, pt_ref, seq, pages_per_chunk):
        self.hbm, self.buf, self.sem, self.pt, self.seq, self.ppc = hbm_ref, buf_ref, sem_ref, pt_ref, seq, pages_per_chunk

    def _copy(self, page, slot, r):
        return pltpu.make_async_copy(self.hbm.at[page], self.buf.at[slot, pl.ds(r * PAGE_SIZE, PAGE_SIZE)],
                                     self.sem.at[slot])

    def start(self, chunk, slot):
        for r in range(self.ppc):
            self._copy(self.pt[self.seq, chunk * self.ppc + r], slot, r).start()

    def wait(self, slot):
        for r in range(self.ppc):
            self._copy(0, slot, r).wait()


_STREAM_DEPTH = 4


def _stream_loop(streams, n_chunks, body):
    depth = _STREAM_DEPTH
    for d in range(min(depth - 1, n_chunks)):
        for st in streams:
            st.start(d, d)

    def step(c, _):
        slot = c % depth
        nxt = c + depth - 1

        @pl.when(nxt < n_chunks)
        def _():
            for st in streams:
                st.start(nxt, nxt % depth)

        for st in streams:
            st.wait(slot)
        body(c, slot)
        return 0

    lax.fori_loop(0, n_chunks, step, 0)


def _softmax_stats(sc_ref, n_rows, blk):
    nb = n_rows // blk
    tail = n_rows - nb * blk

    def mx_body(j, m):
        return jnp.maximum(m, jnp.max(sc_ref[pl.ds(pl.multiple_of(j * blk, blk), blk), :], axis=0, keepdims=True))

    m = lax.fori_loop(0, nb, mx_body, jnp.full((1, LANES), NEG_INF, F32))
    if tail:
        m = jnp.maximum(m, jnp.max(sc_ref[pl.ds(nb * blk, tail), :], axis=0, keepdims=True))

    def sum_body(j, l):
        return l + jnp.sum(jnp.exp(sc_ref[pl.ds(pl.multiple_of(j * blk, blk), blk), :] - m), axis=0, keepdims=True)

    l = lax.fori_loop(0, nb, sum_body, jnp.zeros((1, LANES), F32))
    if tail:
        l = l + jnp.sum(jnp.exp(sc_ref[pl.ds(nb * blk, tail), :] - m), axis=0, keepdims=True)
    return m, l


_MLA_S_PAGES = 4


def _mla_sample_kernel(pt_ref, q_ref, knew_ref, cnew_ref, wuk_ref, gk_ref, wuv_ref, lat_hbm, kr_hbm, o_ref,
                       latbuf, krbuf, sems, sc_ref, latbf_ref, *, n_pages):
    s_id = pl.program_id(0)
    ppc = _MLA_S_PAGES
    ck = ppc * PAGE_SIZE
    n_chunks = n_pages // ppc
    past = n_pages * PAGE_SIZE
    row8 = lax.broadcasted_iota(I32, (8, LANES), 0)

    qblk = _col_blocks(q_ref[...]).astype(BF16)
    qr = qblk[MLA_NOPE:MLA_NOPE + MLA_ROPE, :].astype(F32)
    for h in range(1, MLA_HEADS):
        qr = qr + qblk[h * LANES + MLA_NOPE:h * LANES + MLA_NOPE + MLA_ROPE, :].astype(F32)
    qr = qr.astype(BF16)
    lat_stream = _PageStream(lat_hbm, latbuf, sems.at[0], pt_ref, s_id, ppc)
    kr_stream = _PageStream(kr_hbm, krbuf, sems.at[1], pt_ref, s_id, ppc)

    def score_chunk(c, slot):
        lat = latbuf[slot].astype(BF16)
        latbf_ref[pl.ds(pl.multiple_of(c * ck, ck), ck), :] = lat
        kraw = jnp.dot(lat, wuk_ref[...], preferred_element_type=F32)
        slabs = []
        for h in range(MLA_HEADS):
            x = kraw[:, h * LANES:(h + 1) * LANES]
            ms = jnp.sum(x * x, axis=1, keepdims=True) * (1.0 / MLA_NOPE)
            slabs.append((x * lax.rsqrt(ms + EPS) * gk_ref[:, h * LANES:(h + 1) * LANES]).astype(BF16))
        kn = jnp.concatenate(slabs, axis=1)
        s = (jnp.dot(kn, qblk, preferred_element_type=F32)
             + jnp.dot(krbuf[slot].astype(BF16), qr, preferred_element_type=F32)) * MLA_SCALE
        sc_ref[pl.ds(pl.multiple_of(c * ck, ck), ck), :] = s

    _stream_loop((lat_stream, kr_stream), n_chunks, score_chunk)
    s_new = jnp.dot(jnp.broadcast_to(knew_ref[...], (8, knew_ref.shape[1])), qblk,
                    preferred_element_type=F32) * MLA_SCALE
    sc_ref[pl.ds(past, 8), :] = jnp.where(row8 == 0, s_new, NEG_INF)
    m, l = _softmax_stats(sc_ref, past + 8, ck)

    def pv_chunk(c, acc):
        k0 = pl.multiple_of(c * ck, ck)
        p = jnp.exp(sc_ref[pl.ds(k0, ck), :] - m) / l
        return acc + _tn_dot(p.astype(BF16), latbf_ref[pl.ds(k0, ck), :])

    acc = lax.fori_loop(0, n_chunks, pv_chunk, jnp.zeros((LANES, KV_LORA), F32))
    p_new = jnp.exp(sc_ref[pl.ds(past, 8), :] - m) / l
    acc = acc + _tn_dot(p_new.astype(BF16), jnp.broadcast_to(cnew_ref[...], (8, KV_LORA)))
    out = jnp.dot(acc.astype(BF16), wuv_ref[...], preferred_element_type=F32)
    o_ref[...] = _head_diag(out, MLA_V).astype(BF16)


def _mla_sample(page_table, q8, knew, cnew, wuk_pad, gk, wuv_flat, cache_lat, cache_kr):
    db, n_pages = page_table.shape
    ck = _MLA_S_PAGES * PAGE_SIZE
    past = n_pages * PAGE_SIZE
    per_seq = lambda shape: pl.BlockSpec((None,) + shape, lambda s, pt: (s,) + (0,) * len(shape))
    const = lambda shape: pl.BlockSpec(shape, lambda s, pt: (0,) * len(shape))
    grid_spec = pltpu.PrefetchScalarGridSpec(
        num_scalar_prefetch=1, grid=(db,),
        in_specs=[per_seq((MLA_HEADS, LANES)), per_seq((1, MLA_HEADS * LANES)), per_seq((1, KV_LORA)),
                  const((KV_LORA, MLA_HEADS * LANES)), const((1, MLA_HEADS * LANES)), const((KV_LORA, MLA_OUT)),
                  pl.BlockSpec(memory_space=pl.ANY), pl.BlockSpec(memory_space=pl.ANY)],
        out_specs=per_seq((1, MLA_OUT)),
        scratch_shapes=[pltpu.VMEM((_STREAM_DEPTH, ck, KV_LORA), F32), pltpu.VMEM((_STREAM_DEPTH, ck, MLA_ROPE), F32),
                        pltpu.SemaphoreType.DMA((2, _STREAM_DEPTH)), pltpu.VMEM((past + 8, LANES), F32),
                        pltpu.VMEM((past, KV_LORA), BF16)])
    return pl.pallas_call(
        functools.partial(_mla_sample_kernel, n_pages=n_pages),
        grid_spec=grid_spec,
        out_shape=jax.ShapeDtypeStruct((db, 1, MLA_OUT), BF16),
        compiler_params=_cparams(("arbitrary",)),
        name="mla_sample",
    )(page_table, q8, knew, cnew, wuk_pad, gk, wuv_flat, cache_lat, cache_kr)


_DSA_S_IDX_PAGES = 8


def _select_flat(keys, tri_u, tri_l, n_sel):
    def count(pred):
        c = jnp.sum(jnp.where(pred, 1.0, 0.0), axis=1, keepdims=True)
        return jnp.sum(c, axis=0, keepdims=True)

    kf = jnp.float32(n_sel)
    zero = jnp.zeros((1, 1), I32)
    thr = jnp.where(count(keys >= zero) >= kf, zero, jnp.full((1, 1), INT_MIN, I32))

    def bit_body(bi, thr):
        cand = thr + lax.shift_left(jnp.int32(1), 30 - bi)
        return jnp.where(count(keys >= cand) >= kf, cand, thr)

    thr = lax.fori_loop(0, 31, bit_body, thr)
    need = kf - count(keys > thr)
    eq = keys == thr
    eqf = jnp.where(eq, 1.0, 0.0)
    within = jnp.dot(eqf.astype(BF16), tri_u, preferred_element_type=F32)
    rowcount = jnp.broadcast_to(jnp.sum(eqf, axis=1, keepdims=True), eqf.shape)
    carry = jnp.dot(tri_l, rowcount.astype(BF16), preferred_element_type=F32)
    keep = ((keys > thr) | (eq & (within + carry < need))) & (keys > INT_MIN)
    return jnp.where(keep, 1.0, 0.0)


def _t5_buckets(n):
    exact = REL_BUCKETS // 2
    d = np.arange(n)
    logd = np.log(np.maximum(d, 1).astype(np.float32) / np.float32(exact)) / np.float32(math.log(REL_MAX_DIST / exact))
    far = np.minimum(exact + (logd * (REL_BUCKETS - exact)).astype(np.int32), REL_BUCKETS - 1)
    return [int(b) for b in np.where(d < exact, d, far)]


_GATHER_UNROLL = 8


def _dsa_gather_kernel(pt_ref, rbt_ref, qi_ref, wi_ref, kinew_ref, qd_ref, kdnew_ref, vdnew_ref,
                       triu_ref, tril_ref, idx_hbm, k_hbm, v_hbm, o_ref,
                       idxbuf, sems, possem, keys_ref, rank_ref, posv_ref, pos_smem, kbuf, vbuf, bbuf, tab_ref,
                       *, n_pages, n_sel):
    s_id = pl.program_id(0)
    past = n_pages * PAGE_SIZE
    lane1 = lax.broadcasted_iota(I32, (1, LANES), 1)
    qi = qi_ref[...]
    wi = wi_ref[...]

    @pl.when(s_id == 0)
    def _():
        for d, bk in enumerate(_t5_buckets(REL_MAX_DIST + 1)):
            tab_ref[d] = jnp.broadcast_to(rbt_ref[:, bk:bk + 1], (DSA_HEADS, LANES))

    keys_ref[...] = jnp.full(keys_ref.shape, INT_MIN, I32)
    idx_stream = _PageStream(idx_hbm, idxbuf, sems.at[0], pt_ref, s_id, _DSA_S_IDX_PAGES)

    def idx_chunk(c, slot):
        for r in range(_DSA_S_IDX_PAGES):
            kk = idxbuf[slot, r * PAGE_SIZE:(r + 1) * PAGE_SIZE, :].astype(BF16)
            rr = jnp.maximum(_nt_dot(qi, kk) * (IDX_DIM ** -0.5), 0.0)
            sc = jnp.sum(wi * rr, axis=0, keepdims=True)
            keys_ref[pl.ds(c * _DSA_S_IDX_PAGES + r, 1), :] = _sortable_key(sc)

    _stream_loop((idx_stream,), n_pages // _DSA_S_IDX_PAGES, idx_chunk)
    r_new = jnp.maximum(jnp.sum(qi.astype(F32) * kinew_ref[...].astype(F32), axis=1, keepdims=True)
                        * (IDX_DIM ** -0.5), 0.0)
    sc_new = jnp.sum(wi * r_new, axis=0, keepdims=True)
    keys_ref[pl.ds(n_pages, 1), :] = jnp.where(lane1 == 0, _sortable_key(jnp.broadcast_to(sc_new, (1, LANES))),
                                               INT_MIN)

    mask = _select_flat(keys_ref[...], triu_ref[...], tril_ref[...], n_sel)
    within = jnp.dot(mask.astype(BF16), triu_ref[...], preferred_element_type=F32)
    rowcount = jnp.broadcast_to(jnp.sum(mask, axis=1, keepdims=True), mask.shape)
    carry = jnp.dot(tril_ref[...], rowcount.astype(BF16), preferred_element_type=F32)
    rank_ref[...] = jnp.where(mask > 0.5, within + carry, -1.0)

    row8 = lax.broadcasted_iota(I32, (8, LANES), 0)
    lane8 = lax.broadcasted_iota(I32, (8, LANES), 1)
    lhs = jnp.where(row8 == 0, lane8.astype(F32), jnp.where(row8 == 1, 1.0, 0.0)).astype(BF16)
    slot_lane = lax.broadcasted_iota(I32, (LANES, n_sel), 1).astype(F32)
    n_groups = -(-(n_pages + 1) // _GATHER_UNROLL)

    def compact(g, acc):
        offs, pages = acc
        for u in range(_GATHER_UNROLL):
            i = g * _GATHER_UNROLL + u
            rc = _rows_to_cols(rank_ref[pl.ds(i, 1), :].astype(BF16))
            hit = jnp.concatenate([rc] * (n_sel // LANES), axis=1) == slot_lane
            res = jnp.dot(lhs, jnp.where(hit, 1.0, 0.0).astype(BF16), preferred_element_type=F32)
            offs = offs + res
            pages = pages + res * jnp.asarray(i, F32)
        return offs, pages

    zero = jnp.zeros((8, n_sel), F32)
    offs, pages = lax.fori_loop(0, n_groups, compact, (zero, zero))
    pos_row = (pages[1:2, :] * PAGE_SIZE + offs[0:1, :]).astype(I32)
    posv_ref[...] = jnp.broadcast_to(pos_row, posv_ref.shape)
    pos_copy = pltpu.make_async_copy(posv_ref, pos_smem, possem.at[0])
    pos_copy.start()
    pos_copy.wait()

    def k_copy(page, r, j):
        return pltpu.make_async_copy(k_hbm.at[page, r], kbuf.at[j], sems.at[1, 0])

    def v_copy(page, r, j):
        return pltpu.make_async_copy(v_hbm.at[page, r], vbuf.at[j], sems.at[1, 1])

    def issue(j, jnew):
        pos = pos_smem[0, j]
        pc = jnp.minimum(pos, past - 1)
        page = pt_ref[s_id, pc // PAGE_SIZE]
        r = pc % PAGE_SIZE
        k_copy(page, r, j).start()
        v_copy(page, r, j).start()
        bbuf[j] = tab_ref[jnp.minimum(past - pos, REL_MAX_DIST)]
        return jnp.where(pos >= past, j, jnew)

    jnew = lax.fori_loop(0, n_sel, issue, jnp.int32(-1))

    def drain(j, _):
        k_copy(0, 0, j).wait()
        v_copy(0, 0, j).wait()
        return 0

    lax.fori_loop(0, n_sel, drain, 0)

    @pl.when(jnew >= 0)
    def _():
        kbuf[jnew] = kdnew_ref[...]
        vbuf[jnew] = vdnew_ref[...]

    k3 = kbuf[...].astype(BF16).astype(F32)
    q3 = qd_ref[...].astype(F32)
    s3 = jnp.sum(k3 * q3[None], axis=-1, keepdims=True) * DSA_SCALE + bbuf[...]
    m = jnp.max(s3, axis=0, keepdims=True)
    p = jnp.exp(s3 - m)
    pn = (p / jnp.sum(p, axis=0, keepdims=True)).astype(BF16).astype(F32)
    v3 = vbuf[...].astype(BF16).astype(F32)
    o_ref[...] = jnp.sum(pn[:, :, :DSA_HEAD_DIM] * v3, axis=0).astype(BF16)


def _dsa_gather(page_table, rel_bias, qi8, wi8, kinew, qd8, kdnew, vdnew, cache_idx, cache_k, cache_v):
    db, n_pages = page_table.shape
    past = n_pages * PAGE_SIZE
    n_sel = min(IDX_TOPK_MAX, (past + 1) // 4)
    assert n_sel % LANES == 0
    rows = -(-(n_pages + _GATHER_UNROLL) // LANES) * LANES
    tri_l = jnp.asarray(np.tril(np.ones((rows, rows), np.float32), -1), BF16)
    per_seq = lambda shape: pl.BlockSpec((None,) + shape, lambda s, pt: (s,) + (0,) * len(shape))
    const = lambda shape: pl.BlockSpec(shape, lambda s, pt: (0,) * len(shape))
    any_spec = pl.BlockSpec(memory_space=pl.ANY)
    head_tile = (DSA_HEADS, DSA_HEAD_DIM)
    grid_spec = pltpu.PrefetchScalarGridSpec(
        num_scalar_prefetch=1, grid=(db,),
        in_specs=[const((DSA_HEADS, REL_BUCKETS)),
                  per_seq((IDX_HEADS, IDX_DIM)), per_seq((IDX_HEADS, 1)), per_seq((1, IDX_DIM)),
                  per_seq(head_tile), per_seq(head_tile), per_seq(head_tile),
                  const((LANES, LANES)), const((rows, rows)), any_spec, any_spec, any_spec],
        out_specs=per_seq(head_tile),
        scratch_shapes=[pltpu.VMEM((_STREAM_DEPTH, PAGE_SIZE * _DSA_S_IDX_PAGES, IDX_DIM), F32),
                        pltpu.SemaphoreType.DMA((2, _STREAM_DEPTH)), pltpu.SemaphoreType.DMA((1,)),
                        pltpu.VMEM((rows, LANES), I32), pltpu.VMEM((rows, LANES), F32),
                        pltpu.VMEM((8, n_sel), I32), pltpu.SMEM((8, n_sel), I32),
                        pltpu.VMEM((n_sel,) + head_tile, F32), pltpu.VMEM((n_sel,) + head_tile, F32),
                        pltpu.VMEM((n_sel, DSA_HEADS, LANES), F32),
                        pltpu.VMEM((REL_MAX_DIST + 1, DSA_HEADS, LANES), F32)])
    return pl.pallas_call(
        functools.partial(_dsa_gather_kernel, n_pages=n_pages, n_sel=n_sel),
        grid_spec=grid_spec,
        out_shape=jax.ShapeDtypeStruct((db,) + head_tile, BF16),
        compiler_params=_cparams(("arbitrary",)),
        name="dsa_sample",
    )(page_table, rel_bias.T, qi8, wi8, kinew, qd8, kdnew, vdnew, _strict_upper(LANES), tri_l,
      cache_idx, cache_k, cache_v)


def kernel(x_prompt, x_sample, cache_mla_latent, cache_mla_krope, cache_dsa_k, cache_dsa_v, cache_idx_k, page_table, c_prompt, c_sample, rel_bias, w_ada, b_ada, g_attn_norm, w_in, g_q_lora, w_q_up, g_kv_lora, w_kv_up, g_mla_qn, g_mla_qr, g_mla_kn, g_mla_kr, g_dsa_q, g_dsa_k, w_out, g_ffn_norm, w_router, b_router, w_e_gu, w_e_down, w_s_gu, w_s_down):
    depth = w_ada.shape[0]
    assert depth == 1, "single-layer trunk"
    l = 0
    B, T, D = x_prompt.shape
    DB, TS, _ = x_sample.shape
    assert TS == 1, "one new token per sampled sequence"
    ns = DB * TS
    past = page_table.shape[1] * PAGE_SIZE

    w_kv = w_kv_up[l].reshape(KV_LORA, MLA_HEADS, MLA_NOPE + MLA_V)
    w_uk, w_uv = w_kv[..., :MLA_NOPE], w_kv[..., MLA_NOPE:]

    mod = _adaln(jnp.concatenate([c_prompt, c_sample], axis=0), w_ada[l].astype(BF16), b_ada[l])
    mod_p = [m.reshape(B, 1, D) for m in jnp.split(mod[:B], 6, axis=-1)]
    mod_s = jnp.split(mod[B:], 6, axis=-1)

    pw = _prep_proj_weights(w_in[l], g_q_lora[l], w_q_up[l], g_kv_lora[l], w_uk, g_mla_qn[l], g_mla_qr[l],
                            g_mla_kn[l], g_mla_kr[l], g_dsa_q[l], g_dsa_k[l], g_attn_norm[l])
    pp = _project(x_prompt.reshape(B * T, D), mod_p[0], mod_p[1], pw, _rope_tables(jnp.arange(T)), 256, T)
    ps = _project(x_sample.reshape(ns, D), mod_s[0], mod_s[1], pw,
                  _rope_tables(jnp.tile(past + jnp.arange(TS), DB)), ns, 0)

    r3 = lambda a: a.reshape(B, T, a.shape[-1])
    o_mla_p = _mla_prompt(r3(pp["qall"]), r3(pp["kall"]), r3(pp["ckvb"]), _pad_wuv(w_uv))
    o_dsa_p = _dsa_prompt(rel_bias, r3(pp["qi"]), r3(pp["wib"]), r3(pp["kid"]), r3(pp["qd"]), r3(pp["kdb"]),
                          r3(pp["vdb"]))

    o_mla_s = _mla_sample(page_table, ps["qall"].reshape(ns, MLA_HEADS, LANES), ps["kall"].reshape(ns, 1, -1),
                          ps["ckvb"].reshape(ns, 1, KV_LORA), pw["wuk"], pw["gk"],
                          w_uv.reshape(KV_LORA, MLA_OUT).astype(BF16), cache_mla_latent[l], cache_mla_krope[l])
    heads3 = lambda a: a.reshape(ns, DSA_HEADS, DSA_HEAD_DIM)
    o_dsa_s = _dsa_gather(page_table, rel_bias, ps["qi"].reshape(ns, IDX_HEADS, IDX_DIM),
                          ps["misc"][:, _M_WI:_M_WI + IDX_HEADS].reshape(ns, IDX_HEADS, 1),
                          ps["kid"][:, :IDX_DIM].reshape(ns, 1, IDX_DIM),
                          heads3(ps["qd"]), heads3(ps["kd"]), heads3(ps["vd"]),
                          cache_idx_k[l], cache_dsa_k[l], cache_dsa_v[l])

    moe_w = (g_ffn_norm[l], w_out[l].astype(BF16), w_router[l].T.astype(BF16), b_router[l],
             w_s_gu[l].astype(BF16), w_s_down[l].astype(BF16),
             w_e_gu[l].astype(BF16), w_e_down[l].astype(BF16))
    xp = _moe(x_prompt.reshape(B * T, D), o_mla_p.reshape(B * T, MLA_OUT), o_dsa_p.reshape(B * T, DSA_OUT),
              (mod_p[2], mod_p[3], mod_p[4], mod_p[5]), *moe_w, min(1024, T), T).reshape(B, T, D)
    ns_pad = -(-ns // LANES) * LANES
    pad_rows = lambda a: jnp.pad(a, ((0, ns_pad - ns), (0, 0)))
    xs = _moe(pad_rows(x_sample.reshape(ns, D)), pad_rows(o_mla_s.reshape(ns, MLA_OUT)),
              pad_rows(o_dsa_s.reshape(ns, DSA_OUT)), tuple(pad_rows(mod_s[k]) for k in (2, 3, 4, 5)),
              *moe_w, ns_pad, 0)[:ns].reshape(DB, TS, D)

    def caches(p, nb, nt):
        return (p["ckv"].reshape(1, nb, nt, KV_LORA),
                p["misc"][:, :MLA_ROPE].reshape(1, nb, nt, MLA_ROPE),
                p["kd"].reshape(1, nb, nt, DSA_HEADS, DSA_HEAD_DIM),
                p["vd"].reshape(1, nb, nt, DSA_HEADS, DSA_HEAD_DIM),
                p["misc"][:, _M_KI:_M_KI + IDX_DIM].reshape(1, nb, nt, IDX_DIM))

    return (xp, xs) + caches(pp, B, T) + caches(ps, DB, TS)
```

```python
import functools
import math

import jax
import jax.numpy as jnp
import numpy as np
from jax import lax
from jax.experimental import pallas as pl
from jax.experimental.pallas import tpu as pltpu

F32 = jnp.float32
BF16 = jnp.bfloat16
I32 = jnp.int32

D_MODEL = 1024
PAGE_SIZE = 128
EPS = 1e-6
MLA_HEADS = 8
MLA_NOPE = 64
MLA_ROPE = 32
MLA_V = 64
Q_LORA = 256
KV_LORA = 128
ROPE_BASE = 10000.0
MLA_SCALE = (MLA_NOPE + MLA_ROPE) ** -0.5
DSA_HEADS = 8
DSA_HEAD_DIM = 64
DSA_SCALE = DSA_HEAD_DIM ** -0.5
IDX_HEADS = 8
IDX_DIM = 64
IDX_TOPK_MAX = 256
REL_BUCKETS = 32
REL_MAX_DIST = 128
N_EXPERTS = 64
TOP_K = 6
N_GROUPS = 8
TOPK_GROUPS = 4
D_EXPERT = 256
D_SHARED = 256
ROUTED_SCALE = 2.5
MLA_OUT = MLA_HEADS * MLA_V
DSA_OUT = DSA_HEADS * DSA_HEAD_DIM
IN_SIZES = (Q_LORA, KV_LORA, MLA_ROPE, DSA_OUT, DSA_OUT, DSA_OUT, IDX_HEADS * IDX_DIM, IDX_DIM, IDX_HEADS)

LANES = 128
INT_MIN = -(2 ** 31)
NEG_INF = float("-inf")
VMEM_LIMIT = 56 * 1024 * 1024


def _cparams(sem):
    return pltpu.CompilerParams(dimension_semantics=sem, vmem_limit_bytes=VMEM_LIMIT)


def _split_dot(x, m01, passes=3):
    acc = None
    r = x
    for p in range(passes):
        hi = r.astype(BF16)
        part = jnp.dot(hi, m01, preferred_element_type=F32)
        acc = part if acc is None else acc + part
        if p + 1 < passes:
            r = r - hi.astype(F32)
    return acc


def _group_mean(sq, bmat):
    outs = [_split_dot(sq[:, s * LANES:(s + 1) * LANES], bmat) for s in range(sq.shape[1] // LANES)]
    return outs[0] if len(outs) == 1 else jnp.concatenate(outs, axis=1)


def _rope_slabs(x, cos, sin):
    lane = lax.broadcasted_iota(I32, (x.shape[0], LANES), 1)
    first_half = (lane % MLA_ROPE) < (MLA_ROPE // 2)
    outs = []
    for s in range(x.shape[1] // LANES):
        xs = x[:, s * LANES:(s + 1) * LANES]
        rot = jnp.where(first_half, pltpu.roll(xs, LANES - MLA_ROPE // 2, 1), pltpu.roll(xs, MLA_ROPE // 2, 1))
        outs.append(xs * cos + rot * sin)
    return outs[0] if len(outs) == 1 else jnp.concatenate(outs, axis=1)


def _nt_dot(a, b):
    return lax.dot_general(a, b, (((1,), (1,)), ((), ())), preferred_element_type=F32)


def _tn_dot(a, b):
    return lax.dot_general(a, b, (((0,), (0,)), ((), ())), preferred_element_type=F32)


def _fold_lanes(x, op):
    acc = x[:, :LANES]
    for s in range(1, x.shape[1] // LANES):
        acc = op(acc, x[:, s * LANES:(s + 1) * LANES])
    return acc


def _adaln_kernel(c_ref, w_ref, b_ref, o_ref):
    c = c_ref[...]
    s = (c * jax.nn.sigmoid(c)).astype(BF16)
    o_ref[...] = jnp.dot(s, w_ref[...], preferred_element_type=F32) + b_ref[...]


def _adaln(c, w_bf, b):
    rows = c.shape[0]
    n = w_bf.shape[1]
    tn = 1536
    return pl.pallas_call(
        _adaln_kernel,
        grid=(n // tn,),
        in_specs=[pl.BlockSpec((rows, D_MODEL), lambda j: (0, 0)),
                  pl.BlockSpec((D_MODEL, tn), lambda j: (0, j)),
                  pl.BlockSpec((1, tn), lambda j: (0, j))],
        out_specs=pl.BlockSpec((rows, tn), lambda j: (0, j)),
        out_shape=jax.ShapeDtypeStruct((rows, n), F32),
        compiler_params=_cparams(("arbitrary",)),
        name="adaln",
    )(c, w_bf, b.reshape(1, n))


_C_QLAT = 0
_C_KV = _C_QLAT + Q_LORA
_C_QD = _C_KV + KV_LORA
_C_KD = _C_QD + DSA_OUT
_C_VD = _C_KD + DSA_OUT
_C_QI = _C_VD + DSA_OUT
_C_MISC = _C_QI + IDX_HEADS * IDX_DIM
_C_KIDUP = _C_MISC + LANES
_C_END = _C_KIDUP + LANES
_M_KI = MLA_ROPE
_M_WI = MLA_ROPE + IDX_DIM


def _proj_kernel(x_ref, sh_ref, sc_ref, ga_ref, win_ref, gql_ref, wqu_ref, gkv_ref, wuk_ref,
                 gq_ref, gk_ref, gm_ref, gdq_ref, gdk_ref, cq_ref, sq_ref, cm_ref, sm_ref,
                 bq_ref, b64_ref, bm_ref, ex_ref,
                 qall_ref, kall_ref, ckv_ref, ckvb_ref, misc_ref, qd_ref, kd_ref, kdb_ref, vd_ref, vdb_ref,
                 qi_ref, kid_ref, wib_ref):
    x = x_ref[...]
    xn = x * lax.rsqrt(jnp.mean(x * x, axis=-1, keepdims=True) + EPS) * ga_ref[...]
    h = xn * (1.0 + sc_ref[...]) + sh_ref[...]
    p = jnp.dot(h.astype(BF16), win_ref[...], preferred_element_type=F32)

    ql = p[:, _C_QLAT:_C_KV]
    qln = ql * lax.rsqrt(jnp.mean(ql * ql, axis=-1, keepdims=True) + EPS) * gql_ref[...]
    q = jnp.dot(qln.astype(BF16), wqu_ref[...], preferred_element_type=F32)
    qn = q * lax.rsqrt(_group_mean(q * q, bq_ref[...]) + EPS) * gq_ref[...]
    qall_ref[...] = _rope_slabs(qn, cq_ref[...], sq_ref[...]).astype(BF16)

    kv = p[:, _C_KV:_C_QD]
    ckv = kv * lax.rsqrt(jnp.mean(kv * kv, axis=-1, keepdims=True) + EPS) * gkv_ref[...]
    ckv_ref[...] = ckv
    ckvb = ckv.astype(BF16)
    ckvb_ref[...] = ckvb
    kn = jnp.dot(ckvb, wuk_ref[...], preferred_element_type=F32)
    kn = kn * lax.rsqrt(_group_mean(kn * kn, bq_ref[...]) + EPS) * gk_ref[...]

    m = p[:, _C_MISC:_C_KIDUP]
    lane = lax.broadcasted_iota(I32, m.shape, 1)
    is_kr = lane < MLA_ROPE
    mm = _split_dot(m * m, bm_ref[...])
    mn = jnp.where(is_kr, m * lax.rsqrt(mm + EPS) * gm_ref[...], m)
    mr = _rope_slabs(mn, cm_ref[...], sm_ref[...])
    is_wi = (lane >= _M_WI) & (lane < _M_WI + IDX_HEADS)
    misc = jnp.where(is_wi, mr * (IDX_HEADS ** -0.5), mr)
    misc_ref[...] = misc
    wib_ref[...] = _split_dot(misc, ex_ref[...])
    kr_placed = jnp.where((lane >= MLA_NOPE) & (lane < MLA_NOPE + MLA_ROPE), pltpu.roll(mr, MLA_NOPE, 1), 0.0)
    kall_ref[...] = jnp.concatenate(
        [kn[:, s * LANES:(s + 1) * LANES] + kr_placed for s in range(MLA_HEADS)], axis=1).astype(BF16)

    qd = p[:, _C_QD:_C_KD]
    qd_ref[...] = (qd * lax.rsqrt(_group_mean(qd * qd, b64_ref[...]) + EPS) * gdq_ref[...]).astype(BF16)
    kd = p[:, _C_KD:_C_VD]
    kdn = kd * lax.rsqrt(_group_mean(kd * kd, b64_ref[...]) + EPS) * gdk_ref[...]
    kd_ref[...] = kdn
    kdb_ref[...] = kdn.astype(BF16)
    vd = p[:, _C_VD:_C_QI]
    vd_ref[...] = vd
    vdb_ref[...] = vd.astype(BF16)
    qi_ref[...] = p[:, _C_QI:_C_MISC].astype(BF16)
    kid_ref[...] = p[:, _C_KIDUP:_C_END].astype(BF16)


def _block_mean_matrix(blocks):
    m = np.zeros((LANES, LANES), np.float32)
    for start, size in blocks:
        m[start:start + size, start:start + size] = 1.0 / size
    return jnp.asarray(m, BF16)


def _head_weight_expander():
    m = np.zeros((LANES, IDX_HEADS * LANES), np.float32)
    for h in range(IDX_HEADS):
        m[_M_WI + h, h * LANES:(h + 1) * LANES] = 1.0
    return jnp.asarray(m, BF16)


def _prep_proj_weights(w_in, g_q_lora, w_q_up, g_kv_lora, w_uk, g_mla_qn, g_mla_qr, g_mla_kn, g_mla_kr,
                       g_dsa_q, g_dsa_k, g_attn_norm):
    offs = np.cumsum((0,) + IN_SIZES)
    sec = lambda k: w_in[:, offs[k]:offs[k + 1]]
    zeros = lambda n: jnp.zeros((D_MODEL, n), w_in.dtype)
    misc = jnp.concatenate([sec(2), sec(7), sec(8), zeros(LANES - MLA_ROPE - IDX_DIM - IDX_HEADS)], axis=1)
    w_in_r = jnp.concatenate([sec(0), sec(1), sec(3), sec(4), sec(5), sec(6), misc, sec(7), sec(7)], axis=1)
    wq = w_q_up.reshape(Q_LORA, MLA_HEADS, MLA_NOPE + MLA_ROPE)
    wq = jnp.pad(wq, ((0, 0), (0, 0), (0, LANES - MLA_NOPE - MLA_ROPE))).reshape(Q_LORA, MLA_HEADS * LANES)
    wk = jnp.pad(w_uk, ((0, 0), (0, 0), (0, LANES - MLA_NOPE))).reshape(KV_LORA, MLA_HEADS * LANES)
    pad1 = lambda v, n: jnp.pad(v, (0, n - v.shape[0]))
    gq = jnp.tile(pad1(jnp.concatenate([g_mla_qn, g_mla_qr]), LANES), MLA_HEADS).reshape(1, -1)
    gk = jnp.tile(pad1(g_mla_kn, LANES), MLA_HEADS).reshape(1, -1)
    gm = jnp.concatenate([g_mla_kr, jnp.ones((LANES - MLA_ROPE,), F32)]).reshape(1, -1)
    return dict(
        win=w_in_r.astype(BF16), wqu=wq.astype(BF16), wuk=wk.astype(BF16),
        ga=g_attn_norm.reshape(1, -1), gql=g_q_lora.reshape(1, -1), gkv=g_kv_lora.reshape(1, -1),
        gq=gq, gk=gk, gm=gm,
        gdq=jnp.tile(g_dsa_q, DSA_HEADS).reshape(1, -1), gdk=jnp.tile(g_dsa_k, DSA_HEADS).reshape(1, -1),
        bq=_block_mean_matrix([(0, MLA_NOPE), (MLA_NOPE, MLA_ROPE)]),
        b64=_block_mean_matrix([(0, DSA_HEAD_DIM), (DSA_HEAD_DIM, DSA_HEAD_DIM)]),
        bm=_block_mean_matrix([(0, MLA_ROPE)]),
        ex=_head_weight_expander(),
    )


def _rope_tables(pos):
    half = MLA_ROPE // 2
    inv = ROPE_BASE ** (-jnp.arange(half, dtype=F32) / half)
    ang = pos.astype(F32)[:, None] * inv
    cos, sin = jnp.cos(ang), jnp.sin(ang)
    cos32 = jnp.concatenate([cos, cos], axis=1)
    sin32 = jnp.concatenate([-sin, sin], axis=1)
    n = pos.shape[0]
    ones = lambda w: jnp.ones((n, w), F32)
    zeros = lambda w: jnp.zeros((n, w), F32)
    cq = jnp.concatenate([ones(MLA_NOPE), cos32, ones(LANES - MLA_NOPE - MLA_ROPE)], axis=1)
    sq = jnp.concatenate([zeros(MLA_NOPE), sin32, zeros(LANES - MLA_NOPE - MLA_ROPE)], axis=1)
    cm = jnp.concatenate([cos32, ones(LANES - MLA_ROPE)], axis=1)
    sm = jnp.concatenate([sin32, zeros(LANES - MLA_ROPE)], axis=1)
    return cq, sq, cm, sm


def _project(x2d, shift, scale, pw, tables, tm, rows_per_mod):
    n = x2d.shape[0]
    nt = n // tm
    cq, sq, cm, sm = tables
    tpos = cq.shape[0] // tm
    const = lambda shape: pl.BlockSpec(shape, lambda i: (0,) * len(shape))
    row = lambda w: pl.BlockSpec((tm, w), lambda i: (i, 0))
    if rows_per_mod:
        per = rows_per_mod // tm
        mod_spec = pl.BlockSpec((None, 1, D_MODEL), lambda i: (i // per, 0, 0))
    else:
        mod_spec = row(D_MODEL)
    tab = pl.BlockSpec((tm, LANES), lambda i: (i % tpos, 0))
    in_specs = [row(D_MODEL), mod_spec, mod_spec, const((1, D_MODEL)), const((D_MODEL, _C_END)),
                const((1, Q_LORA)), const((Q_LORA, MLA_HEADS * LANES)), const((1, KV_LORA)),
                const((KV_LORA, MLA_HEADS * LANES)), const((1, MLA_HEADS * LANES)), const((1, MLA_HEADS * LANES)),
                const((1, LANES)), const((1, DSA_OUT)), const((1, DSA_OUT)), tab, tab, tab, tab,
                const((LANES, LANES)), const((LANES, LANES)), const((LANES, LANES)),
                const((LANES, IDX_HEADS * LANES))]
    widths = [(MLA_HEADS * LANES, BF16), (MLA_HEADS * LANES, BF16), (KV_LORA, F32), (KV_LORA, BF16), (LANES, F32),
              (DSA_OUT, BF16), (DSA_OUT, F32), (DSA_OUT, BF16), (DSA_OUT, F32), (DSA_OUT, BF16),
              (IDX_HEADS * IDX_DIM, BF16), (LANES, BF16), (IDX_HEADS * LANES, F32)]
    outs = pl.pallas_call(
        _proj_kernel,
        grid=(nt,),
        in_specs=in_specs,
        out_specs=[row(w) for w, _ in widths],
        out_shape=[jax.ShapeDtypeStruct((n, w), dt) for w, dt in widths],
        compiler_params=_cparams(("arbitrary",)),
        name="project",
    )(x2d, shift, scale, pw["ga"], pw["win"], pw["gql"], pw["wqu"], pw["gkv"], pw["wuk"],
      pw["gq"], pw["gk"], pw["gm"], pw["gdq"], pw["gdk"], cq, sq, cm, sm, pw["bq"], pw["b64"], pw["bm"],
      pw["ex"])
    names = ("qall", "kall", "ckv", "ckvb", "misc", "qd", "kd", "kdb", "vd", "vdb", "qi", "kid", "wib")
    return dict(zip(names, outs))


_MLA_TQ = 256
_MLA_TK = 512


def _mla_prompt_kernel(q_ref, k_ref, c_ref, wuv_ref, o_ref):
    tq, tk = _MLA_TQ, _MLA_TK
    i = pl.program_id(1)
    q0 = i * tq
    nkb = (q0 + tq + tk - 1) // tk
    row = q0 + lax.broadcasted_iota(I32, (tq, tk), 0)
    col0 = lax.broadcasted_iota(I32, (tq, tk), 1)
    heads = []
    for h in range(MLA_HEADS):
        qh = q_ref[:, h * LANES:(h + 1) * LANES]

        def body(j, carry, qh=qh, h=h):
            m, l, acc = carry
            k0 = pl.multiple_of(j * tk, tk)
            kh = k_ref[pl.ds(k0, tk), h * LANES:(h + 1) * LANES]
            s = _nt_dot(qh, kh) * MLA_SCALE
            s = jnp.where(col0 + k0 <= row, s, NEG_INF)
            m_new = jnp.maximum(m, jnp.max(s, axis=1, keepdims=True))
            alpha = jnp.exp(m - m_new)
            p = jnp.exp(s - m_new)
            l = alpha * l + jnp.sum(p, axis=1, keepdims=True)
            acc = alpha * acc + jnp.dot(p.astype(BF16), c_ref[pl.ds(k0, tk), :], preferred_element_type=F32)
            return m_new, l, acc

        init = (jnp.full((tq, 1), NEG_INF, F32), jnp.zeros((tq, 1), F32), jnp.zeros((tq, KV_LORA), F32))
        m, l, acc = lax.fori_loop(0, nkb, body, init)
        heads.append((acc / l).astype(BF16))
    for pr in range(MLA_HEADS // 2):
        o = (jnp.dot(heads[2 * pr], wuv_ref[2 * pr], preferred_element_type=F32)
             + jnp.dot(heads[2 * pr + 1], wuv_ref[2 * pr + 1], preferred_element_type=F32))
        o_ref[:, pr * LANES:(pr + 1) * LANES] = o.astype(BF16)


def _pad_wuv(w_uv):
    w = jnp.transpose(w_uv, (1, 0, 2))
    even = jnp.pad(w, ((0, 0), (0, 0), (0, LANES - MLA_V)))
    odd = jnp.pad(w, ((0, 0), (0, 0), (LANES - MLA_V, 0)))
    is_odd = (jnp.arange(MLA_HEADS) % 2 == 1)[:, None, None]
    return jnp.where(is_odd, odd, even).astype(BF16)


def _mla_prompt(qall, kall, ckvb, wuv_pad):
    b, t, _ = qall.shape
    tq = _MLA_TQ
    assert t % _MLA_TK == 0
    return pl.pallas_call(
        _mla_prompt_kernel,
        grid=(b, t // tq),
        in_specs=[pl.BlockSpec((None, tq, MLA_HEADS * LANES), lambda bi, i: (bi, i, 0)),
                  pl.BlockSpec((None, t, MLA_HEADS * LANES), lambda bi, i: (bi, 0, 0)),
                  pl.BlockSpec((None, t, KV_LORA), lambda bi, i: (bi, 0, 0)),
                  pl.BlockSpec((MLA_HEADS, KV_LORA, LANES), lambda bi, i: (0, 0, 0))],
        out_specs=pl.BlockSpec((None, tq, MLA_OUT), lambda bi, i: (bi, i, 0)),
        out_shape=jax.ShapeDtypeStruct((b, t, MLA_OUT), BF16),
        compiler_params=_cparams(("arbitrary", "arbitrary")),
        name="mla_prompt",
    )(qall, kall, ckvb, wuv_pad)


_DSA_T = 128
_DSA_CW = 512


def _sortable_key(score):
    bits = pltpu.bitcast(score, I32)
    key = jnp.where(bits < 0, bits ^ jnp.int32(0x7FFFFFFF), bits)
    return jnp.where(score == 0.0, 0, key)


def _bias_tiles(rb_ref, tz_ref):
    t = _DSA_T
    r = lax.broadcasted_iota(I32, (t, t), 0)
    c = lax.broadcasted_iota(I32, (t, t), 1)
    exact = REL_BUCKETS // 2
    for which in range(2):
        d = jnp.maximum(r - c + t * which, 0)
        logd = jnp.log(jnp.maximum(d, 1).astype(F32) / exact) / math.log(REL_MAX_DIST / exact)
        far = jnp.minimum(exact + (logd * (REL_BUCKETS - exact)).astype(I32), REL_BUCKETS - 1)
        bucket = jnp.where(d < exact, d, far)
        for h in range(DSA_HEADS):
            tile = jnp.zeros((t, t), F32)
            for bk in range(REL_BUCKETS):
                tile = jnp.where(bucket == bk, rb_ref[bk, h], tile)
            tz_ref[h, which] = tile


def _select_mask(keys_ref, mask_ref, tri_ref, nch, n_sel, rows):
    cw = _DSA_CW

    def count(pred):
        def body(j, acc):
            kb = keys_ref[:, pl.ds(pl.multiple_of(j * cw, cw), cw)]
            return acc + _fold_lanes(jnp.where(pred(kb), 1.0, 0.0), jnp.add)
        acc = lax.fori_loop(0, nch, body, jnp.zeros((rows, LANES), F32))
        return jnp.sum(acc, axis=1, keepdims=True)

    kf = jnp.float32(n_sel)
    zero = jnp.zeros((rows, 1), I32)
    thr = jnp.where(count(lambda kb: kb >= zero) >= kf, zero, jnp.full((rows, 1), INT_MIN, I32))

    def bit_body(bi, thr):
        cand = thr + lax.shift_left(jnp.int32(1), 30 - bi)
        return jnp.where(count(lambda kb: kb >= cand) >= kf, cand, thr)

    thr = lax.fori_loop(0, 31, bit_body, thr)
    need = kf - count(lambda kb: kb > thr)

    def mask_body(j, carry):
        k0 = pl.multiple_of(j * cw, cw)
        kb = keys_ref[:, pl.ds(k0, cw)]
        eq = kb == thr
        eqf = jnp.where(eq, 1.0, 0.0)
        before = jnp.dot(eqf.astype(BF16), tri_ref[...], preferred_element_type=F32) + carry
        keep = ((kb > thr) | (eq & (before < need))) & (kb > INT_MIN)
        mask_ref[:, pl.ds(k0, cw)] = jnp.where(keep, 0.0, NEG_INF)
        return carry + jnp.sum(eqf, axis=1, keepdims=True)

    lax.fori_loop(0, nch, mask_body, jnp.zeros((rows, 1), F32))


def _dsa_prompt_kernel(rb_ref, qi_ref, wib_ref, kid_ref, qd_ref, kd_ref, vd_ref, tri_ref, o_ref,
                       keys_ref, mask_ref, tz_ref, s_ref, *, n_sel):
    t, cw = _DSA_T, _DSA_CW
    sub_n = cw // t
    bi = pl.program_id(0)
    i = pl.program_id(1)
    nch = i // sub_n + 1

    @pl.when((bi == 0) & (i == 0))
    def _():
        _bias_tiles(rb_ref, tz_ref)

    row = i * t + lax.broadcasted_iota(I32, (t, cw), 0)
    col0 = lax.broadcasted_iota(I32, (t, cw), 1)
    lane = lax.broadcasted_iota(I32, (t, LANES), 1)
    low = lane < DSA_HEAD_DIM

    def head_halves(ref, pr):
        qs = ref[:, pr * LANES:(pr + 1) * LANES]
        zero = jnp.zeros_like(qs)
        return jnp.where(low, qs, zero), jnp.where(low, zero, qs)

    qim = [q for pr in range(IDX_HEADS // 2) for q in head_halves(qi_ref, pr)]

    def idx_body(c, _):
        k0 = pl.multiple_of(c * cw, cw)
        kk = kid_ref[pl.ds(k0, cw), :]
        acc = jnp.zeros((t, cw), F32)
        for h in range(IDX_HEADS):
            r = jnp.maximum(_nt_dot(qim[h], kk) * (IDX_DIM ** -0.5), 0.0)
            w = wib_ref[:, h * LANES:(h + 1) * LANES]
            acc = acc + jnp.concatenate([w] * sub_n, axis=1) * r
        keys_ref[:, pl.ds(k0, cw)] = jnp.where(col0 + k0 <= row, _sortable_key(acc), INT_MIN)
        return 0

    lax.fori_loop(0, nch, idx_body, 0)
    _select_mask(keys_ref, mask_ref, tri_ref, nch, n_sel, t)

    for pr in range(DSA_HEADS // 2):
        qms = head_halves(qd_ref, pr)
        fars = [rb_ref[REL_BUCKETS - 1, 2 * pr + half] for half in range(2)]

        def pass_a(c, mrun, pr=pr, qms=qms, fars=fars):
            k0 = pl.multiple_of(c * cw, cw)
            kb = kd_ref[pl.ds(k0, cw), pr * LANES:(pr + 1) * LANES]
            mk = mask_ref[:, pl.ds(k0, cw)]
            out = []
            for half in range(2):
                h = 2 * pr + half
                s = _nt_dot(qms[half], kb) * DSA_SCALE
                parts = []
                for sub in range(sub_n):
                    blk = c * sub_n + sub
                    bias = jnp.where(blk == i, tz_ref[h, 0], jnp.where(blk == i - 1, tz_ref[h, 1], fars[half]))
                    parts.append(s[:, sub * t:(sub + 1) * t] + bias)
                s = jnp.concatenate(parts, axis=1) + mk
                s_ref[half, :, pl.ds(k0, cw)] = s
                out.append(jnp.maximum(mrun[half], _fold_lanes(s, jnp.maximum)))
            return tuple(out)

        ninf = jnp.full((t, LANES), NEG_INF, F32)
        mrun = lax.fori_loop(0, nch, pass_a, (ninf, ninf))
        ms = [jnp.max(m, axis=1, keepdims=True) for m in mrun]

        def pass_b(c, carry, pr=pr, ms=ms):
            k0 = pl.multiple_of(c * cw, cw)
            vb = vd_ref[pl.ds(k0, cw), pr * LANES:(pr + 1) * LANES]
            out = []
            for half in range(2):
                l, acc = carry[half]
                p = jnp.exp(s_ref[half, :, pl.ds(k0, cw)] - ms[half])
                out.append((l + _fold_lanes(p, jnp.add),
                            acc + jnp.dot(p.astype(BF16), vb, preferred_element_type=F32)))
            return tuple(out)

        zero = jnp.zeros((t, LANES), F32)
        (l0, a0), (l1, a1) = lax.fori_loop(0, nch, pass_b, ((zero, zero), (zero, zero)))
        o0 = a0 / jnp.sum(l0, axis=1, keepdims=True)
        o1 = a1 / jnp.sum(l1, axis=1, keepdims=True)
        o_ref[:, pr * LANES:(pr + 1) * LANES] = jnp.where(low, o0, o1).astype(BF16)


def _strict_upper(n):
    return jnp.asarray(np.triu(np.ones((n, n), np.float32), 1), BF16)


def _dsa_prompt(rel_bias, qi, wib, kid, qd, kdb, vdb):
    b, t, _ = qi.shape
    tq, cw = _DSA_T, _DSA_CW
    assert t % cw == 0
    n_sel = min(IDX_TOPK_MAX, t // 4)
    blk = lambda w: pl.BlockSpec((None, tq, w), lambda bi, i: (bi, i, 0))
    full = lambda w: pl.BlockSpec((None, t, w), lambda bi, i: (bi, 0, 0))
    return pl.pallas_call(
        functools.partial(_dsa_prompt_kernel, n_sel=n_sel),
        grid=(b, t // tq),
        in_specs=[pl.BlockSpec(memory_space=pltpu.SMEM),
                  blk(IDX_HEADS * IDX_DIM), blk(IDX_HEADS * LANES), full(LANES), blk(DSA_OUT), full(DSA_OUT),
                  full(DSA_OUT), pl.BlockSpec((cw, cw), lambda bi, i: (0, 0))],
        out_specs=blk(DSA_OUT),
        out_shape=jax.ShapeDtypeStruct((b, t, DSA_OUT), BF16),
        scratch_shapes=[pltpu.VMEM((tq, t), I32), pltpu.VMEM((tq, t), F32),
                        pltpu.VMEM((DSA_HEADS, 2, tq, tq), F32), pltpu.VMEM((2, tq, t), F32)],
        compiler_params=_cparams(("arbitrary", "arbitrary")),
        name="dsa_prompt",
    )(rel_bias, qi, wib, kid, qd, kdb, vdb, _strict_upper(cw))


_EXPERTS_PER_GROUP = N_EXPERTS // N_GROUPS


def _first_index_of_max(v, idx, axis, sentinel):
    mx = jnp.max(v, axis=axis, keepdims=True)
    first = jnp.min(jnp.where(v == mx, idx, sentinel), axis=axis, keepdims=True)
    return mx, first


def _route(logits_t, bias_col):
    n_tok = logits_t.shape[1]
    scores = jax.nn.sigmoid(logits_t)
    biased = scores + bias_col
    b3 = biased.reshape(N_GROUPS, _EXPERTS_PER_GROUP, n_tok)
    j3 = lax.broadcasted_iota(I32, b3.shape, 1)
    m1, f1 = _first_index_of_max(b3, j3, 1, _EXPERTS_PER_GROUP)
    m2 = jnp.max(jnp.where(j3 == f1, NEG_INF, b3), axis=1, keepdims=True)
    gs = (m1 + m2).reshape(N_GROUPS, n_tok)
    gi = lax.broadcasted_iota(I32, gs.shape, 0)
    gsel = jnp.zeros(gs.shape, jnp.bool_)
    for _ in range(TOPK_GROUPS):
        _, first = _first_index_of_max(gs, gi, 0, N_GROUPS)
        hit = gi == first
        gsel = gsel | hit
        gs = jnp.where(hit, NEG_INF, gs)
    gsel3 = jnp.broadcast_to(gsel.reshape(N_GROUPS, 1, n_tok), b3.shape)
    masked = jnp.where(gsel3, b3, NEG_INF).reshape(N_EXPERTS, n_tok)
    ei = lax.broadcasted_iota(I32, masked.shape, 0)
    sel = jnp.zeros(masked.shape, jnp.bool_)
    for _ in range(TOP_K):
        _, first = _first_index_of_max(masked, ei, 0, N_EXPERTS)
        hit = ei == first
        sel = sel | hit
        masked = jnp.where(hit, NEG_INF, masked)
    w = jnp.where(sel, scores, 0.0)
    gate = w / jnp.sum(w, axis=0, keepdims=True) * ROUTED_SCALE
    return sel, gate


def _swiglu_bf(x_bf, wgu_ref, wdown_ref, d_hidden):
    gu = jnp.dot(x_bf, wgu_ref[...], preferred_element_type=F32)
    g, u = gu[:, :d_hidden], gu[:, d_hidden:]
    act = (g * jax.nn.sigmoid(g)) * u
    return jnp.dot(act.astype(BF16), wdown_ref[...], preferred_element_type=F32)


def _moe_kernel(x_ref, oa_ref, ob_ref, ga_ref, sf_ref, cf_ref, gf_ref, gn_ref, wo_ref, wr_ref, br_ref,
                wsg_ref, wsd_ref, tri_ref, weg_ref, wed_ref, y_ref, xt_ref, acc_ref, gate_ref, rank_ref,
                *, cap):
    tm = x_ref.shape[0]
    e = pl.program_id(1)

    @pl.when(e == 0)
    def _():
        half = oa_ref.shape[1]
        attn = (jnp.dot(oa_ref[...], wo_ref[:half, :], preferred_element_type=F32)
                + jnp.dot(ob_ref[...], wo_ref[half:, :], preferred_element_type=F32))
        x1 = x_ref[...] + ga_ref[...] * attn
        y_ref[...] = x1
        hn = x1 * lax.rsqrt(jnp.mean(x1 * x1, axis=-1, keepdims=True) + EPS) * gn_ref[...]
        xt = (hn * (1.0 + cf_ref[...]) + sf_ref[...]).astype(BF16)
        xt_ref[...] = xt
        sel, gate = _route(_nt_dot(wr_ref[...], xt), br_ref[...])
        ind = jnp.where(sel, 1.0, 0.0)
        before = jnp.dot(ind.astype(BF16), tri_ref[...], preferred_element_type=F32)
        rank_ref[...] = jnp.where(sel, before, -1.0)
        gate_ref[...] = jnp.where(sel, gate, 0.0)
        acc_ref[...] = jnp.zeros_like(acc_ref)

    rank_row = rank_ref[pl.ds(e, 1), :]
    gate_row = gate_ref[pl.ds(e, 1), :]
    count = (jnp.max(rank_row) + 1.0).astype(I32)
    n_chunks = (count + cap - 1) // cap
    slot = lax.broadcasted_iota(I32, (cap, tm), 0).astype(F32)

    def chunk(c, _):
        pick = slot + (c * cap).astype(F32) == rank_row
        pick_f = jnp.where(pick, 1.0, 0.0)
        pick_b = pick_f.astype(BF16)
        xe = jnp.dot(pick_b, xt_ref[...], preferred_element_type=F32).astype(BF16)
        ye = _swiglu_bf(xe, weg_ref, wed_ref, D_EXPERT)
        ge = jnp.sum(pick_f * gate_row, axis=1, keepdims=True)
        ys = (ye * ge).astype(BF16)
        acc_ref[...] += _tn_dot(pick_b, ys)
        return 0

    lax.fori_loop(0, n_chunks, chunk, 0)

    @pl.when(e == N_EXPERTS - 1)
    def _():
        shared = _swiglu_bf(xt_ref[...], wsg_ref, wsd_ref, D_SHARED)
        y_ref[...] = y_ref[...] + gf_ref[...] * (acc_ref[...] + shared)


def _moe(x2d, oa, ob, mods, g_ffn, wo_bf, wr_t_bf, b_router, wsg_bf, wsd_bf, weg_bf, wed_bf, tm, rows_per_mod):
    n = x2d.shape[0]
    assert n % tm == 0 and tm % LANES == 0
    nt = n // tm
    cap = LANES
    const = lambda shape: pl.BlockSpec(shape, lambda t, e: (0,) * len(shape))
    row = lambda w: pl.BlockSpec((tm, w), lambda t, e: (t, 0))
    if rows_per_mod:
        per = rows_per_mod // tm
        mod_spec = pl.BlockSpec((None, 1, D_MODEL), lambda t, e: (t // per, 0, 0))
    else:
        mod_spec = row(D_MODEL)
    half = oa.shape[1]
    return pl.pallas_call(
        functools.partial(_moe_kernel, cap=cap),
        grid=(nt, N_EXPERTS),
        in_specs=[row(D_MODEL), row(half), row(half), mod_spec, mod_spec, mod_spec, mod_spec,
                  const((1, D_MODEL)), const((2 * half, D_MODEL)), const((N_EXPERTS, D_MODEL)),
                  const((N_EXPERTS, 1)), const((D_MODEL, 2 * D_SHARED)), const((D_SHARED, D_MODEL)),
                  const((tm, tm)),
                  pl.BlockSpec((None, D_MODEL, 2 * D_EXPERT), lambda t, e: (e, 0, 0)),
                  pl.BlockSpec((None, D_EXPERT, D_MODEL), lambda t, e: (e, 0, 0))],
        out_specs=row(D_MODEL),
        out_shape=jax.ShapeDtypeStruct((n, D_MODEL), F32),
        scratch_shapes=[pltpu.VMEM((tm, D_MODEL), BF16), pltpu.VMEM((tm, D_MODEL), F32),
                        pltpu.VMEM((N_EXPERTS, tm), F32), pltpu.VMEM((N_EXPERTS, tm), F32)],
        compiler_params=_cparams(("arbitrary", "arbitrary")),
        name="moe",
    )(x2d, oa, ob, *mods, g_ffn.reshape(1, -1), wo_bf, wr_t_bf, b_router.reshape(-1, 1), wsg_bf, wsd_bf,
      _strict_upper(tm), weg_bf, wed_bf)


def _col_blocks(q):
    nh = q.shape[0]
    r = lax.broadcasted_iota(I32, (nh, LANES), 0)
    c = lax.broadcasted_iota(I32, (nh, LANES), 1)
    blocks = [_tn_dot(q, jnp.where((r == h) & (c == h), 1.0, 0.0).astype(q.dtype)) for h in range(nh)]
    return jnp.concatenate(blocks, axis=0)


def _rows_to_cols(row):
    r8 = jnp.broadcast_to(row, (8, row.shape[1]))
    e0 = jnp.where(lax.broadcasted_iota(I32, (8, LANES), 0) == 0, 1.0, 0.0).astype(row.dtype)
    return _tn_dot(r8, e0)


def _head_diag(acc, width):
    r = lax.broadcasted_iota(I32, acc.shape, 0)
    c = lax.broadcasted_iota(I32, acc.shape, 1)
    return jnp.sum(jnp.where(r == c // width, acc, 0.0), axis=0, keepdims=True)


class _PageStream:
    def __init__(self, hbm_ref, buf_ref, sem_ref, pt_ref, seq, pages_per_chunk):
        self.hbm, self.buf, self.sem, self.pt, self.seq, self.ppc = hbm_ref, buf_ref, sem_ref, pt_ref, seq, pages_per_chunk

    def _copy(self, page, slot, r):
        return pltpu.make_async_copy(self.hbm.at[page], self.buf.at[slot, r], self.sem.at[slot])

    def start(self, chunk, slot):
        for r in range(self.ppc):
            self._copy(self.pt[self.seq, chunk * self.ppc + r], slot, r).start()

    def wait(self, slot):
        for r in range(self.ppc):
            self._copy(0, slot, r).wait()


_STREAM_DEPTH = 4


def _stream_loop(streams, n_chunks, body):
    depth = _STREAM_DEPTH
    for d in range(min(depth - 1, n_chunks)):
        for st in streams:
            st.start(d, d)

    def step(c, _):
        slot = c % depth
        nxt = c + depth - 1

        @pl.when(nxt < n_chunks)
        def _():
            for st in streams:
                st.start(nxt, nxt % depth)

        for st in streams:
            st.wait(slot)
        body(c, slot)
        return 0

    lax.fori_loop(0, n_chunks, step, 0)


def _softmax_stats(sc_ref, n_rows, blk):
    nb = n_rows // blk
    tail = n_rows - nb * blk

    def mx_body(j, m):
        return jnp.maximum(m, jnp.max(sc_ref[pl.ds(pl.multiple_of(j * blk, blk), blk), :], axis=0, keepdims=True))

    m = lax.fori_loop(0, nb, mx_body, jnp.full((1, LANES), NEG_INF, F32))
    if tail:
        m = jnp.maximum(m, jnp.max(sc_ref[pl.ds(nb * blk, tail), :], axis=0, keepdims=True))

    def sum_body(j, l):
        return l + jnp.sum(jnp.exp(sc_ref[pl.ds(pl.multiple_of(j * blk, blk), blk), :] - m), axis=0, keepdims=True)

    l = lax.fori_loop(0, nb, sum_body, jnp.zeros((1, LANES), F32))
    if tail:
        l = l + jnp.sum(jnp.exp(sc_ref[pl.ds(nb * blk, tail), :] - m), axis=0, keepdims=True)
    return m, l


_MLA_S_PAGES = 4


def _mla_sample_kernel(pt_ref, q_ref, knew_ref, cnew_ref, wuk_ref, gk_ref, wuv_ref, lat_hbm, kr_hbm, o_ref,
                       latbuf, krbuf, sems, sc_ref, latbf_ref, *, n_pages):
    s_id = pl.program_id(0)
    ppc = _MLA_S_PAGES
    ck = ppc * PAGE_SIZE
    n_chunks = n_pages // ppc
    past = n_pages * PAGE_SIZE
    row8 = lax.broadcasted_iota(I32, (8, LANES), 0)

    qblk = _col_blocks(q_ref[...]).astype(BF16)
    qr = qblk[MLA_NOPE:MLA_NOPE + MLA_ROPE, :].astype(F32)
    for h in range(1, MLA_HEADS):
        qr = qr + qblk[h * LANES + MLA_NOPE:h * LANES + MLA_NOPE + MLA_ROPE, :].astype(F32)
    qr = qr.astype(BF16)
    lat_stream = _PageStream(lat_hbm, latbuf, sems.at[0], pt_ref, s_id, ppc)
    kr_stream = _PageStream(kr_hbm, krbuf, sems.at[1], pt_ref, s_id, ppc)

    def score_chunk(c, slot):
        lat = latbuf[slot].reshape(ck, KV_LORA).astype(BF16)
        latbf_ref[pl.ds(pl.multiple_of(c * ck, ck), ck), :] = lat
        kraw = jnp.dot(lat, wuk_ref[...], preferred_element_type=F32)
        slabs = []
        for h in range(MLA_HEADS):
            x = kraw[:, h * LANES:(h + 1) * LANES]
            ms = jnp.sum(x * x, axis=1, keepdims=True) * (1.0 / MLA_NOPE)
            slabs.append((x * lax.rsqrt(ms + EPS) * gk_ref[:, h * LANES:(h + 1) * LANES]).astype(BF16))
        kn = jnp.concatenate(slabs, axis=1)
        s_rope = jnp.concatenate([_tn_dot(krbuf[slot, r].astype(BF16), qr) for r in range(ppc)], axis=0)
        s = (jnp.dot(kn, qblk, preferred_element_type=F32) + s_rope) * MLA_SCALE
        sc_ref[pl.ds(pl.multiple_of(c * ck, ck), ck), :] = s

    _stream_loop((lat_stream, kr_stream), n_chunks, score_chunk)
    s_new = jnp.dot(jnp.broadcast_to(knew_ref[...], (8, knew_ref.shape[1])), qblk,
                    preferred_element_type=F32) * MLA_SCALE
    sc_ref[pl.ds(past, 8), :] = jnp.where(row8 == 0, s_new, NEG_INF)
    m, l = _softmax_stats(sc_ref, past + 8, ck)

    def pv_chunk(c, acc):
        k0 = pl.multiple_of(c * ck, ck)
        p = jnp.exp(sc_ref[pl.ds(k0, ck), :] - m) / l
        return acc + _tn_dot(p.astype(BF16), latbf_ref[pl.ds(k0, ck), :])

    acc = lax.fori_loop(0, n_chunks, pv_chunk, jnp.zeros((LANES, KV_LORA), F32))
    p_new = jnp.exp(sc_ref[pl.ds(past, 8), :] - m) / l
    acc = acc + _tn_dot(p_new.astype(BF16), jnp.broadcast_to(cnew_ref[...], (8, KV_LORA)))
    out = jnp.dot(acc.astype(BF16), wuv_ref[...], preferred_element_type=F32)
    o_ref[...] = _head_diag(out, MLA_V).astype(BF16)


def _mla_sample(page_table, q8, knew, cnew, wuk_pad, gk, wuv_flat, cache_lat, cache_kr):
    db, n_pages = page_table.shape
    ck = _MLA_S_PAGES * PAGE_SIZE
    past = n_pages * PAGE_SIZE
    per_seq = lambda shape: pl.BlockSpec((None,) + shape, lambda s, pt: (s,) + (0,) * len(shape))
    const = lambda shape: pl.BlockSpec(shape, lambda s, pt: (0,) * len(shape))
    grid_spec = pltpu.PrefetchScalarGridSpec(
        num_scalar_prefetch=1, grid=(db,),
        in_specs=[per_seq((MLA_HEADS, LANES)), per_seq((1, MLA_HEADS * LANES)), per_seq((1, KV_LORA)),
                  const((KV_LORA, MLA_HEADS * LANES)), const((1, MLA_HEADS * LANES)), const((KV_LORA, MLA_OUT)),
                  pl.BlockSpec(memory_space=pl.ANY), pl.BlockSpec(memory_space=pl.ANY)],
        out_specs=per_seq((1, MLA_OUT)),
        scratch_shapes=[pltpu.VMEM((_STREAM_DEPTH, _MLA_S_PAGES, PAGE_SIZE, KV_LORA), F32),
                        pltpu.VMEM((_STREAM_DEPTH, _MLA_S_PAGES, MLA_ROPE, PAGE_SIZE), F32),
                        pltpu.SemaphoreType.DMA((2, _STREAM_DEPTH)), pltpu.VMEM((past + 8, LANES), F32),
                        pltpu.VMEM((past, KV_LORA), BF16)])
    return pl.pallas_call(
        functools.partial(_mla_sample_kernel, n_pages=n_pages),
        grid_spec=grid_spec,
        out_shape=jax.ShapeDtypeStruct((db, 1, MLA_OUT), BF16),
        compiler_params=_cparams(("arbitrary",)),
        name="mla_sample",
    )(page_table, q8, knew, cnew, wuk_pad, gk, wuv_flat, cache_lat, cache_kr)


_DSA_S_IDX_PAGES = 8


def _select_flat(keys, tri_u, tri_l, n_sel):
    def count(pred):
        c = jnp.sum(jnp.where(pred, 1.0, 0.0), axis=1, keepdims=True)
        return jnp.sum(c, axis=0, keepdims=True)

    kf = jnp.float32(n_sel)
    zero = jnp.zeros((1, 1), I32)
    thr = jnp.where(count(keys >= zero) >= kf, zero, jnp.full((1, 1), INT_MIN, I32))

    def bit_body(bi, thr):
        cand = thr + lax.shift_left(jnp.int32(1), 30 - bi)
        return jnp.where(count(keys >= cand) >= kf, cand, thr)

    thr = lax.fori_loop(0, 31, bit_body, thr)
    need = kf - count(keys > thr)
    eq = keys == thr
    eqf = jnp.where(eq, 1.0, 0.0)
    within = jnp.dot(eqf.astype(BF16), tri_u, preferred_element_type=F32)
    rowcount = jnp.broadcast_to(jnp.sum(eqf, axis=1, keepdims=True), eqf.shape)
    carry = jnp.dot(tri_l, rowcount.astype(BF16), preferred_element_type=F32)
    keep = ((keys > thr) | (eq & (within + carry < need))) & (keys > INT_MIN)
    return jnp.where(keep, 1.0, 0.0)


_DSA_S_KV_PAGES = 2


def _round_bf16(x):
    return x.astype(BF16).astype(F32)


def _dsa_sample_kernel(pt_ref, rbt_ref, qi_ref, wi_ref, kinew_ref, qd_ref, kdnew_ref, vdnew_ref,
                       triu_ref, tril_ref, idx_hbm, k_hbm, v_hbm, o_ref,
                       idxbuf, kvbuf, sems, keys_ref, mask_ref, s_ref, qcol_ref, acc_ref, *, n_pages, n_sel):
    s_id = pl.program_id(0)
    lane1 = lax.broadcasted_iota(I32, (1, LANES), 1)
    qi = qi_ref[...]
    wi = wi_ref[...]

    keys_ref[...] = jnp.full(keys_ref.shape, INT_MIN, I32)
    idx_stream = _PageStream(idx_hbm, idxbuf, sems.at[0], pt_ref, s_id, _DSA_S_IDX_PAGES)

    def idx_chunk(c, slot):
        for r in range(_DSA_S_IDX_PAGES):
            kk = idxbuf[slot, r].astype(BF16)
            rr = jnp.maximum(jnp.dot(qi, kk, preferred_element_type=F32) * (IDX_DIM ** -0.5), 0.0)
            sc = jnp.sum(wi * rr, axis=0, keepdims=True)
            keys_ref[pl.ds(c * _DSA_S_IDX_PAGES + r, 1), :] = _sortable_key(sc)

    _stream_loop((idx_stream,), n_pages // _DSA_S_IDX_PAGES, idx_chunk)
    r_new = jnp.maximum(jnp.sum(qi.astype(F32) * kinew_ref[...].astype(F32), axis=1, keepdims=True)
                        * (IDX_DIM ** -0.5), 0.0)
    sc_new = jnp.sum(wi * r_new, axis=0, keepdims=True)
    keys_ref[pl.ds(n_pages, 1), :] = jnp.where(lane1 == 0, _sortable_key(jnp.broadcast_to(sc_new, (1, LANES))),
                                               INT_MIN)

    mask_ref[...] = _select_flat(keys_ref[...], triu_ref[...], tril_ref[...], n_sel)

    qd = qd_ref[...]
    row8 = lax.broadcasted_iota(I32, (8, LANES), 0)
    e0 = jnp.where(row8 == 0, 1.0, 0.0).astype(qd.dtype)
    for h in range(DSA_HEADS):
        qcol_ref[h] = _tn_dot(jnp.broadcast_to(qd[h:h + 1, :], (8, DSA_HEAD_DIM)), e0)

    far_col = rbt_ref[:, REL_BUCKETS - 1:REL_BUCKETS]
    exact = REL_BUCKETS // 2
    d_row = PAGE_SIZE - lane1
    logd = jnp.log(jnp.maximum(d_row, 1).astype(F32) / exact) / math.log(REL_MAX_DIST / exact)
    far_b = jnp.minimum(exact + (logd * (REL_BUCKETS - exact)).astype(I32), REL_BUCKETS - 1)
    bucket = jnp.where(d_row < exact, d_row, far_b)
    near = jnp.zeros((DSA_HEADS, LANES), F32)
    for bk in range(REL_BUCKETS):
        near = jnp.where(bucket == bk, rbt_ref[:, bk:bk + 1], near)

    k_stream = _PageStream(k_hbm, kvbuf, sems.at[1], pt_ref, s_id, _DSA_S_KV_PAGES)
    v_stream = _PageStream(v_hbm, kvbuf, sems.at[1], pt_ref, s_id, _DSA_S_KV_PAGES)

    def k_chunk(c, slot):
        for r in range(_DSA_S_KV_PAGES):
            page = c * _DSA_S_KV_PAGES + r
            for h in range(DSA_HEADS):
                kt = _round_bf16(kvbuf[slot, r, h])
                s_ref[page, h:h + 1, :] = jnp.sum(kt * qcol_ref[h], axis=0, keepdims=True)
            bias = jnp.where(page == n_pages - 1, near, far_col)
            keep = mask_ref[pl.ds(page, 1), :] > 0.5
            s_ref[page] = jnp.where(keep, s_ref[page] * DSA_SCALE + bias, NEG_INF)

    _stream_loop((k_stream,), n_pages // _DSA_S_KV_PAGES, k_chunk)
    s_new = (jnp.sum(qd.astype(F32) * _round_bf16(kdnew_ref[...]), axis=1, keepdims=True) * DSA_SCALE
             + rbt_ref[:, 0:1])
    keep_new = mask_ref[pl.ds(n_pages, 1), :][:, 0:1] > 0.5
    s_new = jnp.where(keep_new, s_new, NEG_INF)

    blk = 8
    def mx_body(j, m):
        return jnp.maximum(m, jnp.max(s_ref[pl.ds(pl.multiple_of(j * blk, blk), blk)], axis=0))

    m_t = lax.fori_loop(0, n_pages // blk, mx_body, jnp.full((DSA_HEADS, LANES), NEG_INF, F32))
    m = jnp.maximum(jnp.max(m_t, axis=1, keepdims=True), s_new)

    def sum_body(j, l):
        return l + jnp.sum(jnp.exp(s_ref[pl.ds(pl.multiple_of(j * blk, blk), blk)] - m), axis=0)

    l_t = lax.fori_loop(0, n_pages // blk, sum_body, jnp.zeros((DSA_HEADS, LANES), F32))
    l = jnp.sum(l_t, axis=1, keepdims=True) + jnp.exp(s_new - m)

    acc_ref[...] = jnp.zeros_like(acc_ref)

    def v_chunk(c, slot):
        for r in range(_DSA_S_KV_PAGES):
            page = c * _DSA_S_KV_PAGES + r
            p = _round_bf16(jnp.exp(s_ref[page] - m) / l)
            for h in range(DSA_HEADS):
                acc_ref[h] += p[h:h + 1, :] * _round_bf16(kvbuf[slot, r, h])

    _stream_loop((v_stream,), n_pages // _DSA_S_KV_PAGES, v_chunk)
    ones8 = jnp.ones((8, LANES), BF16)
    rowh = lax.broadcasted_iota(I32, (DSA_HEADS, DSA_HEAD_DIM), 0)
    out = jnp.zeros((DSA_HEADS, DSA_HEAD_DIM), F32)
    for h in range(DSA_HEADS):
        resid = acc_ref[h]
        tot = jnp.zeros((8, DSA_HEAD_DIM), F32)
        for _ in range(3):
            piece = resid.astype(BF16)
            tot = tot + _nt_dot(ones8, piece)
            resid = resid - piece.astype(F32)
        out = jnp.where(rowh == h, tot, out)
    p_new = _round_bf16(jnp.exp(s_new - m) / l)
    o_ref[...] = (out + p_new * _round_bf16(vdnew_ref[...])).astype(BF16)


def _dsa_sample(page_table, rel_bias, qi8, wi8, kinew, qd8, kdnew, vdnew, cache_idx_t, cache_k_t, cache_v_t):
    db, n_pages = page_table.shape
    n_sel = min(IDX_TOPK_MAX, (n_pages * PAGE_SIZE + 1) // 4)
    assert n_pages % 8 == 0
    rows = -(-(n_pages + 1) // LANES) * LANES
    tri_l = jnp.asarray(np.tril(np.ones((rows, rows), np.float32), -1), BF16)
    per_seq = lambda shape: pl.BlockSpec((None,) + shape, lambda s, pt: (s,) + (0,) * len(shape))
    const = lambda shape: pl.BlockSpec(shape, lambda s, pt: (0,) * len(shape))
    any_spec = pl.BlockSpec(memory_space=pl.ANY)
    head_tile = (DSA_HEADS, DSA_HEAD_DIM)
    page_tile = (DSA_HEADS, DSA_HEAD_DIM, PAGE_SIZE)
    grid_spec = pltpu.PrefetchScalarGridSpec(
        num_scalar_prefetch=1, grid=(db,),
        in_specs=[const((DSA_HEADS, REL_BUCKETS)),
                  per_seq((IDX_HEADS, IDX_DIM)), per_seq((IDX_HEADS, 1)), per_seq((1, IDX_DIM)),
                  per_seq(head_tile), per_seq(head_tile), per_seq(head_tile),
                  const((LANES, LANES)), const((rows, rows)), any_spec, any_spec, any_spec],
        out_specs=per_seq(head_tile),
        scratch_shapes=[pltpu.VMEM((_STREAM_DEPTH, _DSA_S_IDX_PAGES, IDX_DIM, PAGE_SIZE), F32),
                        pltpu.VMEM((_STREAM_DEPTH, _DSA_S_KV_PAGES) + page_tile, F32),
                        pltpu.SemaphoreType.DMA((2, _STREAM_DEPTH)),
                        pltpu.VMEM((rows, LANES), I32), pltpu.VMEM((rows, LANES), F32),
                        pltpu.VMEM((n_pages, DSA_HEADS, LANES), F32),
                        pltpu.VMEM(page_tile, F32), pltpu.VMEM(page_tile, F32)])
    return pl.pallas_call(
        functools.partial(_dsa_sample_kernel, n_pages=n_pages, n_sel=n_sel),
        grid_spec=grid_spec,
        out_shape=jax.ShapeDtypeStruct((db,) + head_tile, BF16),
        compiler_params=_cparams(("arbitrary",)),
        name="dsa_sample",
    )(page_table, rel_bias.T, qi8, wi8, kinew, qd8, kdnew, vdnew, _strict_upper(LANES), tri_l,
      cache_idx_t, cache_k_t, cache_v_t)


def kernel(x_prompt, x_sample, cache_mla_latent, cache_mla_krope, cache_dsa_k, cache_dsa_v, cache_idx_k, page_table, c_prompt, c_sample, rel_bias, w_ada, b_ada, g_attn_norm, w_in, g_q_lora, w_q_up, g_kv_lora, w_kv_up, g_mla_qn, g_mla_qr, g_mla_kn, g_mla_kr, g_dsa_q, g_dsa_k, w_out, g_ffn_norm, w_router, b_router, w_e_gu, w_e_down, w_s_gu, w_s_down):
    depth = w_ada.shape[0]
    assert depth == 1, "single-layer trunk"
    l = 0
    B, T, D = x_prompt.shape
    DB, TS, _ = x_sample.shape
    assert TS == 1, "one new token per sampled sequence"
    ns = DB * TS
    past = page_table.shape[1] * PAGE_SIZE

    w_kv = w_kv_up[l].reshape(KV_LORA, MLA_HEADS, MLA_NOPE + MLA_V)
    w_uk, w_uv = w_kv[..., :MLA_NOPE], w_kv[..., MLA_NOPE:]

    mod = _adaln(jnp.concatenate([c_prompt, c_sample], axis=0), w_ada[l].astype(BF16), b_ada[l])
    mod_p = [m.reshape(B, 1, D) for m in jnp.split(mod[:B], 6, axis=-1)]
    mod_s = jnp.split(mod[B:], 6, axis=-1)

    pw = _prep_proj_weights(w_in[l], g_q_lora[l], w_q_up[l], g_kv_lora[l], w_uk, g_mla_qn[l], g_mla_qr[l],
                            g_mla_kn[l], g_mla_kr[l], g_dsa_q[l], g_dsa_k[l], g_attn_norm[l])
    pp = _project(x_prompt.reshape(B * T, D), mod_p[0], mod_p[1], pw, _rope_tables(jnp.arange(T)), 256, T)
    ps = _project(x_sample.reshape(ns, D), mod_s[0], mod_s[1], pw,
                  _rope_tables(jnp.tile(past + jnp.arange(TS), DB)), ns, 0)

    r3 = lambda a: a.reshape(B, T, a.shape[-1])
    o_mla_p = _mla_prompt(r3(pp["qall"]), r3(pp["kall"]), r3(pp["ckvb"]), _pad_wuv(w_uv))
    o_dsa_p = _dsa_prompt(rel_bias, r3(pp["qi"]), r3(pp["wib"]), r3(pp["kid"]), r3(pp["qd"]), r3(pp["kdb"]),
                          r3(pp["vdb"]))

    o_mla_s = _mla_sample(page_table, ps["qall"].reshape(ns, MLA_HEADS, LANES), ps["kall"].reshape(ns, 1, -1),
                          ps["ckvb"].reshape(ns, 1, KV_LORA), pw["wuk"], pw["gk"],
                          w_uv.reshape(KV_LORA, MLA_OUT).astype(BF16), cache_mla_latent[l],
                          jnp.transpose(cache_mla_krope[l], (0, 2, 1)))
    heads3 = lambda a: a.reshape(ns, DSA_HEADS, DSA_HEAD_DIM)
    o_dsa_s = _dsa_sample(page_table, rel_bias, ps["qi"].reshape(ns, IDX_HEADS, IDX_DIM),
                          ps["misc"][:, _M_WI:_M_WI + IDX_HEADS].reshape(ns, IDX_HEADS, 1),
                          ps["kid"][:, :IDX_DIM].reshape(ns, 1, IDX_DIM),
                          heads3(ps["qd"]), heads3(ps["kd"]), heads3(ps["vd"]),
                          jnp.transpose(cache_idx_k[l], (0, 2, 1)),
                          jnp.transpose(cache_dsa_k[l], (0, 2, 3, 1)), jnp.transpose(cache_dsa_v[l], (0, 2, 3, 1)))

    moe_w = (g_ffn_norm[l], w_out[l].astype(BF16), w_router[l].T.astype(BF16), b_router[l],
             w_s_gu[l].astype(BF16), w_s_down[l].astype(BF16),
             w_e_gu[l].astype(BF16), w_e_down[l].astype(BF16))
    xp = _moe(x_prompt.reshape(B * T, D), o_mla_p.reshape(B * T, MLA_OUT), o_dsa_p.reshape(B * T, DSA_OUT),
              (mod_p[2], mod_p[3], mod_p[4], mod_p[5]), *moe_w, min(1024, T), T).reshape(B, T, D)
    ns_pad = -(-ns // LANES) * LANES
    pad_rows = lambda a: jnp.pad(a, ((0, ns_pad - ns), (0, 0)))
    xs = _moe(pad_rows(x_sample.reshape(ns, D)), pad_rows(o_mla_s.reshape(ns, MLA_OUT)),
              pad_rows(o_dsa_s.reshape(ns, DSA_OUT)), tuple(pad_rows(mod_s[k]) for k in (2, 3, 4, 5)),
              *moe_w, ns_pad, 0)[:ns].reshape(DB, TS, D)

    def caches(p, nb, nt):
        return (p["ckv"].reshape(1, nb, nt, KV_LORA),
                p["misc"][:, :MLA_ROPE].reshape(1, nb, nt, MLA_ROPE),
                p["kd"].reshape(1, nb, nt, DSA_HEADS, DSA_HEAD_DIM),
                p["vd"].reshape(1, nb, nt, DSA_HEADS, DSA_HEAD_DIM),
                p["misc"][:, _M_KI:_M_KI + IDX_DIM].reshape(1, nb, nt, IDX_DIM))

    return (xp, xs) + caches(pp, B, T) + caches(ps, DB, TS)
```

```python
import functools
import math

import jax
import jax.numpy as jnp
import numpy as np
from jax import lax
from jax.experimental import pallas as pl
from jax.experimental.pallas import tpu as pltpu

F32 = jnp.float32
BF16 = jnp.bfloat16
I32 = jnp.int32

D_MODEL = 1024
PAGE_SIZE = 128
EPS = 1e-6
MLA_HEADS = 8
MLA_NOPE = 64
MLA_ROPE = 32
MLA_V = 64
Q_LORA = 256
KV_LORA = 128
ROPE_BASE = 10000.0
MLA_SCALE = (MLA_NOPE + MLA_ROPE) ** -0.5
DSA_HEADS = 8
DSA_HEAD_DIM = 64
DSA_SCALE = DSA_HEAD_DIM ** -0.5
IDX_HEADS = 8
IDX_DIM = 64
IDX_TOPK_MAX = 256
REL_BUCKETS = 32
REL_MAX_DIST = 128
N_EXPERTS = 64
TOP_K = 6
N_GROUPS = 8
TOPK_GROUPS = 4
D_EXPERT = 256
D_SHARED = 256
ROUTED_SCALE = 2.5
MLA_OUT = MLA_HEADS * MLA_V
DSA_OUT = DSA_HEADS * DSA_HEAD_DIM
IN_SIZES = (Q_LORA, KV_LORA, MLA_ROPE, DSA_OUT, DSA_OUT, DSA_OUT, IDX_HEADS * IDX_DIM, IDX_DIM, IDX_HEADS)

LANES = 128
INT_MIN = -(2 ** 31)
NEG_INF = float("-inf")
VMEM_LIMIT = 56 * 1024 * 1024


def _cparams(sem):
    return pltpu.CompilerParams(dimension_semantics=sem, vmem_limit_bytes=VMEM_LIMIT)


def _split_dot(x, m01, passes=3):
    acc = None
    r = x
    for p in range(passes):
        hi = r.astype(BF16)
        part = jnp.dot(hi, m01, preferred_element_type=F32)
        acc = part if acc is None else acc + part
        if p + 1 < passes:
            r = r - hi.astype(F32)
    return acc


def _group_mean(sq, bmat):
    outs = [_split_dot(sq[:, s * LANES:(s + 1) * LANES], bmat) for s in range(sq.shape[1] // LANES)]
    return outs[0] if len(outs) == 1 else jnp.concatenate(outs, axis=1)


def _rope_slabs(x, cos, sin):
    lane = lax.broadcasted_iota(I32, (x.shape[0], LANES), 1)
    first_half = (lane % MLA_ROPE) < (MLA_ROPE // 2)
    outs = []
    for s in range(x.shape[1] // LANES):
        xs = x[:, s * LANES:(s + 1) * LANES]
        rot = jnp.where(first_half, pltpu.roll(xs, LANES - MLA_ROPE // 2, 1), pltpu.roll(xs, MLA_ROPE // 2, 1))
        outs.append(xs * cos + rot * sin)
    return outs[0] if len(outs) == 1 else jnp.concatenate(outs, axis=1)


def _nt_dot(a, b):
    return lax.dot_general(a, b, (((1,), (1,)), ((), ())), preferred_element_type=F32)


def _tn_dot(a, b):
    return lax.dot_general(a, b, (((0,), (0,)), ((), ())), preferred_element_type=F32)


def _fold_lanes(x, op):
    acc = x[:, :LANES]
    for s in range(1, x.shape[1] // LANES):
        acc = op(acc, x[:, s * LANES:(s + 1) * LANES])
    return acc


def _adaln_kernel(c_ref, w_ref, b_ref, o_ref):
    c = c_ref[...]
    s = (c * jax.nn.sigmoid(c)).astype(BF16)
    o_ref[...] = jnp.dot(s, w_ref[...], preferred_element_type=F32) + b_ref[...]


def _adaln(c, w_bf, b):
    rows = c.shape[0]
    n = w_bf.shape[1]
    tn = 1536
    return pl.pallas_call(
        _adaln_kernel,
        grid=(n // tn,),
        in_specs=[pl.BlockSpec((rows, D_MODEL), lambda j: (0, 0)),
                  pl.BlockSpec((D_MODEL, tn), lambda j: (0, j)),
                  pl.BlockSpec((1, tn), lambda j: (0, j))],
        out_specs=pl.BlockSpec((rows, tn), lambda j: (0, j)),
        out_shape=jax.ShapeDtypeStruct((rows, n), F32),
        compiler_params=_cparams(("arbitrary",)),
        name="adaln",
    )(c, w_bf, b.reshape(1, n))


_C_QLAT = 0
_C_KV = _C_QLAT + Q_LORA
_C_QD = _C_KV + KV_LORA
_C_KD = _C_QD + DSA_OUT
_C_VD = _C_KD + DSA_OUT
_C_QI = _C_VD + DSA_OUT
_C_MISC = _C_QI + IDX_HEADS * IDX_DIM
_C_KIDUP = _C_MISC + LANES
_C_END = _C_KIDUP + LANES
_M_KI = MLA_ROPE
_M_WI = MLA_ROPE + IDX_DIM


def _proj_kernel(x_ref, sh_ref, sc_ref, ga_ref, win_ref, gql_ref, wqu_ref, gkv_ref, wuk_ref,
                 gq_ref, gk_ref, gm_ref, gdq_ref, gdk_ref, cq_ref, sq_ref, cm_ref, sm_ref,
                 bq_ref, b64_ref, bm_ref, ex_ref,
                 qall_ref, kall_ref, ckv_ref, ckvb_ref, misc_ref, qd_ref, kd_ref, kdb_ref, vd_ref, vdb_ref,
                 qi_ref, kid_ref, wib_ref):
    x = x_ref[...]
    xn = x * lax.rsqrt(jnp.mean(x * x, axis=-1, keepdims=True) + EPS) * ga_ref[...]
    h = xn * (1.0 + sc_ref[...]) + sh_ref[...]
    p = jnp.dot(h.astype(BF16), win_ref[...], preferred_element_type=F32)

    ql = p[:, _C_QLAT:_C_KV]
    qln = ql * lax.rsqrt(jnp.mean(ql * ql, axis=-1, keepdims=True) + EPS) * gql_ref[...]
    q = jnp.dot(qln.astype(BF16), wqu_ref[...], preferred_element_type=F32)
    qn = q * lax.rsqrt(_group_mean(q * q, bq_ref[...]) + EPS) * gq_ref[...]
    qall_ref[...] = _rope_slabs(qn, cq_ref[...], sq_ref[...]).astype(BF16)

    kv = p[:, _C_KV:_C_QD]
    ckv = kv * lax.rsqrt(jnp.mean(kv * kv, axis=-1, keepdims=True) + EPS) * gkv_ref[...]
    ckv_ref[...] = ckv
    ckvb = ckv.astype(BF16)
    ckvb_ref[...] = ckvb
    kn = jnp.dot(ckvb, wuk_ref[...], preferred_element_type=F32)
    kn = kn * lax.rsqrt(_group_mean(kn * kn, bq_ref[...]) + EPS) * gk_ref[...]

    m = p[:, _C_MISC:_C_KIDUP]
    lane = lax.broadcasted_iota(I32, m.shape, 1)
    is_kr = lane < MLA_ROPE
    mm = _split_dot(m * m, bm_ref[...])
    mn = jnp.where(is_kr, m * lax.rsqrt(mm + EPS) * gm_ref[...], m)
    mr = _rope_slabs(mn, cm_ref[...], sm_ref[...])
    is_wi = (lane >= _M_WI) & (lane < _M_WI + IDX_HEADS)
    misc = jnp.where(is_wi, mr * (IDX_HEADS ** -0.5), mr)
    misc_ref[...] = misc
    wib_ref[...] = _split_dot(misc, ex_ref[...])
    kr_placed = jnp.where((lane >= MLA_NOPE) & (lane < MLA_NOPE + MLA_ROPE), pltpu.roll(mr, MLA_NOPE, 1), 0.0)
    kall_ref[...] = jnp.concatenate(
        [kn[:, s * LANES:(s + 1) * LANES] + kr_placed for s in range(MLA_HEADS)], axis=1).astype(BF16)

    qd = p[:, _C_QD:_C_KD]
    qd_ref[...] = qd * lax.rsqrt(_group_mean(qd * qd, b64_ref[...]) + EPS) * gdq_ref[...]
    kd = p[:, _C_KD:_C_VD]
    kdn = kd * lax.rsqrt(_group_mean(kd * kd, b64_ref[...]) + EPS) * gdk_ref[...]
    kd_ref[...] = kdn
    kdb_ref[...] = kdn.astype(BF16)
    vd = p[:, _C_VD:_C_QI]
    vd_ref[...] = vd
    vdb_ref[...] = vd.astype(BF16)
    qi_ref[...] = p[:, _C_QI:_C_MISC].astype(BF16)
    kid_ref[...] = p[:, _C_KIDUP:_C_END].astype(BF16)


def _block_mean_matrix(blocks):
    m = np.zeros((LANES, LANES), np.float32)
    for start, size in blocks:
        m[start:start + size, start:start + size] = 1.0 / size
    return jnp.asarray(m, BF16)


def _head_weight_expander():
    m = np.zeros((LANES, IDX_HEADS * LANES), np.float32)
    for h in range(IDX_HEADS):
        m[_M_WI + h, h * LANES:(h + 1) * LANES] = 1.0
    return jnp.asarray(m, BF16)


def _prep_proj_weights(w_in, g_q_lora, w_q_up, g_kv_lora, w_uk, g_mla_qn, g_mla_qr, g_mla_kn, g_mla_kr,
                       g_dsa_q, g_dsa_k, g_attn_norm):
    offs = np.cumsum((0,) + IN_SIZES)
    sec = lambda k: w_in[:, offs[k]:offs[k + 1]]
    zeros = lambda n: jnp.zeros((D_MODEL, n), w_in.dtype)
    misc = jnp.concatenate([sec(2), sec(7), sec(8), zeros(LANES - MLA_ROPE - IDX_DIM - IDX_HEADS)], axis=1)
    w_in_r = jnp.concatenate([sec(0), sec(1), sec(3), sec(4), sec(5), sec(6), misc, sec(7), sec(7)], axis=1)
    wq = w_q_up.reshape(Q_LORA, MLA_HEADS, MLA_NOPE + MLA_ROPE)
    wq = jnp.pad(wq, ((0, 0), (0, 0), (0, LANES - MLA_NOPE - MLA_ROPE))).reshape(Q_LORA, MLA_HEADS * LANES)
    wk = jnp.pad(w_uk, ((0, 0), (0, 0), (0, LANES - MLA_NOPE))).reshape(KV_LORA, MLA_HEADS * LANES)
    pad1 = lambda v, n: jnp.pad(v, (0, n - v.shape[0]))
    gq = jnp.tile(pad1(jnp.concatenate([g_mla_qn, g_mla_qr]), LANES), MLA_HEADS).reshape(1, -1)
    gk = jnp.tile(pad1(g_mla_kn, LANES), MLA_HEADS).reshape(1, -1)
    gm = jnp.concatenate([g_mla_kr, jnp.ones((LANES - MLA_ROPE,), F32)]).reshape(1, -1)
    return dict(
        win=w_in_r.astype(BF16), wqu=wq.astype(BF16), wuk=wk.astype(BF16),
        ga=g_attn_norm.reshape(1, -1), gql=g_q_lora.reshape(1, -1), gkv=g_kv_lora.reshape(1, -1),
        gq=gq, gk=gk, gm=gm,
        gdq=jnp.tile(g_dsa_q, DSA_HEADS).reshape(1, -1), gdk=jnp.tile(g_dsa_k, DSA_HEADS).reshape(1, -1),
        bq=_block_mean_matrix([(0, MLA_NOPE), (MLA_NOPE, MLA_ROPE)]),
        b64=_block_mean_matrix([(0, DSA_HEAD_DIM), (DSA_HEAD_DIM, DSA_HEAD_DIM)]),
        bm=_block_mean_matrix([(0, MLA_ROPE)]),
        ex=_head_weight_expander(),
    )


def _rope_tables(pos):
    half = MLA_ROPE // 2
    inv = ROPE_BASE ** (-jnp.arange(half, dtype=F32) / half)
    ang = pos.astype(F32)[:, None] * inv
    cos, sin = jnp.cos(ang), jnp.sin(ang)
    cos32 = jnp.concatenate([cos, cos], axis=1)
    sin32 = jnp.concatenate([-sin, sin], axis=1)
    n = pos.shape[0]
    ones = lambda w: jnp.ones((n, w), F32)
    zeros = lambda w: jnp.zeros((n, w), F32)
    cq = jnp.concatenate([ones(MLA_NOPE), cos32, ones(LANES - MLA_NOPE - MLA_ROPE)], axis=1)
    sq = jnp.concatenate([zeros(MLA_NOPE), sin32, zeros(LANES - MLA_NOPE - MLA_ROPE)], axis=1)
    cm = jnp.concatenate([cos32, ones(LANES - MLA_ROPE)], axis=1)
    sm = jnp.concatenate([sin32, zeros(LANES - MLA_ROPE)], axis=1)
    return cq, sq, cm, sm


def _project(x2d, shift, scale, pw, tables, tm, rows_per_mod):
    n = x2d.shape[0]
    nt = n // tm
    cq, sq, cm, sm = tables
    tpos = cq.shape[0] // tm
    const = lambda shape: pl.BlockSpec(shape, lambda i: (0,) * len(shape))
    row = lambda w: pl.BlockSpec((tm, w), lambda i: (i, 0))
    if rows_per_mod:
        per = rows_per_mod // tm
        mod_spec = pl.BlockSpec((None, 1, D_MODEL), lambda i: (i // per, 0, 0))
    else:
        mod_spec = row(D_MODEL)
    tab = pl.BlockSpec((tm, LANES), lambda i: (i % tpos, 0))
    in_specs = [row(D_MODEL), mod_spec, mod_spec, const((1, D_MODEL)), const((D_MODEL, _C_END)),
                const((1, Q_LORA)), const((Q_LORA, MLA_HEADS * LANES)), const((1, KV_LORA)),
                const((KV_LORA, MLA_HEADS * LANES)), const((1, MLA_HEADS * LANES)), const((1, MLA_HEADS * LANES)),
                const((1, LANES)), const((1, DSA_OUT)), const((1, DSA_OUT)), tab, tab, tab, tab,
                const((LANES, LANES)), const((LANES, LANES)), const((LANES, LANES)),
                const((LANES, IDX_HEADS * LANES))]
    widths = [(MLA_HEADS * LANES, BF16), (MLA_HEADS * LANES, BF16), (KV_LORA, F32), (KV_LORA, BF16), (LANES, F32),
              (DSA_OUT, F32), (DSA_OUT, F32), (DSA_OUT, BF16), (DSA_OUT, F32), (DSA_OUT, BF16),
              (IDX_HEADS * IDX_DIM, BF16), (LANES, BF16), (IDX_HEADS * LANES, F32)]
    outs = pl.pallas_call(
        _proj_kernel,
        grid=(nt,),
        in_specs=in_specs,
        out_specs=[row(w) for w, _ in widths],
        out_shape=[jax.ShapeDtypeStruct((n, w), dt) for w, dt in widths],
        compiler_params=_cparams(("arbitrary",)),
        name="project",
    )(x2d, shift, scale, pw["ga"], pw["win"], pw["gql"], pw["wqu"], pw["gkv"], pw["wuk"],
      pw["gq"], pw["gk"], pw["gm"], pw["gdq"], pw["gdk"], cq, sq, cm, sm, pw["bq"], pw["b64"], pw["bm"],
      pw["ex"])
    names = ("qall", "kall", "ckv", "ckvb", "misc", "qd", "kd", "kdb", "vd", "vdb", "qi", "kid", "wib")
    return dict(zip(names, outs))


_MLA_TQ = 256
_MLA_TK = 512


def _mla_prompt_kernel(q_ref, k_ref, c_ref, wuv_ref, o_ref):
    tq, tk = _MLA_TQ, _MLA_TK
    i = pl.program_id(1)
    q0 = i * tq
    nkb = (q0 + tq + tk - 1) // tk
    row = q0 + lax.broadcasted_iota(I32, (tq, tk), 0)
    col0 = lax.broadcasted_iota(I32, (tq, tk), 1)
    heads = []
    for h in range(MLA_HEADS):
        qh = q_ref[:, h * LANES:(h + 1) * LANES]

        def body(j, carry, qh=qh, h=h):
            m, l, acc = carry
            k0 = pl.multiple_of(j * tk, tk)
            kh = k_ref[pl.ds(k0, tk), h * LANES:(h + 1) * LANES]
            s = _nt_dot(qh, kh) * MLA_SCALE
            s = jnp.where(col0 + k0 <= row, s, NEG_INF)
            m_new = jnp.maximum(m, jnp.max(s, axis=1, keepdims=True))
            alpha = jnp.exp(m - m_new)
            p = jnp.exp(s - m_new)
            l = alpha * l + jnp.sum(p, axis=1, keepdims=True)
            acc = alpha * acc + jnp.dot(p.astype(BF16), c_ref[pl.ds(k0, tk), :], preferred_element_type=F32)
            return m_new, l, acc

        init = (jnp.full((tq, 1), NEG_INF, F32), jnp.zeros((tq, 1), F32), jnp.zeros((tq, KV_LORA), F32))
        m, l, acc = lax.fori_loop(0, nkb, body, init)
        heads.append((acc / l).astype(BF16))
    for pr in range(MLA_HEADS // 2):
        o = (jnp.dot(heads[2 * pr], wuv_ref[2 * pr], preferred_element_type=F32)
             + jnp.dot(heads[2 * pr + 1], wuv_ref[2 * pr + 1], preferred_element_type=F32))
        o_ref[:, pr * LANES:(pr + 1) * LANES] = o.astype(BF16)


def _pad_wuv(w_uv):
    w = jnp.transpose(w_uv, (1, 0, 2))
    even = jnp.pad(w, ((0, 0), (0, 0), (0, LANES - MLA_V)))
    odd = jnp.pad(w, ((0, 0), (0, 0), (LANES - MLA_V, 0)))
    is_odd = (jnp.arange(MLA_HEADS) % 2 == 1)[:, None, None]
    return jnp.where(is_odd, odd, even).astype(BF16)


def _mla_prompt(qall, kall, ckvb, wuv_pad):
    b, t, _ = qall.shape
    tq = _MLA_TQ
    assert t % _MLA_TK == 0
    return pl.pallas_call(
        _mla_prompt_kernel,
        grid=(b, t // tq),
        in_specs=[pl.BlockSpec((None, tq, MLA_HEADS * LANES), lambda bi, i: (bi, i, 0)),
                  pl.BlockSpec((None, t, MLA_HEADS * LANES), lambda bi, i: (bi, 0, 0)),
                  pl.BlockSpec((None, t, KV_LORA), lambda bi, i: (bi, 0, 0)),
                  pl.BlockSpec((MLA_HEADS, KV_LORA, LANES), lambda bi, i: (0, 0, 0))],
        out_specs=pl.BlockSpec((None, tq, MLA_OUT), lambda bi, i: (bi, i, 0)),
        out_shape=jax.ShapeDtypeStruct((b, t, MLA_OUT), BF16),
        compiler_params=_cparams(("arbitrary", "arbitrary")),
        name="mla_prompt",
    )(qall, kall, ckvb, wuv_pad)


_DSA_T = 128
_DSA_CW = 512


def _sortable_key(score):
    bits = pltpu.bitcast(score, I32)
    key = jnp.where(bits < 0, bits ^ jnp.int32(0x7FFFFFFF), bits)
    return jnp.where(score == 0.0, 0, key)


def _bias_tiles(rb_ref, tz_ref):
    t = _DSA_T
    r = lax.broadcasted_iota(I32, (t, t), 0)
    c = lax.broadcasted_iota(I32, (t, t), 1)
    exact = REL_BUCKETS // 2
    for which in range(2):
        d = jnp.maximum(r - c + t * which, 0)
        logd = jnp.log(jnp.maximum(d, 1).astype(F32) / exact) / math.log(REL_MAX_DIST / exact)
        far = jnp.minimum(exact + (logd * (REL_BUCKETS - exact)).astype(I32), REL_BUCKETS - 1)
        bucket = jnp.where(d < exact, d, far)
        for h in range(DSA_HEADS):
            tile = jnp.zeros((t, t), F32)
            for bk in range(REL_BUCKETS):
                tile = jnp.where(bucket == bk, rb_ref[bk, h], tile)
            tz_ref[h, which] = tile


def _select_mask(keys_ref, mask_ref, tri_ref, nch, n_sel, rows):
    cw = _DSA_CW

    def count(pred):
        def body(j, acc):
            kb = keys_ref[:, pl.ds(pl.multiple_of(j * cw, cw), cw)]
            return acc + _fold_lanes(jnp.where(pred(kb), 1.0, 0.0), jnp.add)
        acc = lax.fori_loop(0, nch, body, jnp.zeros((rows, LANES), F32))
        return jnp.sum(acc, axis=1, keepdims=True)

    kf = jnp.float32(n_sel)
    zero = jnp.zeros((rows, 1), I32)
    thr = jnp.where(count(lambda kb: kb >= zero) >= kf, zero, jnp.full((rows, 1), INT_MIN, I32))

    def bit_body(bi, thr):
        cand = thr + lax.shift_left(jnp.int32(1), 30 - bi)
        return jnp.where(count(lambda kb: kb >= cand) >= kf, cand, thr)

    thr = lax.fori_loop(0, 31, bit_body, thr)
    need = kf - count(lambda kb: kb > thr)

    def mask_body(j, carry):
        k0 = pl.multiple_of(j * cw, cw)
        kb = keys_ref[:, pl.ds(k0, cw)]
        eq = kb == thr
        eqf = jnp.where(eq, 1.0, 0.0)
        before = jnp.dot(eqf.astype(BF16), tri_ref[...], preferred_element_type=F32) + carry
        keep = ((kb > thr) | (eq & (before < need))) & (kb > INT_MIN)
        mask_ref[:, pl.ds(k0, cw)] = jnp.where(keep, 0.0, NEG_INF)
        return carry + jnp.sum(eqf, axis=1, keepdims=True)

    lax.fori_loop(0, nch, mask_body, jnp.zeros((rows, 1), F32))


def _dsa_prompt_kernel(rb_ref, qi_ref, wib_ref, kid_ref, qd_ref, kd_ref, vd_ref, tri_ref, o_ref,
                       keys_ref, mask_ref, tz_ref, s_ref, *, n_sel):
    t, cw = _DSA_T, _DSA_CW
    sub_n = cw // t
    bi = pl.program_id(0)
    i = pl.program_id(1)
    nch = i // sub_n + 1

    @pl.when((bi == 0) & (i == 0))
    def _():
        _bias_tiles(rb_ref, tz_ref)

    row = i * t + lax.broadcasted_iota(I32, (t, cw), 0)
    col0 = lax.broadcasted_iota(I32, (t, cw), 1)
    lane = lax.broadcasted_iota(I32, (t, LANES), 1)
    low = lane < DSA_HEAD_DIM

    def head_halves(ref, pr):
        qs = ref[:, pr * LANES:(pr + 1) * LANES].astype(BF16)
        zero = jnp.zeros_like(qs)
        return jnp.where(low, qs, zero), jnp.where(low, zero, qs)

    qim = [q for pr in range(IDX_HEADS // 2) for q in head_halves(qi_ref, pr)]

    def idx_body(c, _):
        k0 = pl.multiple_of(c * cw, cw)
        kk = kid_ref[pl.ds(k0, cw), :]
        acc = jnp.zeros((t, cw), F32)
        for h in range(IDX_HEADS):
            r = jnp.maximum(_nt_dot(qim[h], kk) * (IDX_DIM ** -0.5), 0.0)
            w = wib_ref[:, h * LANES:(h + 1) * LANES]
            acc = acc + jnp.concatenate([w] * sub_n, axis=1) * r
        keys_ref[:, pl.ds(k0, cw)] = jnp.where(col0 + k0 <= row, _sortable_key(acc), INT_MIN)
        return 0

    lax.fori_loop(0, nch, idx_body, 0)
    _select_mask(keys_ref, mask_ref, tri_ref, nch, n_sel, t)

    for pr in range(DSA_HEADS // 2):
        qms = head_halves(qd_ref, pr)
        fars = [rb_ref[REL_BUCKETS - 1, 2 * pr + half] for half in range(2)]

        def pass_a(c, mrun, pr=pr, qms=qms, fars=fars):
            k0 = pl.multiple_of(c * cw, cw)
            kb = kd_ref[pl.ds(k0, cw), pr * LANES:(pr + 1) * LANES]
            mk = mask_ref[:, pl.ds(k0, cw)]
            out = []
            for half in range(2):
                h = 2 * pr + half
                s = _nt_dot(qms[half], kb) * DSA_SCALE
                parts = []
                for sub in range(sub_n):
                    blk = c * sub_n + sub
                    bias = jnp.where(blk == i, tz_ref[h, 0], jnp.where(blk == i - 1, tz_ref[h, 1], fars[half]))
                    parts.append(s[:, sub * t:(sub + 1) * t] + bias)
                s = jnp.concatenate(parts, axis=1) + mk
                s_ref[half, :, pl.ds(k0, cw)] = s
                out.append(jnp.maximum(mrun[half], _fold_lanes(s, jnp.maximum)))
            return tuple(out)

        ninf = jnp.full((t, LANES), NEG_INF, F32)
        mrun = lax.fori_loop(0, nch, pass_a, (ninf, ninf))
        ms = [jnp.max(m, axis=1, keepdims=True) for m in mrun]

        def pass_b(c, carry, pr=pr, ms=ms):
            k0 = pl.multiple_of(c * cw, cw)
            vb = vd_ref[pl.ds(k0, cw), pr * LANES:(pr + 1) * LANES]
            out = []
            for half in range(2):
                l, acc = carry[half]
                p = jnp.exp(s_ref[half, :, pl.ds(k0, cw)] - ms[half])
                out.append((l + _fold_lanes(p, jnp.add),
                            acc + jnp.dot(p.astype(BF16), vb, preferred_element_type=F32)))
            return tuple(out)

        zero = jnp.zeros((t, LANES), F32)
        (l0, a0), (l1, a1) = lax.fori_loop(0, nch, pass_b, ((zero, zero), (zero, zero)))
        o0 = a0 / jnp.sum(l0, axis=1, keepdims=True)
        o1 = a1 / jnp.sum(l1, axis=1, keepdims=True)
        o_ref[:, pr * LANES:(pr + 1) * LANES] = jnp.where(low, o0, o1).astype(BF16)


def _strict_upper(n):
    return jnp.asarray(np.triu(np.ones((n, n), np.float32), 1), BF16)


def _dsa_prompt(rel_bias, qi, wib, kid, qd, kdb, vdb):
    b, t, _ = qi.shape
    tq, cw = _DSA_T, _DSA_CW
    assert t % cw == 0
    n_sel = min(IDX_TOPK_MAX, t // 4)
    blk = lambda w: pl.BlockSpec((None, tq, w), lambda bi, i: (bi, i, 0))
    full = lambda w: pl.BlockSpec((None, t, w), lambda bi, i: (bi, 0, 0))
    return pl.pallas_call(
        functools.partial(_dsa_prompt_kernel, n_sel=n_sel),
        grid=(b, t // tq),
        in_specs=[pl.BlockSpec(memory_space=pltpu.SMEM),
                  blk(IDX_HEADS * IDX_DIM), blk(IDX_HEADS * LANES), full(LANES), blk(DSA_OUT), full(DSA_OUT),
                  full(DSA_OUT), pl.BlockSpec((cw, cw), lambda bi, i: (0, 0))],
        out_specs=blk(DSA_OUT),
        out_shape=jax.ShapeDtypeStruct((b, t, DSA_OUT), BF16),
        scratch_shapes=[pltpu.VMEM((tq, t), I32), pltpu.VMEM((tq, t), F32),
                        pltpu.VMEM((DSA_HEADS, 2, tq, tq), F32), pltpu.VMEM((2, tq, t), F32)],
        compiler_params=_cparams(("arbitrary", "arbitrary")),
        name="dsa_prompt",
    )(rel_bias, qi, wib, kid, qd, kdb, vdb, _strict_upper(cw))


_EXPERTS_PER_GROUP = N_EXPERTS // N_GROUPS
_MOE_GROUP = 4


def _first_index_of_max(v, idx, axis, sentinel):
    mx = jnp.max(v, axis=axis, keepdims=True)
    first = jnp.min(jnp.where(v == mx, idx, sentinel), axis=axis, keepdims=True)
    return mx, first


def _route(logits_t, bias_col):
    n_tok = logits_t.shape[1]
    scores = jax.nn.sigmoid(logits_t)
    biased = scores + bias_col
    b3 = biased.reshape(N_GROUPS, _EXPERTS_PER_GROUP, n_tok)
    j3 = lax.broadcasted_iota(I32, b3.shape, 1)
    m1, f1 = _first_index_of_max(b3, j3, 1, _EXPERTS_PER_GROUP)
    m2 = jnp.max(jnp.where(j3 == f1, NEG_INF, b3), axis=1, keepdims=True)
    gs = (m1 + m2).reshape(N_GROUPS, n_tok)
    gi = lax.broadcasted_iota(I32, gs.shape, 0)
    gsel = jnp.zeros(gs.shape, jnp.bool_)
    for _ in range(TOPK_GROUPS):
        _, first = _first_index_of_max(gs, gi, 0, N_GROUPS)
        hit = gi == first
        gsel = gsel | hit
        gs = jnp.where(hit, NEG_INF, gs)
    gsel3 = jnp.broadcast_to(gsel.reshape(N_GROUPS, 1, n_tok), b3.shape)
    masked = jnp.where(gsel3, b3, NEG_INF).reshape(N_EXPERTS, n_tok)
    ei = lax.broadcasted_iota(I32, masked.shape, 0)
    sel = jnp.zeros(masked.shape, jnp.bool_)
    for _ in range(TOP_K):
        _, first = _first_index_of_max(masked, ei, 0, N_EXPERTS)
        hit = ei == first
        sel = sel | hit
        masked = jnp.where(hit, NEG_INF, masked)
    w = jnp.where(sel, scores, 0.0)
    gate = w / jnp.sum(w, axis=0, keepdims=True) * ROUTED_SCALE
    return sel, gate


def _swiglu_bf(x_bf, wgu_ref, wdown_ref, d_hidden):
    gu = jnp.dot(x_bf, wgu_ref[...], preferred_element_type=F32)
    g, u = gu[:, :d_hidden], gu[:, d_hidden:]
    act = (g * jax.nn.sigmoid(g)) * u
    return jnp.dot(act.astype(BF16), wdown_ref[...], preferred_element_type=F32)


def _moe_kernel(x_ref, oa_ref, ob_ref, ga_ref, sf_ref, cf_ref, gf_ref, gn_ref, wo_ref, wr_ref, br_ref,
                wsg_ref, wsd_ref, tri_ref, weg_ref, wed_ref, y_ref, xt_ref, acc_ref, gate_ref, rank_ref,
                *, cap):
    tm = x_ref.shape[0]
    e = pl.program_id(1)

    @pl.when(e == 0)
    def _():
        half = oa_ref.shape[1]
        attn = (jnp.dot(oa_ref[...], wo_ref[:half, :], preferred_element_type=F32)
                + jnp.dot(ob_ref[...], wo_ref[half:, :], preferred_element_type=F32))
        x1 = x_ref[...] + ga_ref[...] * attn
        y_ref[...] = x1
        hn = x1 * lax.rsqrt(jnp.mean(x1 * x1, axis=-1, keepdims=True) + EPS) * gn_ref[...]
        xt = (hn * (1.0 + cf_ref[...]) + sf_ref[...]).astype(BF16)
        xt_ref[...] = xt
        sel, gate = _route(_nt_dot(wr_ref[...], xt), br_ref[...])
        ind = jnp.where(sel, 1.0, 0.0)
        before = jnp.dot(ind.astype(BF16), tri_ref[...], preferred_element_type=F32)
        rank_ref[...] = jnp.where(sel, before, -1.0)
        gate_ref[...] = jnp.where(sel, gate, 0.0)
        acc_ref[...] = jnp.zeros_like(acc_ref)

    grp = _MOE_GROUP
    rank_rows = [rank_ref[pl.ds(e * grp + j, 1), :] for j in range(grp)]
    gate_rows = [gate_ref[pl.ds(e * grp + j, 1), :] for j in range(grp)]
    top = rank_rows[0]
    for j in range(1, grp):
        top = jnp.maximum(top, rank_rows[j])
    count = (jnp.max(top) + 1.0).astype(I32)
    n_chunks = (count + cap - 1) // cap
    slot = lax.broadcasted_iota(I32, (cap, tm), 0).astype(F32)

    def chunk(c, _):
        base = slot + (c * cap).astype(F32)
        picks = [jnp.where(base == rank_rows[j], 1.0, 0.0) for j in range(grp)]
        pick_b = jnp.concatenate(picks, axis=0).astype(BF16)
        xe = jnp.dot(pick_b, xt_ref[...], preferred_element_type=F32).astype(BF16)
        ys = []
        for j in range(grp):
            ye = _swiglu_bf(xe[j * cap:(j + 1) * cap], weg_ref.at[j], wed_ref.at[j], D_EXPERT)
            ge = jnp.sum(picks[j] * gate_rows[j], axis=1, keepdims=True)
            ys.append((ye * ge).astype(BF16))
        acc_ref[...] += _tn_dot(pick_b, jnp.concatenate(ys, axis=0))
        return 0

    lax.fori_loop(0, n_chunks, chunk, 0)

    @pl.when(e == N_EXPERTS // grp - 1)
    def _():
        shared = _swiglu_bf(xt_ref[...], wsg_ref, wsd_ref, D_SHARED)
        y_ref[...] = y_ref[...] + gf_ref[...] * (acc_ref[...] + shared)


def _moe(x2d, oa, ob, mods, g_ffn, wo_bf, wr_t_bf, b_router, wsg_bf, wsd_bf, weg_bf, wed_bf, tm, rows_per_mod):
    n = x2d.shape[0]
    assert n % tm == 0 and tm % LANES == 0
    nt = n // tm
    cap = LANES
    const = lambda shape: pl.BlockSpec(shape, lambda t, e: (0,) * len(shape))
    row = lambda w: pl.BlockSpec((tm, w), lambda t, e: (t, 0))
    if rows_per_mod:
        per = rows_per_mod // tm
        mod_spec = pl.BlockSpec((None, 1, D_MODEL), lambda t, e: (t // per, 0, 0))
    else:
        mod_spec = row(D_MODEL)
    half = oa.shape[1]
    return pl.pallas_call(
        functools.partial(_moe_kernel, cap=cap),
        grid=(nt, N_EXPERTS // _MOE_GROUP),
        in_specs=[row(D_MODEL), row(half), row(half), mod_spec, mod_spec, mod_spec, mod_spec,
                  const((1, D_MODEL)), const((2 * half, D_MODEL)), const((N_EXPERTS, D_MODEL)),
                  const((N_EXPERTS, 1)), const((D_MODEL, 2 * D_SHARED)), const((D_SHARED, D_MODEL)),
                  const((tm, tm)),
                  pl.BlockSpec((_MOE_GROUP, D_MODEL, 2 * D_EXPERT), lambda t, e: (e, 0, 0)),
                  pl.BlockSpec((_MOE_GROUP, D_EXPERT, D_MODEL), lambda t, e: (e, 0, 0))],
        out_specs=row(D_MODEL),
        out_shape=jax.ShapeDtypeStruct((n, D_MODEL), F32),
        scratch_shapes=[pltpu.VMEM((tm, D_MODEL), BF16), pltpu.VMEM((tm, D_MODEL), F32),
                        pltpu.VMEM((N_EXPERTS, tm), F32), pltpu.VMEM((N_EXPERTS, tm), F32)],
        compiler_params=_cparams(("arbitrary", "arbitrary")),
        name="moe",
    )(x2d, oa, ob, *mods, g_ffn.reshape(1, -1), wo_bf, wr_t_bf, b_router.reshape(-1, 1), wsg_bf, wsd_bf,
      _strict_upper(tm), weg_bf, wed_bf)


def _col_blocks(q):
    nh = q.shape[0]
    r = lax.broadcasted_iota(I32, (nh, LANES), 0)
    c = lax.broadcasted_iota(I32, (nh, LANES), 1)
    blocks = [_tn_dot(q, jnp.where((r == h) & (c == h), 1.0, 0.0).astype(q.dtype)) for h in range(nh)]
    return jnp.concatenate(blocks, axis=0)


def _rows_to_cols(row):
    r8 = jnp.broadcast_to(row, (8, row.shape[1]))
    e0 = jnp.where(lax.broadcasted_iota(I32, (8, LANES), 0) == 0, 1.0, 0.0).astype(row.dtype)
    return _tn_dot(r8, e0)


def _head_diag(acc, width):
    r = lax.broadcasted_iota(I32, acc.shape, 0)
    c = lax.broadcasted_iota(I32, acc.shape, 1)
    return jnp.sum(jnp.where(r == c // width, acc, 0.0), axis=0, keepdims=True)


class _PageStream:
    def __init__(self, hbm_ref, buf_ref, sem_ref, pt_ref, seq, pages_per_chunk):
        self.hbm, self.buf, self.sem, self.pt, self.seq, self.ppc = hbm_ref, buf_ref, sem_ref, pt_ref, seq, pages_per_chunk

    def _copy(self, page, slot, r):
        return pltpu.make_async_copy(self.hbm.at[page], self.buf.at[slot, r], self.sem.at[slot])

    def start(self, chunk, slot):
        for r in range(self.ppc):
            self._copy(self.pt[self.seq, chunk * self.ppc + r], slot, r).start()

    def wait(self, slot):
        for r in range(self.ppc):
            self._copy(0, slot, r).wait()


_STREAM_DEPTH = 8


def _stream_loop(streams, n_chunks, body):
    depth = _STREAM_DEPTH
    for d in range(min(depth - 1, n_chunks)):
        for st in streams:
            st.start(d, d)

    def step(c, _):
        slot = c % depth
        nxt = c + depth - 1

        @pl.when(nxt < n_chunks)
        def _():
            for st in streams:
                st.start(nxt, nxt % depth)

        for st in streams:
            st.wait(slot)
        body(c, slot)
        return 0

    lax.fori_loop(0, n_chunks, step, 0)


def _softmax_stats(sc_ref, n_rows, blk):
    nb = n_rows // blk
    tail = n_rows - nb * blk

    def mx_body(j, m):
        return jnp.maximum(m, jnp.max(sc_ref[pl.ds(pl.multiple_of(j * blk, blk), blk), :], axis=0, keepdims=True))

    m = lax.fori_loop(0, nb, mx_body, jnp.full((1, LANES), NEG_INF, F32))
    if tail:
        m = jnp.maximum(m, jnp.max(sc_ref[pl.ds(nb * blk, tail), :], axis=0, keepdims=True))

    def sum_body(j, l):
        return l + jnp.sum(jnp.exp(sc_ref[pl.ds(pl.multiple_of(j * blk, blk), blk), :] - m), axis=0, keepdims=True)

    l = lax.fori_loop(0, nb, sum_body, jnp.zeros((1, LANES), F32))
    if tail:
        l = l + jnp.sum(jnp.exp(sc_ref[pl.ds(nb * blk, tail), :] - m), axis=0, keepdims=True)
    return m, l


_MLA_S_PAGES = 8


def _mla_sample_kernel(pt_ref, q_ref, knew_ref, cnew_ref, wuk_ref, gk_ref, wuv_ref, lat_hbm, kr_hbm, o_ref,
                       latbuf, krbuf, sems, sc_ref, latbf_ref, *, n_pages):
    s_id = pl.program_id(0)
    ppc = math.gcd(n_pages, _MLA_S_PAGES)
    ck = ppc * PAGE_SIZE
    n_chunks = n_pages // ppc
    past = n_pages * PAGE_SIZE
    row8 = lax.broadcasted_iota(I32, (8, LANES), 0)

    qblk = _col_blocks(q_ref[...]).astype(BF16)
    qr = qblk[MLA_NOPE:MLA_NOPE + MLA_ROPE, :].astype(F32)
    for h in range(1, MLA_HEADS):
        qr = qr + qblk[h * LANES + MLA_NOPE:h * LANES + MLA_NOPE + MLA_ROPE, :].astype(F32)
    qr = qr.astype(BF16)
    lat_stream = _PageStream(lat_hbm, latbuf, sems.at[0], pt_ref, s_id, ppc)
    kr_stream = _PageStream(kr_hbm, krbuf, sems.at[1], pt_ref, s_id, ppc)

    def score_chunk(c, slot):
        lat = latbuf[slot].reshape(ck, KV_LORA).astype(BF16)
        latbf_ref[pl.ds(pl.multiple_of(c * ck, ck), ck), :] = lat
        kraw = jnp.dot(lat, wuk_ref[...], preferred_element_type=F32)
        slabs = []
        for h in range(MLA_HEADS):
            x = kraw[:, h * LANES:(h + 1) * LANES]
            ms = jnp.sum(x * x, axis=1, keepdims=True) * (1.0 / MLA_NOPE)
            slabs.append((x * lax.rsqrt(ms + EPS) * gk_ref[:, h * LANES:(h + 1) * LANES]).astype(BF16))
        kn = jnp.concatenate(slabs, axis=1)
        s_rope = jnp.concatenate([_tn_dot(krbuf[slot, r].astype(BF16), qr) for r in range(ppc)], axis=0)
        s = (jnp.dot(kn, qblk, preferred_element_type=F32) + s_rope) * MLA_SCALE
        sc_ref[pl.ds(pl.multiple_of(c * ck, ck), ck), :] = s

    _stream_loop((lat_stream, kr_stream), n_chunks, score_chunk)
    s_new = jnp.dot(jnp.broadcast_to(knew_ref[...], (8, knew_ref.shape[1])), qblk,
                    preferred_element_type=F32) * MLA_SCALE
    sc_ref[pl.ds(past, 8), :] = jnp.where(row8 == 0, s_new, NEG_INF)
    m, l = _softmax_stats(sc_ref, past + 8, ck)

    def pv_chunk(c, acc):
        k0 = pl.multiple_of(c * ck, ck)
        p = jnp.exp(sc_ref[pl.ds(k0, ck), :] - m) / l
        return acc + _tn_dot(p.astype(BF16), latbf_ref[pl.ds(k0, ck), :])

    acc = lax.fori_loop(0, n_chunks, pv_chunk, jnp.zeros((LANES, KV_LORA), F32))
    p_new = jnp.exp(sc_ref[pl.ds(past, 8), :] - m) / l
    acc = acc + _tn_dot(p_new.astype(BF16), jnp.broadcast_to(cnew_ref[...], (8, KV_LORA)))
    out = jnp.dot(acc.astype(BF16), wuv_ref[...], preferred_element_type=F32)
    o_ref[...] = _head_diag(out, MLA_V).astype(BF16)


def _mla_sample(page_table, q8, knew, cnew, wuk_pad, gk, wuv_flat, cache_lat, cache_kr):
    db, n_pages = page_table.shape
    ppc = math.gcd(n_pages, _MLA_S_PAGES)
    past = n_pages * PAGE_SIZE
    per_seq = lambda shape: pl.BlockSpec((None,) + shape, lambda s, pt: (s,) + (0,) * len(shape))
    const = lambda shape: pl.BlockSpec(shape, lambda s, pt: (0,) * len(shape))
    grid_spec = pltpu.PrefetchScalarGridSpec(
        num_scalar_prefetch=1, grid=(db,),
        in_specs=[per_seq((MLA_HEADS, LANES)), per_seq((1, MLA_HEADS * LANES)), per_seq((1, KV_LORA)),
                  const((KV_LORA, MLA_HEADS * LANES)), const((1, MLA_HEADS * LANES)), const((KV_LORA, MLA_OUT)),
                  pl.BlockSpec(memory_space=pl.ANY), pl.BlockSpec(memory_space=pl.ANY)],
        out_specs=per_seq((1, MLA_OUT)),
        scratch_shapes=[pltpu.VMEM((_STREAM_DEPTH, ppc, PAGE_SIZE, KV_LORA), F32),
                        pltpu.VMEM((_STREAM_DEPTH, ppc, MLA_ROPE, PAGE_SIZE), F32),
                        pltpu.SemaphoreType.DMA((2, _STREAM_DEPTH)), pltpu.VMEM((past + 8, LANES), F32),
                        pltpu.VMEM((past, KV_LORA), BF16)])
    return pl.pallas_call(
        functools.partial(_mla_sample_kernel, n_pages=n_pages),
        grid_spec=grid_spec,
        out_shape=jax.ShapeDtypeStruct((db, 1, MLA_OUT), BF16),
        compiler_params=_cparams(("arbitrary",)),
        name="mla_sample",
    )(page_table, q8, knew, cnew, wuk_pad, gk, wuv_flat, cache_lat, cache_kr)


_DSA_S_IDX_PAGES = 16


def _select_flat(keys, tri_u, tri_l, n_sel):
    def count(pred):
        c = jnp.sum(jnp.where(pred, 1.0, 0.0), axis=1, keepdims=True)
        return jnp.sum(c, axis=0, keepdims=True)

    kf = jnp.float32(n_sel)
    zero = jnp.zeros((1, 1), I32)
    thr = jnp.where(count(keys >= zero) >= kf, zero, jnp.full((1, 1), INT_MIN, I32))

    def bit_body(bi, thr):
        cand = thr + lax.shift_left(jnp.int32(1), 30 - bi)
        return jnp.where(count(keys >= cand) >= kf, cand, thr)

    thr = lax.fori_loop(0, 31, bit_body, thr)
    need = kf - count(keys > thr)
    eq = keys == thr
    eqf = jnp.where(eq, 1.0, 0.0)
    within = jnp.dot(eqf.astype(BF16), tri_u, preferred_element_type=F32)
    rowcount = jnp.broadcast_to(jnp.sum(eqf, axis=1, keepdims=True), eqf.shape)
    carry = jnp.dot(tri_l, rowcount.astype(BF16), preferred_element_type=F32)
    keep = ((keys > thr) | (eq & (within + carry < need))) & (keys > INT_MIN)
    return jnp.where(keep, 1.0, 0.0)


_DSA_S_KV_PAGES = 4


def _round_bf16(x):
    return x.astype(BF16).astype(F32)


def _dsa_sample_kernel(pt_ref, rbt_ref, qi_ref, wi_ref, kinew_ref, qd_ref, kdnew_ref, vdnew_ref,
                       triu_ref, tril_ref, idx_hbm, k_hbm, v_hbm, o_ref,
                       idxbuf, kvbuf, sems, keys_ref, mask_ref, s_ref, qcol_ref, acc_ref, *, n_pages, n_sel):
    s_id = pl.program_id(0)
    ipc = math.gcd(n_pages, _DSA_S_IDX_PAGES)
    kpc = math.gcd(n_pages, _DSA_S_KV_PAGES)
    lane1 = lax.broadcasted_iota(I32, (1, LANES), 1)
    qi = qi_ref[...]
    wi = _round_bf16(wi_ref[...])

    keys_ref[...] = jnp.full(keys_ref.shape, INT_MIN, I32)
    idx_stream = _PageStream(idx_hbm, idxbuf, sems.at[0], pt_ref, s_id, ipc)

    def idx_chunk(c, slot):
        for r in range(ipc):
            kk = idxbuf[slot, r].astype(BF16)
            rr = jnp.maximum(jnp.dot(qi, kk, preferred_element_type=F32) * (IDX_DIM ** -0.5), 0.0)
            sc = jnp.sum(wi * _round_bf16(rr), axis=0, keepdims=True)
            keys_ref[pl.ds(c * ipc + r, 1), :] = _sortable_key(sc)

    _stream_loop((idx_stream,), n_pages // ipc, idx_chunk)
    r_new = jnp.maximum(jnp.sum(qi.astype(F32) * kinew_ref[...].astype(F32), axis=1, keepdims=True)
                        * (IDX_DIM ** -0.5), 0.0)
    sc_new = jnp.sum(wi * _round_bf16(r_new), axis=0, keepdims=True)
    keys_ref[pl.ds(n_pages, 1), :] = jnp.where(lane1 == 0, _sortable_key(jnp.broadcast_to(sc_new, (1, LANES))),
                                               INT_MIN)

    mask_ref[...] = _select_flat(keys_ref[...], triu_ref[...], tril_ref[...], n_sel)

    qd = qd_ref[...]
    row8 = lax.broadcasted_iota(I32, (8, LANES), 0)
    e0 = jnp.where(row8 == 0, 1.0, 0.0).astype(BF16)
    for h in range(DSA_HEADS):
        resid = jnp.broadcast_to(qd[h:h + 1, :], (8, DSA_HEAD_DIM))
        col = jnp.zeros((DSA_HEAD_DIM, LANES), F32)
        for _ in range(3):
            piece = resid.astype(BF16)
            col = col + _tn_dot(piece, e0)
            resid = resid - piece.astype(F32)
        qcol_ref[h] = col

    far_col = rbt_ref[:, REL_BUCKETS - 1:REL_BUCKETS]
    exact = REL_BUCKETS // 2
    d_row = PAGE_SIZE - lane1
    logd = jnp.log(jnp.maximum(d_row, 1).astype(F32) / exact) / math.log(REL_MAX_DIST / exact)
    far_b = jnp.minimum(exact + (logd * (REL_BUCKETS - exact)).astype(I32), REL_BUCKETS - 1)
    bucket = jnp.where(d_row < exact, d_row, far_b)
    near = jnp.zeros((DSA_HEADS, LANES), F32)
    for bk in range(REL_BUCKETS):
        near = jnp.where(bucket == bk, rbt_ref[:, bk:bk + 1], near)

    k_stream = _PageStream(k_hbm, kvbuf, sems.at[1], pt_ref, s_id, kpc)
    v_stream = _PageStream(v_hbm, kvbuf, sems.at[1], pt_ref, s_id, kpc)

    def k_chunk(c, slot):
        for r in range(kpc):
            page = c * kpc + r
            for h in range(DSA_HEADS):
                s_ref[page, h:h + 1, :] = jnp.sum(kvbuf[slot, r, h] * qcol_ref[h], axis=0, keepdims=True)
            bias = jnp.where(page == n_pages - 1, near, far_col)
            keep = mask_ref[pl.ds(page, 1), :] > 0.5
            s_ref[page] = jnp.where(keep, s_ref[page] * DSA_SCALE + bias, NEG_INF)

    _stream_loop((k_stream,), n_pages // kpc, k_chunk)
    s_new = (jnp.sum(qd * kdnew_ref[...], axis=1, keepdims=True) * DSA_SCALE
             + rbt_ref[:, 0:1])
    keep_new = mask_ref[pl.ds(n_pages, 1), :][:, 0:1] > 0.5
    s_new = jnp.where(keep_new, s_new, NEG_INF)

    blk = 8
    def mx_body(j, m):
        return jnp.maximum(m, jnp.max(s_ref[pl.ds(pl.multiple_of(j * blk, blk), blk)], axis=0))

    m_t = lax.fori_loop(0, n_pages // blk, mx_body, jnp.full((DSA_HEADS, LANES), NEG_INF, F32))
    m = jnp.maximum(jnp.max(m_t, axis=1, keepdims=True), s_new)

    def sum_body(j, l):
        return l + jnp.sum(jnp.exp(s_ref[pl.ds(pl.multiple_of(j * blk, blk), blk)] - m), axis=0)

    l_t = lax.fori_loop(0, n_pages // blk, sum_body, jnp.zeros((DSA_HEADS, LANES), F32))
    l = jnp.sum(l_t, axis=1, keepdims=True) + jnp.exp(s_new - m)

    acc_ref[...] = jnp.zeros_like(acc_ref)

    def v_chunk(c, slot):
        for r in range(kpc):
            page = c * kpc + r
            p = jnp.exp(s_ref[page] - m) / l
            for h in range(DSA_HEADS):
                acc_ref[h] += p[h:h + 1, :] * kvbuf[slot, r, h]

    _stream_loop((v_stream,), n_pages // kpc, v_chunk)
    ones8 = jnp.ones((8, LANES), BF16)
    rowh = lax.broadcasted_iota(I32, (DSA_HEADS, DSA_HEAD_DIM), 0)
    out = jnp.zeros((DSA_HEADS, DSA_HEAD_DIM), F32)
    for h in range(DSA_HEADS):
        resid = acc_ref[h]
        tot = jnp.zeros((8, DSA_HEAD_DIM), F32)
        for _ in range(3):
            piece = resid.astype(BF16)
            tot = tot + _nt_dot(ones8, piece)
            resid = resid - piece.astype(F32)
        out = jnp.where(rowh == h, tot, out)
    p_new = jnp.exp(s_new - m) / l
    o_ref[...] = (out + p_new * vdnew_ref[...]).astype(BF16)


def _dsa_sample(page_table, rel_bias, qi8, wi8, kinew, qd8, kdnew, vdnew, cache_idx_t, cache_k_t, cache_v_t):
    db, n_pages = page_table.shape
    n_sel = min(IDX_TOPK_MAX, (n_pages * PAGE_SIZE + 1) // 4)
    assert n_pages % 8 == 0
    ipc = math.gcd(n_pages, _DSA_S_IDX_PAGES)
    kpc = math.gcd(n_pages, _DSA_S_KV_PAGES)
    rows = -(-(n_pages + 1) // LANES) * LANES
    tri_l = jnp.asarray(np.tril(np.ones((rows, rows), np.float32), -1), BF16)
    per_seq = lambda shape: pl.BlockSpec((None,) + shape, lambda s, pt: (s,) + (0,) * len(shape))
    const = lambda shape: pl.BlockSpec(shape, lambda s, pt: (0,) * len(shape))
    any_spec = pl.BlockSpec(memory_space=pl.ANY)
    head_tile = (DSA_HEADS, DSA_HEAD_DIM)
    page_tile = (DSA_HEADS, DSA_HEAD_DIM, PAGE_SIZE)
    grid_spec = pltpu.PrefetchScalarGridSpec(
        num_scalar_prefetch=1, grid=(db,),
        in_specs=[const((DSA_HEADS, REL_BUCKETS)),
                  per_seq((IDX_HEADS, IDX_DIM)), per_seq((IDX_HEADS, 1)), per_seq((1, IDX_DIM)),
                  per_seq(head_tile), per_seq(head_tile), per_seq(head_tile),
                  const((LANES, LANES)), const((rows, rows)), any_spec, any_spec, any_spec],
        out_specs=per_seq(head_tile),
        scratch_shapes=[pltpu.VMEM((_STREAM_DEPTH, ipc, IDX_DIM, PAGE_SIZE), F32),
                        pltpu.VMEM((_STREAM_DEPTH, kpc) + page_tile, F32),
                        pltpu.SemaphoreType.DMA((2, _STREAM_DEPTH)),
                        pltpu.VMEM((rows, LANES), I32), pltpu.VMEM((rows, LANES), F32),
                        pltpu.VMEM((n_pages, DSA_HEADS, LANES), F32),
                        pltpu.VMEM(page_tile, F32), pltpu.VMEM(page_tile, F32)])
    return pl.pallas_call(
        functools.partial(_dsa_sample_kernel, n_pages=n_pages, n_sel=n_sel),
        grid_spec=grid_spec,
        out_shape=jax.ShapeDtypeStruct((db,) + head_tile, BF16),
        compiler_params=_cparams(("arbitrary",)),
        name="dsa_sample",
    )(page_table, rel_bias.T, qi8, wi8, kinew, qd8, kdnew, vdnew, _strict_upper(LANES), tri_l,
      cache_idx_t, cache_k_t, cache_v_t)


def kernel(x_prompt, x_sample, cache_mla_latent, cache_mla_krope, cache_dsa_k, cache_dsa_v, cache_idx_k, page_table, c_prompt, c_sample, rel_bias, w_ada, b_ada, g_attn_norm, w_in, g_q_lora, w_q_up, g_kv_lora, w_kv_up, g_mla_qn, g_mla_qr, g_mla_kn, g_mla_kr, g_dsa_q, g_dsa_k, w_out, g_ffn_norm, w_router, b_router, w_e_gu, w_e_down, w_s_gu, w_s_down):
    depth = w_ada.shape[0]
    assert depth == 1, "single-layer trunk"
    l = 0
    B, T, D = x_prompt.shape
    DB, TS, _ = x_sample.shape
    assert TS == 1, "one new token per sampled sequence"
    ns = DB * TS
    past = page_table.shape[1] * PAGE_SIZE

    w_kv = w_kv_up[l].reshape(KV_LORA, MLA_HEADS, MLA_NOPE + MLA_V)
    w_uk, w_uv = w_kv[..., :MLA_NOPE], w_kv[..., MLA_NOPE:]

    mod = _adaln(jnp.concatenate([c_prompt, c_sample], axis=0), w_ada[l].astype(BF16), b_ada[l])
    mod_p = [m.reshape(B, 1, D) for m in jnp.split(mod[:B], 6, axis=-1)]
    mod_s = jnp.split(mod[B:], 6, axis=-1)

    pw = _prep_proj_weights(w_in[l], g_q_lora[l], w_q_up[l], g_kv_lora[l], w_uk, g_mla_qn[l], g_mla_qr[l],
                            g_mla_kn[l], g_mla_kr[l], g_dsa_q[l], g_dsa_k[l], g_attn_norm[l])
    pp = _project(x_prompt.reshape(B * T, D), mod_p[0], mod_p[1], pw, _rope_tables(jnp.arange(T)), 256, T)
    ps = _project(x_sample.reshape(ns, D), mod_s[0], mod_s[1], pw,
                  _rope_tables(jnp.tile(past + jnp.arange(TS), DB)), ns, 0)

    r3 = lambda a: a.reshape(B, T, a.shape[-1])
    o_mla_p = _mla_prompt(r3(pp["qall"]), r3(pp["kall"]), r3(pp["ckvb"]), _pad_wuv(w_uv))
    o_dsa_p = _dsa_prompt(rel_bias, r3(pp["qi"]), r3(pp["wib"]), r3(pp["kid"]), r3(pp["qd"]), r3(pp["kdb"]),
                          r3(pp["vdb"]))

    o_mla_s = _mla_sample(page_table, ps["qall"].reshape(ns, MLA_HEADS, LANES), ps["kall"].reshape(ns, 1, -1),
                          ps["ckvb"].reshape(ns, 1, KV_LORA), pw["wuk"], pw["gk"],
                          w_uv.reshape(KV_LORA, MLA_OUT).astype(BF16), cache_mla_latent[l],
                          jnp.transpose(cache_mla_krope[l], (0, 2, 1)))
    heads3 = lambda a: a.reshape(ns, DSA_HEADS, DSA_HEAD_DIM)
    o_dsa_s = _dsa_sample(page_table, rel_bias, ps["qi"].reshape(ns, IDX_HEADS, IDX_DIM),
                          ps["misc"][:, _M_WI:_M_WI + IDX_HEADS].reshape(ns, IDX_HEADS, 1),
                          ps["kid"][:, :IDX_DIM].reshape(ns, 1, IDX_DIM),
                          heads3(ps["qd"]), heads3(ps["kd"]), heads3(ps["vd"]),
                          jnp.transpose(cache_idx_k[l], (0, 2, 1)),
                          jnp.transpose(cache_dsa_k[l], (0, 2, 3, 1)), jnp.transpose(cache_dsa_v[l], (0, 2, 3, 1)))

    moe_w = (g_ffn_norm[l], w_out[l].astype(BF16), w_router[l].T.astype(BF16), b_router[l],
             w_s_gu[l].astype(BF16), w_s_down[l].astype(BF16),
             w_e_gu[l].astype(BF16), w_e_down[l].astype(BF16))
    xp = _moe(x_prompt.reshape(B * T, D), o_mla_p.reshape(B * T, MLA_OUT), o_dsa_p.reshape(B * T, DSA_OUT),
              (mod_p[2], mod_p[3], mod_p[4], mod_p[5]), *moe_w, min(1024, T), T).reshape(B, T, D)
    ns_pad = -(-ns // LANES) * LANES
    pad_rows = lambda a: jnp.pad(a, ((0, ns_pad - ns), (0, 0)))
    xs = _moe(pad_rows(x_sample.reshape(ns, D)), pad_rows(o_mla_s.reshape(ns, MLA_OUT)),
              pad_rows(o_dsa_s.reshape(ns, DSA_OUT)), tuple(pad_rows(mod_s[k]) for k in (2, 3, 4, 5)),
              *moe_w, ns_pad, 0)[:ns].reshape(DB, TS, D)

    def caches(p, nb, nt):
        return (p["ckv"].reshape(1, nb, nt, KV_LORA),
                p["misc"][:, :MLA_ROPE].reshape(1, nb, nt, MLA_ROPE),
                p["kd"].reshape(1, nb, nt, DSA_HEADS, DSA_HEAD_DIM),
                p["vd"].reshape(1, nb, nt, DSA_HEADS, DSA_HEAD_DIM),
                p["misc"][:, _M_KI:_M_KI + IDX_DIM].reshape(1, nb, nt, IDX_DIM))

    return (xp, xs) + caches(pp, B, T) + caches(ps, DB, TS)
```

```python
import functools
import math

import jax
import jax.numpy as jnp
import numpy as np
from jax import lax
from jax.experimental import pallas as pl
from jax.experimental.pallas import tpu as pltpu

F32 = jnp.float32
BF16 = jnp.bfloat16
I32 = jnp.int32

D_MODEL = 1024
PAGE_SIZE = 128
EPS = 1e-6
MLA_HEADS = 8
MLA_NOPE = 64
MLA_ROPE = 32
MLA_V = 64
Q_LORA = 256
KV_LORA = 128
ROPE_BASE = 10000.0
MLA_SCALE = (MLA_NOPE + MLA_ROPE) ** -0.5
DSA_HEADS = 8
DSA_HEAD_DIM = 64
DSA_SCALE = DSA_HEAD_DIM ** -0.5
IDX_HEADS = 8
IDX_DIM = 64
IDX_TOPK_MAX = 256
REL_BUCKETS = 32
REL_MAX_DIST = 128
N_EXPERTS = 64
TOP_K = 6
N_GROUPS = 8
TOPK_GROUPS = 4
D_EXPERT = 256
D_SHARED = 256
ROUTED_SCALE = 2.5
MLA_OUT = MLA_HEADS * MLA_V
DSA_OUT = DSA_HEADS * DSA_HEAD_DIM
IN_SIZES = (Q_LORA, KV_LORA, MLA_ROPE, DSA_OUT, DSA_OUT, DSA_OUT, IDX_HEADS * IDX_DIM, IDX_DIM, IDX_HEADS)

LANES = 128
INT_MIN = -(2 ** 31)
NEG_INF = float("-inf")
VMEM_LIMIT = 56 * 1024 * 1024


def _cparams(sem):
    return pltpu.CompilerParams(dimension_semantics=sem, vmem_limit_bytes=VMEM_LIMIT)


def _split_dot(x, m01, passes=3):
    acc = None
    r = x
    for p in range(passes):
        hi = r.astype(BF16)
        part = jnp.dot(hi, m01, preferred_element_type=F32)
        acc = part if acc is None else acc + part
        if p + 1 < passes:
            r = r - hi.astype(F32)
    return acc


def _group_mean(sq, bmat):
    outs = [_split_dot(sq[:, s * LANES:(s + 1) * LANES], bmat) for s in range(sq.shape[1] // LANES)]
    return outs[0] if len(outs) == 1 else jnp.concatenate(outs, axis=1)


def _rope_slabs(x, cos, sin):
    lane = lax.broadcasted_iota(I32, (x.shape[0], LANES), 1)
    first_half = (lane % MLA_ROPE) < (MLA_ROPE // 2)
    outs = []
    for s in range(x.shape[1] // LANES):
        xs = x[:, s * LANES:(s + 1) * LANES]
        rot = jnp.where(first_half, pltpu.roll(xs, LANES - MLA_ROPE // 2, 1), pltpu.roll(xs, MLA_ROPE // 2, 1))
        outs.append(xs * cos + rot * sin)
    return outs[0] if len(outs) == 1 else jnp.concatenate(outs, axis=1)


def _nt_dot(a, b):
    return lax.dot_general(a, b, (((1,), (1,)), ((), ())), preferred_element_type=F32)


def _tn_dot(a, b):
    return lax.dot_general(a, b, (((0,), (0,)), ((), ())), preferred_element_type=F32)


def _fold_lanes(x, op):
    acc = x[:, :LANES]
    for s in range(1, x.shape[1] // LANES):
        acc = op(acc, x[:, s * LANES:(s + 1) * LANES])
    return acc


def _adaln_kernel(c_ref, w_ref, b_ref, o_ref):
    c = c_ref[...]
    s = (c * jax.nn.sigmoid(c)).astype(BF16)
    o_ref[...] = jnp.dot(s, w_ref[...], preferred_element_type=F32) + b_ref[...]


def _adaln(c, w_bf, b):
    rows = c.shape[0]
    n = w_bf.shape[1]
    tn = 1536
    return pl.pallas_call(
        _adaln_kernel,
        grid=(n // tn,),
        in_specs=[pl.BlockSpec((rows, D_MODEL), lambda j: (0, 0)),
                  pl.BlockSpec((D_MODEL, tn), lambda j: (0, j)),
                  pl.BlockSpec((1, tn), lambda j: (0, j))],
        out_specs=pl.BlockSpec((rows, tn), lambda j: (0, j)),
        out_shape=jax.ShapeDtypeStruct((rows, n), F32),
        compiler_params=_cparams(("arbitrary",)),
        name="adaln",
    )(c, w_bf, b.reshape(1, n))


_C_QLAT = 0
_C_KV = _C_QLAT + Q_LORA
_C_QD = _C_KV + KV_LORA
_C_KD = _C_QD + DSA_OUT
_C_VD = _C_KD + DSA_OUT
_C_QI = _C_VD + DSA_OUT
_C_MISC = _C_QI + IDX_HEADS * IDX_DIM
_C_KIDUP = _C_MISC + LANES
_C_END = _C_KIDUP + LANES
_M_KI = MLA_ROPE
_M_WI = MLA_ROPE + IDX_DIM


def _proj_kernel(x_ref, sh_ref, sc_ref, ga_ref, win_ref, gql_ref, wqu_ref, gkv_ref, wuk_ref,
                 gq_ref, gk_ref, gm_ref, gdq_ref, gdk_ref, cq_ref, sq_ref, cm_ref, sm_ref,
                 bq_ref, b64_ref, bm_ref, ex_ref,
                 qall_ref, kall_ref, ckv_ref, ckvb_ref, misc_ref, qd_ref, kd_ref, kdb_ref, vd_ref, vdb_ref,
                 qi_ref, kid_ref, wib_ref):
    x = x_ref[...]
    xn = x * lax.rsqrt(jnp.mean(x * x, axis=-1, keepdims=True) + EPS) * ga_ref[...]
    h = xn * (1.0 + sc_ref[...]) + sh_ref[...]
    p = jnp.dot(h.astype(BF16), win_ref[...], preferred_element_type=F32)

    ql = p[:, _C_QLAT:_C_KV]
    qln = ql * lax.rsqrt(jnp.mean(ql * ql, axis=-1, keepdims=True) + EPS) * gql_ref[...]
    q = jnp.dot(qln.astype(BF16), wqu_ref[...], preferred_element_type=F32)
    qn = q * lax.rsqrt(_group_mean(q * q, bq_ref[...]) + EPS) * gq_ref[...]
    qall_ref[...] = _rope_slabs(qn, cq_ref[...], sq_ref[...]).astype(BF16)

    kv = p[:, _C_KV:_C_QD]
    ckv = kv * lax.rsqrt(jnp.mean(kv * kv, axis=-1, keepdims=True) + EPS) * gkv_ref[...]
    ckv_ref[...] = ckv
    ckvb = ckv.astype(BF16)
    ckvb_ref[...] = ckvb
    kn = jnp.dot(ckvb, wuk_ref[...], preferred_element_type=F32)
    kn = kn * lax.rsqrt(_group_mean(kn * kn, bq_ref[...]) + EPS) * gk_ref[...]

    m = p[:, _C_MISC:_C_KIDUP]
    lane = lax.broadcasted_iota(I32, m.shape, 1)
    is_kr = lane < MLA_ROPE
    mm = _split_dot(m * m, bm_ref[...])
    mn = jnp.where(is_kr, m * lax.rsqrt(mm + EPS) * gm_ref[...], m)
    mr = _rope_slabs(mn, cm_ref[...], sm_ref[...])
    is_wi = (lane >= _M_WI) & (lane < _M_WI + IDX_HEADS)
    misc = jnp.where(is_wi, mr * (IDX_HEADS ** -0.5), mr)
    misc_ref[...] = misc
    wib_ref[...] = _split_dot(misc, ex_ref[...])
    kr_placed = jnp.where((lane >= MLA_NOPE) & (lane < MLA_NOPE + MLA_ROPE), pltpu.roll(mr, MLA_NOPE, 1), 0.0)
    kall_ref[...] = jnp.concatenate(
        [kn[:, s * LANES:(s + 1) * LANES] + kr_placed for s in range(MLA_HEADS)], axis=1).astype(BF16)

    qd = p[:, _C_QD:_C_KD]
    qd_ref[...] = qd * lax.rsqrt(_group_mean(qd * qd, b64_ref[...]) + EPS) * gdq_ref[...]
    kd = p[:, _C_KD:_C_VD]
    kdn = kd * lax.rsqrt(_group_mean(kd * kd, b64_ref[...]) + EPS) * gdk_ref[...]
    kd_ref[...] = kdn
    kdb_ref[...] = kdn.astype(BF16)
    vd = p[:, _C_VD:_C_QI]
    vd_ref[...] = vd
    vdb_ref[...] = vd.astype(BF16)
    qi_ref[...] = p[:, _C_QI:_C_MISC].astype(BF16)
    kid_ref[...] = p[:, _C_KIDUP:_C_END].astype(BF16)


def _block_mean_matrix(blocks):
    m = np.zeros((LANES, LANES), np.float32)
    for start, size in blocks:
        m[start:start + size, start:start + size] = 1.0 / size
    return jnp.asarray(m, BF16)


def _head_weight_expander():
    assert IDX_DIM == 64
    m = np.zeros((LANES, IDX_HEADS * LANES), np.float32)
    for h in range(IDX_HEADS):
        m[_M_WI + h, h * LANES:(h + 1) * LANES] = IDX_DIM ** -0.5
    return jnp.asarray(m, BF16)


def _prep_proj_weights(w_in, g_q_lora, w_q_up, g_kv_lora, w_uk, g_mla_qn, g_mla_qr, g_mla_kn, g_mla_kr,
                       g_dsa_q, g_dsa_k, g_attn_norm):
    offs = np.cumsum((0,) + IN_SIZES)
    sec = lambda k: w_in[:, offs[k]:offs[k + 1]]
    zeros = lambda n: jnp.zeros((D_MODEL, n), w_in.dtype)
    misc = jnp.concatenate([sec(2), sec(7), sec(8), zeros(LANES - MLA_ROPE - IDX_DIM - IDX_HEADS)], axis=1)
    w_in_r = jnp.concatenate([sec(0), sec(1), sec(3), sec(4), sec(5), sec(6), misc, sec(7), sec(7)], axis=1)
    wq = w_q_up.reshape(Q_LORA, MLA_HEADS, MLA_NOPE + MLA_ROPE)
    wq = jnp.pad(wq, ((0, 0), (0, 0), (0, LANES - MLA_NOPE - MLA_ROPE))).reshape(Q_LORA, MLA_HEADS * LANES)
    wk = jnp.pad(w_uk, ((0, 0), (0, 0), (0, LANES - MLA_NOPE))).reshape(KV_LORA, MLA_HEADS * LANES)
    pad1 = lambda v, n: jnp.pad(v, (0, n - v.shape[0]))
    gq = jnp.tile(pad1(jnp.concatenate([g_mla_qn, g_mla_qr]), LANES), MLA_HEADS).reshape(1, -1)
    gk = jnp.tile(pad1(g_mla_kn, LANES), MLA_HEADS).reshape(1, -1)
    gm = jnp.concatenate([g_mla_kr, jnp.ones((LANES - MLA_ROPE,), F32)]).reshape(1, -1)
    return dict(
        win=w_in_r.astype(BF16), wqu=wq.astype(BF16), wuk=wk.astype(BF16),
        ga=g_attn_norm.reshape(1, -1), gql=g_q_lora.reshape(1, -1), gkv=g_kv_lora.reshape(1, -1),
        gq=gq, gk=gk, gm=gm,
        gdq=jnp.tile(g_dsa_q, DSA_HEADS).reshape(1, -1), gdk=jnp.tile(g_dsa_k, DSA_HEADS).reshape(1, -1),
        bq=_block_mean_matrix([(0, MLA_NOPE), (MLA_NOPE, MLA_ROPE)]),
        b64=_block_mean_matrix([(0, DSA_HEAD_DIM), (DSA_HEAD_DIM, DSA_HEAD_DIM)]),
        bm=_block_mean_matrix([(0, MLA_ROPE)]),
        ex=_head_weight_expander(),
    )


def _rope_tables(pos):
    half = MLA_ROPE // 2
    inv = ROPE_BASE ** (-jnp.arange(half, dtype=F32) / half)
    ang = pos.astype(F32)[:, None] * inv
    cos, sin = jnp.cos(ang), jnp.sin(ang)
    cos32 = jnp.concatenate([cos, cos], axis=1)
    sin32 = jnp.concatenate([-sin, sin], axis=1)
    n = pos.shape[0]
    ones = lambda w: jnp.ones((n, w), F32)
    zeros = lambda w: jnp.zeros((n, w), F32)
    cq = jnp.concatenate([ones(MLA_NOPE), cos32, ones(LANES - MLA_NOPE - MLA_ROPE)], axis=1)
    sq = jnp.concatenate([zeros(MLA_NOPE), sin32, zeros(LANES - MLA_NOPE - MLA_ROPE)], axis=1)
    cm = jnp.concatenate([cos32, ones(LANES - MLA_ROPE)], axis=1)
    sm = jnp.concatenate([sin32, zeros(LANES - MLA_ROPE)], axis=1)
    return cq, sq, cm, sm


def _project(x2d, shift, scale, pw, tables, tm, rows_per_mod):
    n = x2d.shape[0]
    nt = n // tm
    cq, sq, cm, sm = tables
    tpos = cq.shape[0] // tm
    const = lambda shape: pl.BlockSpec(shape, lambda i: (0,) * len(shape))
    row = lambda w: pl.BlockSpec((tm, w), lambda i: (i, 0))
    if rows_per_mod:
        per = rows_per_mod // tm
        mod_spec = pl.BlockSpec((None, 1, D_MODEL), lambda i: (i // per, 0, 0))
    else:
        mod_spec = row(D_MODEL)
    tab = pl.BlockSpec((tm, LANES), lambda i: (i % tpos, 0))
    in_specs = [row(D_MODEL), mod_spec, mod_spec, const((1, D_MODEL)), const((D_MODEL, _C_END)),
                const((1, Q_LORA)), const((Q_LORA, MLA_HEADS * LANES)), const((1, KV_LORA)),
                const((KV_LORA, MLA_HEADS * LANES)), const((1, MLA_HEADS * LANES)), const((1, MLA_HEADS * LANES)),
                const((1, LANES)), const((1, DSA_OUT)), const((1, DSA_OUT)), tab, tab, tab, tab,
                const((LANES, LANES)), const((LANES, LANES)), const((LANES, LANES)),
                const((LANES, IDX_HEADS * LANES))]
    widths = [(MLA_HEADS * LANES, BF16), (MLA_HEADS * LANES, BF16), (KV_LORA, F32), (KV_LORA, BF16), (LANES, F32),
              (DSA_OUT, F32), (DSA_OUT, F32), (DSA_OUT, BF16), (DSA_OUT, F32), (DSA_OUT, BF16),
              (IDX_HEADS * IDX_DIM, BF16), (LANES, BF16), (IDX_HEADS * LANES, F32)]
    outs = pl.pallas_call(
        _proj_kernel,
        grid=(nt,),
        in_specs=in_specs,
        out_specs=[row(w) for w, _ in widths],
        out_shape=[jax.ShapeDtypeStruct((n, w), dt) for w, dt in widths],
        compiler_params=_cparams(("arbitrary",)),
        name="project",
    )(x2d, shift, scale, pw["ga"], pw["win"], pw["gql"], pw["wqu"], pw["gkv"], pw["wuk"],
      pw["gq"], pw["gk"], pw["gm"], pw["gdq"], pw["gdk"], cq, sq, cm, sm, pw["bq"], pw["b64"], pw["bm"],
      pw["ex"])
    names = ("qall", "kall", "ckv", "ckvb", "misc", "qd", "kd", "kdb", "vd", "vdb", "qi", "kid", "wib")
    return dict(zip(names, outs))


_MLA_TQ = 256
_MLA_TK = 512


def _mla_prompt_kernel(q_ref, k_ref, c_ref, wuv_ref, o_ref, s_ref):
    tq, tk = _MLA_TQ, _MLA_TK
    i = pl.program_id(1)
    q0 = i * tq
    nkb = (q0 + tq + tk - 1) // tk
    row = q0 + lax.broadcasted_iota(I32, (tq, tk), 0)
    col0 = lax.broadcasted_iota(I32, (tq, tk), 1)
    heads = []
    for pr in range(MLA_HEADS // 2):
        qhs = [q_ref[:, (2 * pr + half) * LANES:(2 * pr + half + 1) * LANES] for half in range(2)]

        def pass_a(j, mrun, pr=pr, qhs=qhs):
            k0 = pl.multiple_of(j * tk, tk)
            causal = col0 + k0 <= row
            out = []
            for half in range(2):
                h = 2 * pr + half
                s = _nt_dot(qhs[half], k_ref[pl.ds(k0, tk), h * LANES:(h + 1) * LANES]) * MLA_SCALE
                s = jnp.where(causal, s, NEG_INF)
                s_ref[half, :, pl.ds(k0, tk)] = s
                out.append(jnp.maximum(mrun[half], _fold_lanes(s, jnp.maximum)))
            return tuple(out)

        ninf = jnp.full((tq, LANES), NEG_INF, F32)
        mrun = lax.fori_loop(0, nkb, pass_a, (ninf, ninf))
        ms = [jnp.max(m, axis=1, keepdims=True) for m in mrun]

        def pass_b(j, carry, ms=ms):
            k0 = pl.multiple_of(j * tk, tk)
            cb = c_ref[pl.ds(k0, tk), :]
            out = []
            for half in range(2):
                l, acc = carry[half]
                p = jnp.exp(s_ref[half, :, pl.ds(k0, tk)] - ms[half])
                out.append((l + _fold_lanes(p, jnp.add),
                            acc + jnp.dot(p.astype(BF16), cb, preferred_element_type=F32)))
            return tuple(out)

        zero = jnp.zeros((tq, LANES), F32)
        for l, acc in lax.fori_loop(0, nkb, pass_b, ((zero, zero), (zero, zero))):
            heads.append((acc / jnp.sum(l, axis=1, keepdims=True)).astype(BF16))
    for pr in range(MLA_HEADS // 2):
        o = (jnp.dot(heads[2 * pr], wuv_ref[2 * pr], preferred_element_type=F32)
             + jnp.dot(heads[2 * pr + 1], wuv_ref[2 * pr + 1], preferred_element_type=F32))
        o_ref[:, pr * LANES:(pr + 1) * LANES] = o.astype(BF16)


def _pad_wuv(w_uv):
    w = jnp.transpose(w_uv, (1, 0, 2))
    even = jnp.pad(w, ((0, 0), (0, 0), (0, LANES - MLA_V)))
    odd = jnp.pad(w, ((0, 0), (0, 0), (LANES - MLA_V, 0)))
    is_odd = (jnp.arange(MLA_HEADS) % 2 == 1)[:, None, None]
    return jnp.where(is_odd, odd, even).astype(BF16)


def _mla_prompt(qall, kall, ckvb, wuv_pad):
    b, t, _ = qall.shape
    tq = _MLA_TQ
    assert t % _MLA_TK == 0
    return pl.pallas_call(
        _mla_prompt_kernel,
        grid=(b, t // tq),
        in_specs=[pl.BlockSpec((None, tq, MLA_HEADS * LANES), lambda bi, i: (bi, i, 0)),
                  pl.BlockSpec((None, t, MLA_HEADS * LANES), lambda bi, i: (bi, 0, 0)),
                  pl.BlockSpec((None, t, KV_LORA), lambda bi, i: (bi, 0, 0)),
                  pl.BlockSpec((MLA_HEADS, KV_LORA, LANES), lambda bi, i: (0, 0, 0))],
        out_specs=pl.BlockSpec((None, tq, MLA_OUT), lambda bi, i: (bi, i, 0)),
        out_shape=jax.ShapeDtypeStruct((b, t, MLA_OUT), BF16),
        scratch_shapes=[pltpu.VMEM((2, tq, t), F32)],
        compiler_params=_cparams(("arbitrary", "arbitrary")),
        name="mla_prompt",
    )(qall, kall, ckvb, wuv_pad)


_DSA_T = 128
_DSA_CW = 512


def _sortable_key(score):
    bits = pltpu.bitcast(score, I32)
    key = jnp.where(bits < 0, bits ^ jnp.int32(0x7FFFFFFF), bits)
    return jnp.where(score == 0.0, 0, key)


def _bias_tiles(rb_ref, tz_ref):
    t = _DSA_T
    r = lax.broadcasted_iota(I32, (t, t), 0)
    c = lax.broadcasted_iota(I32, (t, t), 1)
    exact = REL_BUCKETS // 2
    for which in range(2):
        d = jnp.maximum(r - c + t * which, 0)
        logd = jnp.log(jnp.maximum(d, 1).astype(F32) / exact) / math.log(REL_MAX_DIST / exact)
        far = jnp.minimum(exact + (logd * (REL_BUCKETS - exact)).astype(I32), REL_BUCKETS - 1)
        bucket = jnp.where(d < exact, d, far)
        for h in range(DSA_HEADS):
            tile = jnp.zeros((t, t), F32)
            for bk in range(REL_BUCKETS):
                tile = jnp.where(bucket == bk, rb_ref[bk, h], tile)
            tz_ref[h, which] = tile


def _select_mask(keys_ref, mask_ref, tri_ref, nch, n_sel, rows):
    cw = _DSA_CW

    def count(pred):
        def body(j, acc):
            kb = keys_ref[:, pl.ds(pl.multiple_of(j * cw, cw), cw)]
            return acc + _fold_lanes(jnp.where(pred(kb), 1.0, 0.0), jnp.add)
        acc = lax.fori_loop(0, nch, body, jnp.zeros((rows, LANES), F32))
        return jnp.sum(acc, axis=1, keepdims=True)

    kf = jnp.float32(n_sel)
    zero = jnp.zeros((rows, 1), I32)
    thr = jnp.where(count(lambda kb: kb >= zero) >= kf, zero, jnp.full((rows, 1), INT_MIN, I32))

    def bit_body(bi, thr):
        cand = thr + lax.shift_left(jnp.int32(1), 30 - bi)
        return jnp.where(count(lambda kb: kb >= cand) >= kf, cand, thr)

    thr = lax.fori_loop(0, 31, bit_body, thr)
    need = kf - count(lambda kb: kb > thr)

    def mask_body(j, carry):
        k0 = pl.multiple_of(j * cw, cw)
        kb = keys_ref[:, pl.ds(k0, cw)]
        eq = kb == thr
        eqf = jnp.where(eq, 1.0, 0.0)
        before = jnp.dot(eqf.astype(BF16), tri_ref[...], preferred_element_type=F32) + carry
        keep = ((kb > thr) | (eq & (before < need))) & (kb > INT_MIN)
        mask_ref[:, pl.ds(k0, cw)] = jnp.where(keep, 0.0, NEG_INF)
        return carry + jnp.sum(eqf, axis=1, keepdims=True)

    lax.fori_loop(0, nch, mask_body, jnp.zeros((rows, 1), F32))


def _dsa_prompt_kernel(rb_ref, qi_ref, wib_ref, kid_ref, qd_ref, kd_ref, vd_ref, tri_ref, o_ref,
                       keys_ref, mask_ref, tz_ref, s_ref, *, n_sel):
    t, cw = _DSA_T, _DSA_CW
    sub_n = cw // t
    bi = pl.program_id(0)
    i = pl.program_id(1)
    nch = i // sub_n + 1

    @pl.when((bi == 0) & (i == 0))
    def _():
        _bias_tiles(rb_ref, tz_ref)

    row = i * t + lax.broadcasted_iota(I32, (t, cw), 0)
    col0 = lax.broadcasted_iota(I32, (t, cw), 1)
    lane = lax.broadcasted_iota(I32, (t, LANES), 1)
    low = lane < DSA_HEAD_DIM

    def head_halves(ref, pr):
        qs = ref[:, pr * LANES:(pr + 1) * LANES].astype(BF16)
        zero = jnp.zeros_like(qs)
        return jnp.where(low, qs, zero), jnp.where(low, zero, qs)

    qim = [q for pr in range(IDX_HEADS // 2) for q in head_halves(qi_ref, pr)]

    def idx_body(c, _):
        k0 = pl.multiple_of(c * cw, cw)
        kk = kid_ref[pl.ds(k0, cw), :]
        acc = jnp.zeros((t, cw), F32)
        for h in range(IDX_HEADS):
            r = jnp.maximum(_nt_dot(qim[h], kk), 0.0)
            w = wib_ref[:, h * LANES:(h + 1) * LANES]
            acc = acc + jnp.concatenate([w] * sub_n, axis=1) * r
        keys_ref[:, pl.ds(k0, cw)] = jnp.where(col0 + k0 <= row, _sortable_key(acc), INT_MIN)
        return 0

    lax.fori_loop(0, nch, idx_body, 0)
    few_keys = (i + 1) * t <= n_sel

    @pl.when(few_keys)
    def _():
        def keep_all(c, _):
            k0 = pl.multiple_of(c * cw, cw)
            mask_ref[:, pl.ds(k0, cw)] = jnp.where(keys_ref[:, pl.ds(k0, cw)] > INT_MIN, 0.0, NEG_INF)
            return 0
        lax.fori_loop(0, nch, keep_all, 0)

    @pl.when(jnp.logical_not(few_keys))
    def _():
        _select_mask(keys_ref, mask_ref, tri_ref, nch, n_sel, t)

    for pr in range(DSA_HEADS // 2):
        qms = head_halves(qd_ref, pr)
        fars = [rb_ref[REL_BUCKETS - 1, 2 * pr + half] for half in range(2)]

        def pass_a(c, mrun, pr=pr, qms=qms, fars=fars):
            k0 = pl.multiple_of(c * cw, cw)
            kb = kd_ref[pl.ds(k0, cw), pr * LANES:(pr + 1) * LANES]
            mk = mask_ref[:, pl.ds(k0, cw)]
            out = []
            for half in range(2):
                h = 2 * pr + half
                s = _nt_dot(qms[half], kb) * DSA_SCALE
                parts = []
                for sub in range(sub_n):
                    blk = c * sub_n + sub
                    bias = jnp.where(blk == i, tz_ref[h, 0], jnp.where(blk == i - 1, tz_ref[h, 1], fars[half]))
                    parts.append(s[:, sub * t:(sub + 1) * t] + bias)
                s = jnp.concatenate(parts, axis=1) + mk
                s_ref[half, :, pl.ds(k0, cw)] = s
                out.append(jnp.maximum(mrun[half], _fold_lanes(s, jnp.maximum)))
            return tuple(out)

        ninf = jnp.full((t, LANES), NEG_INF, F32)
        mrun = lax.fori_loop(0, nch, pass_a, (ninf, ninf))
        ms = [jnp.max(m, axis=1, keepdims=True) for m in mrun]

        def pass_b(c, carry, pr=pr, ms=ms):
            k0 = pl.multiple_of(c * cw, cw)
            vb = vd_ref[pl.ds(k0, cw), pr * LANES:(pr + 1) * LANES]
            out = []
            for half in range(2):
                l, acc = carry[half]
                p = jnp.exp(s_ref[half, :, pl.ds(k0, cw)] - ms[half])
                out.append((l + _fold_lanes(p, jnp.add),
                            acc + jnp.dot(p.astype(BF16), vb, preferred_element_type=F32)))
            return tuple(out)

        zero = jnp.zeros((t, LANES), F32)
        (l0, a0), (l1, a1) = lax.fori_loop(0, nch, pass_b, ((zero, zero), (zero, zero)))
        o0 = a0 / jnp.sum(l0, axis=1, keepdims=True)
        o1 = a1 / jnp.sum(l1, axis=1, keepdims=True)
        o_ref[:, pr * LANES:(pr + 1) * LANES] = jnp.where(low, o0, o1).astype(BF16)


def _strict_upper(n):
    return jnp.asarray(np.triu(np.ones((n, n), np.float32), 1), BF16)


def _dsa_prompt(rel_bias, qi, wib, kid, qd, kdb, vdb):
    b, t, _ = qi.shape
    tq, cw = _DSA_T, _DSA_CW
    assert t % cw == 0
    n_sel = min(IDX_TOPK_MAX, t // 4)
    blk = lambda w: pl.BlockSpec((None, tq, w), lambda bi, i: (bi, i, 0))
    full = lambda w: pl.BlockSpec((None, t, w), lambda bi, i: (bi, 0, 0))
    return pl.pallas_call(
        functools.partial(_dsa_prompt_kernel, n_sel=n_sel),
        grid=(b, t // tq),
        in_specs=[pl.BlockSpec(memory_space=pltpu.SMEM),
                  blk(IDX_HEADS * IDX_DIM), blk(IDX_HEADS * LANES), full(LANES), blk(DSA_OUT), full(DSA_OUT),
                  full(DSA_OUT), pl.BlockSpec((cw, cw), lambda bi, i: (0, 0))],
        out_specs=blk(DSA_OUT),
        out_shape=jax.ShapeDtypeStruct((b, t, DSA_OUT), BF16),
        scratch_shapes=[pltpu.VMEM((tq, t), I32), pltpu.VMEM((tq, t), F32),
                        pltpu.VMEM((DSA_HEADS, 2, tq, tq), F32), pltpu.VMEM((2, tq, t), F32)],
        compiler_params=_cparams(("arbitrary", "arbitrary")),
        name="dsa_prompt",
    )(rel_bias, qi, wib, kid, qd, kdb, vdb, _strict_upper(cw))


_EXPERTS_PER_GROUP = N_EXPERTS // N_GROUPS
_MOE_GROUP = 4


def _first_index_of_max(v, idx, axis, sentinel):
    mx = jnp.max(v, axis=axis, keepdims=True)
    first = jnp.min(jnp.where(v == mx, idx, sentinel), axis=axis, keepdims=True)
    return mx, first


def _route(logits_t, bias_col):
    n_tok = logits_t.shape[1]
    scores = jax.nn.sigmoid(logits_t)
    biased = scores + bias_col
    b3 = biased.reshape(N_GROUPS, _EXPERTS_PER_GROUP, n_tok)
    j3 = lax.broadcasted_iota(I32, b3.shape, 1)
    m1, f1 = _first_index_of_max(b3, j3, 1, _EXPERTS_PER_GROUP)
    m2 = jnp.max(jnp.where(j3 == f1, NEG_INF, b3), axis=1, keepdims=True)
    gs = (m1 + m2).reshape(N_GROUPS, n_tok)
    gi = lax.broadcasted_iota(I32, gs.shape, 0)
    gsel = jnp.zeros(gs.shape, jnp.bool_)
    for _ in range(TOPK_GROUPS):
        _, first = _first_index_of_max(gs, gi, 0, N_GROUPS)
        hit = gi == first
        gsel = gsel | hit
        gs = jnp.where(hit, NEG_INF, gs)
    gsel3 = jnp.broadcast_to(gsel.reshape(N_GROUPS, 1, n_tok), b3.shape)
    masked = jnp.where(gsel3, b3, NEG_INF).reshape(N_EXPERTS, n_tok)
    ei = lax.broadcasted_iota(I32, masked.shape, 0)
    sel = jnp.zeros(masked.shape, jnp.bool_)
    for _ in range(TOP_K):
        _, first = _first_index_of_max(masked, ei, 0, N_EXPERTS)
        hit = ei == first
        sel = sel | hit
        masked = jnp.where(hit, NEG_INF, masked)
    w = jnp.where(sel, scores, 0.0)
    gate = w / jnp.sum(w, axis=0, keepdims=True) * ROUTED_SCALE
    return sel, gate


def _swiglu_bf(x_bf, wgu_ref, wdown_ref, d_hidden):
    gu = jnp.dot(x_bf, wgu_ref[...], preferred_element_type=F32)
    g, u = gu[:, :d_hidden], gu[:, d_hidden:]
    act = (g * jax.nn.sigmoid(g)) * u
    return jnp.dot(act.astype(BF16), wdown_ref[...], preferred_element_type=F32)


def _moe_kernel(x_ref, oa_ref, ob_ref, ga_ref, sf_ref, cf_ref, gf_ref, gn_ref, wo_ref, wr_ref, br_ref,
                wsg_ref, wsd_ref, tri_ref, weg_ref, wed_ref, y_ref, xt_ref, acc_ref, gate_ref, rank_ref,
                *, cap):
    tm = x_ref.shape[0]
    e = pl.program_id(1)

    @pl.when(e == 0)
    def _():
        half = oa_ref.shape[1]
        attn = (jnp.dot(oa_ref[...], wo_ref[:half, :], preferred_element_type=F32)
                + jnp.dot(ob_ref[...], wo_ref[half:, :], preferred_element_type=F32))
        x1 = x_ref[...] + ga_ref[...] * attn
        y_ref[...] = x1
        hn = x1 * lax.rsqrt(jnp.mean(x1 * x1, axis=-1, keepdims=True) + EPS) * gn_ref[...]
        xt = (hn * (1.0 + cf_ref[...]) + sf_ref[...]).astype(BF16)
        xt_ref[...] = xt
        sel, gate = _route(_nt_dot(wr_ref[...], xt), br_ref[...])
        ind = jnp.where(sel, 1.0, 0.0)
        before = jnp.dot(ind.astype(BF16), tri_ref[...], preferred_element_type=F32)
        rank_ref[...] = jnp.where(sel, before, -1.0)
        gate_ref[...] = jnp.where(sel, gate, 0.0)
        acc_ref[...] = jnp.zeros_like(acc_ref)

    grp = _MOE_GROUP
    rank_rows = [rank_ref[pl.ds(e * grp + j, 1), :] for j in range(grp)]
    gate_rows = [gate_ref[pl.ds(e * grp + j, 1), :] for j in range(grp)]
    top = rank_rows[0]
    for j in range(1, grp):
        top = jnp.maximum(top, rank_rows[j])
    count = (jnp.max(top) + 1.0).astype(I32)
    n_chunks = (count + cap - 1) // cap
    slot = lax.broadcasted_iota(I32, (cap, tm), 0).astype(F32)

    def chunk(c, _):
        base = slot + (c * cap).astype(F32)
        picks = [jnp.where(base == rank_rows[j], 1.0, 0.0) for j in range(grp)]
        pick_b = jnp.concatenate(picks, axis=0).astype(BF16)
        xe = jnp.dot(pick_b, xt_ref[...], preferred_element_type=F32).astype(BF16)
        ys = []
        for j in range(grp):
            ye = _swiglu_bf(xe[j * cap:(j + 1) * cap], weg_ref.at[j], wed_ref.at[j], D_EXPERT)
            ge = jnp.sum(picks[j] * gate_rows[j], axis=1, keepdims=True)
            ys.append((ye * ge).astype(BF16))
        acc_ref[...] += _tn_dot(pick_b, jnp.concatenate(ys, axis=0))
        return 0

    lax.fori_loop(0, n_chunks, chunk, 0)

    @pl.when(e == N_EXPERTS // grp - 1)
    def _():
        shared = _swiglu_bf(xt_ref[...], wsg_ref, wsd_ref, D_SHARED)
        y_ref[...] = y_ref[...] + gf_ref[...] * (acc_ref[...] + shared)


def _moe(x2d, oa, ob, mods, g_ffn, wo_bf, wr_t_bf, b_router, wsg_bf, wsd_bf, weg_bf, wed_bf, tm, rows_per_mod):
    n = x2d.shape[0]
    assert n % tm == 0 and tm % LANES == 0
    nt = n // tm
    cap = LANES
    const = lambda shape: pl.BlockSpec(shape, lambda t, e: (0,) * len(shape))
    row = lambda w: pl.BlockSpec((tm, w), lambda t, e: (t, 0))
    if rows_per_mod:
        per = rows_per_mod // tm
        mod_spec = pl.BlockSpec((None, 1, D_MODEL), lambda t, e: (t // per, 0, 0))
    else:
        mod_spec = row(D_MODEL)
    half = oa.shape[1]
    return pl.pallas_call(
        functools.partial(_moe_kernel, cap=cap),
        grid=(nt, N_EXPERTS // _MOE_GROUP),
        in_specs=[row(D_MODEL), row(half), row(half), mod_spec, mod_spec, mod_spec, mod_spec,
                  const((1, D_MODEL)), const((2 * half, D_MODEL)), const((N_EXPERTS, D_MODEL)),
                  const((N_EXPERTS, 1)), const((D_MODEL, 2 * D_SHARED)), const((D_SHARED, D_MODEL)),
                  const((tm, tm)),
                  pl.BlockSpec((_MOE_GROUP, D_MODEL, 2 * D_EXPERT), lambda t, e: (e, 0, 0)),
                  pl.BlockSpec((_MOE_GROUP, D_EXPERT, D_MODEL), lambda t, e: (e, 0, 0))],
        out_specs=row(D_MODEL),
        out_shape=jax.ShapeDtypeStruct((n, D_MODEL), F32),
        scratch_shapes=[pltpu.VMEM((tm, D_MODEL), BF16), pltpu.VMEM((tm, D_MODEL), F32),
                        pltpu.VMEM((N_EXPERTS, tm), F32), pltpu.VMEM((N_EXPERTS, tm), F32)],
        compiler_params=_cparams(("arbitrary", "arbitrary")),
        name="moe",
    )(x2d, oa, ob, *mods, g_ffn.reshape(1, -1), wo_bf, wr_t_bf, b_router.reshape(-1, 1), wsg_bf, wsd_bf,
      _strict_upper(tm), weg_bf, wed_bf)


def _col_blocks(q):
    nh = q.shape[0]
    r = lax.broadcasted_iota(I32, (nh, LANES), 0)
    c = lax.broadcasted_iota(I32, (nh, LANES), 1)
    blocks = [_tn_dot(q, jnp.where((r == h) & (c == h), 1.0, 0.0).astype(q.dtype)) for h in range(nh)]
    return jnp.concatenate(blocks, axis=0)


def _rows_to_cols(row):
    r8 = jnp.broadcast_to(row, (8, row.shape[1]))
    e0 = jnp.where(lax.broadcasted_iota(I32, (8, LANES), 0) == 0, 1.0, 0.0).astype(row.dtype)
    return _tn_dot(r8, e0)


def _head_diag(acc, width):
    r = lax.broadcasted_iota(I32, acc.shape, 0)
    c = lax.broadcasted_iota(I32, acc.shape, 1)
    return jnp.sum(jnp.where(r == c // width, acc, 0.0), axis=0, keepdims=True)


class _PageStream:
    def __init__(self, hbm_ref, buf_ref, sem_ref, pt_ref, seq, pages_per_chunk):
        self.hbm, self.buf, self.sem, self.pt, self.seq, self.ppc = hbm_ref, buf_ref, sem_ref, pt_ref, seq, pages_per_chunk

    def _copy(self, page, slot, r):
        return pltpu.make_async_copy(self.hbm.at[page], self.buf.at[slot, r], self.sem.at[slot])

    def start(self, chunk, slot):
        for r in range(self.ppc):
            self._copy(self.pt[self.seq, chunk * self.ppc + r], slot, r).start()

    def wait(self, slot):
        for r in range(self.ppc):
            self._copy(0, slot, r).wait()


_STREAM_DEPTH = 8


def _stream_loop(streams, n_chunks, body):
    depth = _STREAM_DEPTH
    for d in range(min(depth - 1, n_chunks)):
        for st in streams:
            st.start(d, d)

    def step(c, _):
        slot = c % depth
        nxt = c + depth - 1

        @pl.when(nxt < n_chunks)
        def _():
            for st in streams:
                st.start(nxt, nxt % depth)

        for st in streams:
            st.wait(slot)
        body(c, slot)
        return 0

    lax.fori_loop(0, n_chunks, step, 0)


def _softmax_stats(sc_ref, n_rows, blk):
    nb = n_rows // blk
    tail = n_rows - nb * blk

    def mx_body(j, m):
        return jnp.maximum(m, jnp.max(sc_ref[pl.ds(pl.multiple_of(j * blk, blk), blk), :], axis=0, keepdims=True))

    m = lax.fori_loop(0, nb, mx_body, jnp.full((1, LANES), NEG_INF, F32))
    if tail:
        m = jnp.maximum(m, jnp.max(sc_ref[pl.ds(nb * blk, tail), :], axis=0, keepdims=True))

    def sum_body(j, l):
        return l + jnp.sum(jnp.exp(sc_ref[pl.ds(pl.multiple_of(j * blk, blk), blk), :] - m), axis=0, keepdims=True)

    l = lax.fori_loop(0, nb, sum_body, jnp.zeros((1, LANES), F32))
    if tail:
        l = l + jnp.sum(jnp.exp(sc_ref[pl.ds(nb * blk, tail), :] - m), axis=0, keepdims=True)
    return m, l


_MLA_S_PAGES = 8


def _mla_sample_kernel(pt_ref, q_ref, knew_ref, cnew_ref, wuk_ref, gk_ref, wuv_ref, lat_hbm, kr_hbm, o_ref,
                       latbuf, krbuf, sems, sc_ref, latbf_ref, *, n_pages):
    s_id = pl.program_id(0)
    ppc = math.gcd(n_pages, _MLA_S_PAGES)
    ck = ppc * PAGE_SIZE
    n_chunks = n_pages // ppc
    past = n_pages * PAGE_SIZE
    row8 = lax.broadcasted_iota(I32, (8, LANES), 0)

    qblk = _col_blocks(q_ref[...]).astype(BF16)
    qr = qblk[MLA_NOPE:MLA_NOPE + MLA_ROPE, :].astype(F32)
    for h in range(1, MLA_HEADS):
        qr = qr + qblk[h * LANES + MLA_NOPE:h * LANES + MLA_NOPE + MLA_ROPE, :].astype(F32)
    qr = qr.astype(BF16)
    lat_stream = _PageStream(lat_hbm, latbuf, sems.at[0], pt_ref, s_id, ppc)
    kr_stream = _PageStream(kr_hbm, krbuf, sems.at[1], pt_ref, s_id, ppc)

    def score_chunk(c, slot):
        lat = latbuf[slot].reshape(ck, KV_LORA).astype(BF16)
        latbf_ref[pl.ds(pl.multiple_of(c * ck, ck), ck), :] = lat
        kraw = jnp.dot(lat, wuk_ref[...], preferred_element_type=F32)
        slabs = []
        for h in range(MLA_HEADS):
            x = kraw[:, h * LANES:(h + 1) * LANES]
            ms = jnp.sum(x * x, axis=1, keepdims=True) * (1.0 / MLA_NOPE)
            slabs.append((x * lax.rsqrt(ms + EPS) * gk_ref[:, h * LANES:(h + 1) * LANES]).astype(BF16))
        kn = jnp.concatenate(slabs, axis=1)
        s_rope = jnp.concatenate([_tn_dot(krbuf[slot, r].astype(BF16), qr) for r in range(ppc)], axis=0)
        s = (jnp.dot(kn, qblk, preferred_element_type=F32) + s_rope) * MLA_SCALE
        sc_ref[pl.ds(pl.multiple_of(c * ck, ck), ck), :] = s

    _stream_loop((lat_stream, kr_stream), n_chunks, score_chunk)
    s_new = jnp.dot(jnp.broadcast_to(knew_ref[...], (8, knew_ref.shape[1])), qblk,
                    preferred_element_type=F32) * MLA_SCALE
    sc_ref[pl.ds(past, 8), :] = jnp.where(row8 == 0, s_new, NEG_INF)
    m, l = _softmax_stats(sc_ref, past + 8, ck)

    def pv_chunk(c, acc):
        k0 = pl.multiple_of(c * ck, ck)
        p = jnp.exp(sc_ref[pl.ds(k0, ck), :] - m) / l
        return acc + _tn_dot(p.astype(BF16), latbf_ref[pl.ds(k0, ck), :])

    acc = lax.fori_loop(0, n_chunks, pv_chunk, jnp.zeros((LANES, KV_LORA), F32))
    p_new = jnp.exp(sc_ref[pl.ds(past, 8), :] - m) / l
    acc = acc + _tn_dot(p_new.astype(BF16), jnp.broadcast_to(cnew_ref[...], (8, KV_LORA)))
    out = jnp.dot(acc.astype(BF16), wuv_ref[...], preferred_element_type=F32)
    o_ref[...] = _head_diag(out, MLA_V).astype(BF16)


def _mla_sample(page_table, q8, knew, cnew, wuk_pad, gk, wuv_flat, cache_lat, cache_kr):
    db, n_pages = page_table.shape
    ppc = math.gcd(n_pages, _MLA_S_PAGES)
    past = n_pages * PAGE_SIZE
    per_seq = lambda shape: pl.BlockSpec((None,) + shape, lambda s, pt: (s,) + (0,) * len(shape))
    const = lambda shape: pl.BlockSpec(shape, lambda s, pt: (0,) * len(shape))
    grid_spec = pltpu.PrefetchScalarGridSpec(
        num_scalar_prefetch=1, grid=(db,),
        in_specs=[per_seq((MLA_HEADS, LANES)), per_seq((1, MLA_HEADS * LANES)), per_seq((1, KV_LORA)),
                  const((KV_LORA, MLA_HEADS * LANES)), const((1, MLA_HEADS * LANES)), const((KV_LORA, MLA_OUT)),
                  pl.BlockSpec(memory_space=pl.ANY), pl.BlockSpec(memory_space=pl.ANY)],
        out_specs=per_seq((1, MLA_OUT)),
        scratch_shapes=[pltpu.VMEM((_STREAM_DEPTH, ppc, PAGE_SIZE, KV_LORA), F32),
                        pltpu.VMEM((_STREAM_DEPTH, ppc, MLA_ROPE, PAGE_SIZE), F32),
                        pltpu.SemaphoreType.DMA((2, _STREAM_DEPTH)), pltpu.VMEM((past + 8, LANES), F32),
                        pltpu.VMEM((past, KV_LORA), BF16)])
    return pl.pallas_call(
        functools.partial(_mla_sample_kernel, n_pages=n_pages),
        grid_spec=grid_spec,
        out_shape=jax.ShapeDtypeStruct((db, 1, MLA_OUT), BF16),
        compiler_params=_cparams(("arbitrary",)),
        name="mla_sample",
    )(page_table, q8, knew, cnew, wuk_pad, gk, wuv_flat, cache_lat, cache_kr)


_DSA_S_IDX_PAGES = 16


def _select_flat(keys, tri_u, tri_l, n_sel):
    def count(pred):
        c = jnp.sum(jnp.where(pred, 1.0, 0.0), axis=1, keepdims=True)
        return jnp.sum(c, axis=0, keepdims=True)

    kf = jnp.float32(n_sel)
    zero = jnp.zeros((1, 1), I32)
    thr = jnp.where(count(keys >= zero) >= kf, zero, jnp.full((1, 1), INT_MIN, I32))

    def bit_body(bi, thr):
        cand = thr + lax.shift_left(jnp.int32(1), 30 - bi)
        return jnp.where(count(keys >= cand) >= kf, cand, thr)

    thr = lax.fori_loop(0, 31, bit_body, thr)
    need = kf - count(keys > thr)
    eq = keys == thr
    eqf = jnp.where(eq, 1.0, 0.0)
    within = jnp.dot(eqf.astype(BF16), tri_u, preferred_element_type=F32)
    rowcount = jnp.broadcast_to(jnp.sum(eqf, axis=1, keepdims=True), eqf.shape)
    carry = jnp.dot(tri_l, rowcount.astype(BF16), preferred_element_type=F32)
    keep = ((keys > thr) | (eq & (within + carry < need))) & (keys > INT_MIN)
    return jnp.where(keep, 1.0, 0.0)


_DSA_S_KV_PAGES = 4


def _round_bf16(x):
    return x.astype(BF16).astype(F32)


def _dsa_sample_kernel(pt_ref, rbt_ref, qi_ref, wi_ref, kinew_ref, qd_ref, kdnew_ref, vdnew_ref,
                       triu_ref, tril_ref, idx_hbm, k_hbm, v_hbm, o_ref,
                       idxbuf, kvbuf, sems, keys_ref, mask_ref, s_ref, qcol_ref, acc_ref, *, n_pages, n_sel):
    s_id = pl.program_id(0)
    ipc = math.gcd(n_pages, _DSA_S_IDX_PAGES)
    kpc = math.gcd(n_pages, _DSA_S_KV_PAGES)
    lane1 = lax.broadcasted_iota(I32, (1, LANES), 1)
    qi = qi_ref[...]
    wi = _round_bf16(wi_ref[...])

    keys_ref[...] = jnp.full(keys_ref.shape, INT_MIN, I32)
    idx_stream = _PageStream(idx_hbm, idxbuf, sems.at[0], pt_ref, s_id, ipc)

    def idx_chunk(c, slot):
        for r in range(ipc):
            kk = idxbuf[slot, r].astype(BF16)
            rr = jnp.maximum(jnp.dot(qi, kk, preferred_element_type=F32) * (IDX_DIM ** -0.5), 0.0)
            sc = jnp.sum(wi * _round_bf16(rr), axis=0, keepdims=True)
            keys_ref[pl.ds(c * ipc + r, 1), :] = _sortable_key(sc)

    _stream_loop((idx_stream,), n_pages // ipc, idx_chunk)
    r_new = jnp.maximum(jnp.sum(qi.astype(F32) * kinew_ref[...].astype(F32), axis=1, keepdims=True)
                        * (IDX_DIM ** -0.5), 0.0)
    sc_new = jnp.sum(wi * _round_bf16(r_new), axis=0, keepdims=True)
    keys_ref[pl.ds(n_pages, 1), :] = jnp.where(lane1 == 0, _sortable_key(jnp.broadcast_to(sc_new, (1, LANES))),
                                               INT_MIN)

    mask_ref[...] = _select_flat(keys_ref[...], triu_ref[...], tril_ref[...], n_sel)

    qd = qd_ref[...]
    row8 = lax.broadcasted_iota(I32, (8, LANES), 0)
    e0 = jnp.where(row8 == 0, 1.0, 0.0).astype(BF16)
    for h in range(DSA_HEADS):
        resid = jnp.broadcast_to(qd[h:h + 1, :], (8, DSA_HEAD_DIM))
        col = jnp.zeros((DSA_HEAD_DIM, LANES), F32)
        for _ in range(3):
            piece = resid.astype(BF16)
            col = col + _tn_dot(piece, e0)
            resid = resid - piece.astype(F32)
        qcol_ref[h] = col

    far_col = rbt_ref[:, REL_BUCKETS - 1:REL_BUCKETS]
    exact = REL_BUCKETS // 2
    d_row = PAGE_SIZE - lane1
    logd = jnp.log(jnp.maximum(d_row, 1).astype(F32) / exact) / math.log(REL_MAX_DIST / exact)
    far_b = jnp.minimum(exact + (logd * (REL_BUCKETS - exact)).astype(I32), REL_BUCKETS - 1)
    bucket = jnp.where(d_row < exact, d_row, far_b)
    near = jnp.zeros((DSA_HEADS, LANES), F32)
    for bk in range(REL_BUCKETS):
        near = jnp.where(bucket == bk, rbt_ref[:, bk:bk + 1], near)

    k_stream = _PageStream(k_hbm, kvbuf, sems.at[1], pt_ref, s_id, kpc)
    v_stream = _PageStream(v_hbm, kvbuf, sems.at[1], pt_ref, s_id, kpc)

    def k_chunk(c, slot):
        for r in range(kpc):
            page = c * kpc + r
            for h in range(DSA_HEADS):
                s_ref[page, h:h + 1, :] = jnp.sum(kvbuf[slot, r, h] * qcol_ref[h], axis=0, keepdims=True)
            bias = jnp.where(page == n_pages - 1, near, far_col)
            keep = mask_ref[pl.ds(page, 1), :] > 0.5
            s_ref[page] = jnp.where(keep, s_ref[page] * DSA_SCALE + bias, NEG_INF)

    _stream_loop((k_stream,), n_pages // kpc, k_chunk)
    s_new = (jnp.sum(qd * kdnew_ref[...], axis=1, keepdims=True) * DSA_SCALE
             + rbt_ref[:, 0:1])
    keep_new = mask_ref[pl.ds(n_pages, 1), :][:, 0:1] > 0.5
    s_new = jnp.where(keep_new, s_new, NEG_INF)

    blk = 8
    def mx_body(j, m):
        return jnp.maximum(m, jnp.max(s_ref[pl.ds(pl.multiple_of(j * blk, blk), blk)], axis=0))

    m_t = lax.fori_loop(0, n_pages // blk, mx_body, jnp.full((DSA_HEADS, LANES), NEG_INF, F32))
    m = jnp.maximum(jnp.max(m_t, axis=1, keepdims=True), s_new)

    def sum_body(j, l):
        return l + jnp.sum(jnp.exp(s_ref[pl.ds(pl.multiple_of(j * blk, blk), blk)] - m), axis=0)

    l_t = lax.fori_loop(0, n_pages // blk, sum_body, jnp.zeros((DSA_HEADS, LANES), F32))
    l = jnp.sum(l_t, axis=1, keepdims=True) + jnp.exp(s_new - m)

    acc_ref[...] = jnp.zeros_like(acc_ref)

    def v_chunk(c, slot):
        for r in range(kpc):
            page = c * kpc + r
            p = jnp.exp(s_ref[page] - m) / l
            for h in range(DSA_HEADS):
                acc_ref[h] += p[h:h + 1, :] * kvbuf[slot, r, h]

    _stream_loop((v_stream,), n_pages // kpc, v_chunk)
    ones8 = jnp.ones((8, LANES), BF16)
    rowh = lax.broadcasted_iota(I32, (DSA_HEADS, DSA_HEAD_DIM), 0)
    out = jnp.zeros((DSA_HEADS, DSA_HEAD_DIM), F32)
    for h in range(DSA_HEADS):
        resid = acc_ref[h]
        tot = jnp.zeros((8, DSA_HEAD_DIM), F32)
        for _ in range(3):
            piece = resid.astype(BF16)
            tot = tot + _nt_dot(ones8, piece)
            resid = resid - piece.astype(F32)
        out = jnp.where(rowh == h, tot, out)
    p_new = jnp.exp(s_new - m) / l
    o_ref[...] = (out + p_new * vdnew_ref[...]).astype(BF16)


def _dsa_sample(page_table, rel_bias, qi8, wi8, kinew, qd8, kdnew, vdnew, cache_idx_t, cache_k_t, cache_v_t):
    db, n_pages = page_table.shape
    n_sel = min(IDX_TOPK_MAX, (n_pages * PAGE_SIZE + 1) // 4)
    assert n_pages % 8 == 0
    ipc = math.gcd(n_pages, _DSA_S_IDX_PAGES)
    kpc = math.gcd(n_pages, _DSA_S_KV_PAGES)
    rows = -(-(n_pages + 1) // LANES) * LANES
    tri_l = jnp.asarray(np.tril(np.ones((rows, rows), np.float32), -1), BF16)
    per_seq = lambda shape: pl.BlockSpec((None,) + shape, lambda s, pt: (s,) + (0,) * len(shape))
    const = lambda shape: pl.BlockSpec(shape, lambda s, pt: (0,) * len(shape))
    any_spec = pl.BlockSpec(memory_space=pl.ANY)
    head_tile = (DSA_HEADS, DSA_HEAD_DIM)
    page_tile = (DSA_HEADS, DSA_HEAD_DIM, PAGE_SIZE)
    grid_spec = pltpu.PrefetchScalarGridSpec(
        num_scalar_prefetch=1, grid=(db,),
        in_specs=[const((DSA_HEADS, REL_BUCKETS)),
                  per_seq((IDX_HEADS, IDX_DIM)), per_seq((IDX_HEADS, 1)), per_seq((1, IDX_DIM)),
                  per_seq(head_tile), per_seq(head_tile), per_seq(head_tile),
                  const((LANES, LANES)), const((rows, rows)), any_spec, any_spec, any_spec],
        out_specs=per_seq(head_tile),
        scratch_shapes=[pltpu.VMEM((_STREAM_DEPTH, ipc, IDX_DIM, PAGE_SIZE), F32),
                        pltpu.VMEM((_STREAM_DEPTH, kpc) + page_tile, F32),
                        pltpu.SemaphoreType.DMA((2, _STREAM_DEPTH)),
                        pltpu.VMEM((rows, LANES), I32), pltpu.VMEM((rows, LANES), F32),
                        pltpu.VMEM((n_pages, DSA_HEADS, LANES), F32),
                        pltpu.VMEM(page_tile, F32), pltpu.VMEM(page_tile, F32)])
    return pl.pallas_call(
        functools.partial(_dsa_sample_kernel, n_pages=n_pages, n_sel=n_sel),
        grid_spec=grid_spec,
        out_shape=jax.ShapeDtypeStruct((db,) + head_tile, BF16),
        compiler_params=_cparams(("arbitrary",)),
        name="dsa_sample",
    )(page_table, rel_bias.T, qi8, wi8, kinew, qd8, kdnew, vdnew, _strict_upper(LANES), tri_l,
      cache_idx_t, cache_k_t, cache_v_t)


def kernel(x_prompt, x_sample, cache_mla_latent, cache_mla_krope, cache_dsa_k, cache_dsa_v, cache_idx_k, page_table, c_prompt, c_sample, rel_bias, w_ada, b_ada, g_attn_norm, w_in, g_q_lora, w_q_up, g_kv_lora, w_kv_up, g_mla_qn, g_mla_qr, g_mla_kn, g_mla_kr, g_dsa_q, g_dsa_k, w_out, g_ffn_norm, w_router, b_router, w_e_gu, w_e_down, w_s_gu, w_s_down):
    depth = w_ada.shape[0]
    assert depth == 1, "single-layer trunk"
    l = 0
    B, T, D = x_prompt.shape
    DB, TS, _ = x_sample.shape
    assert TS == 1, "one new token per sampled sequence"
    ns = DB * TS
    past = page_table.shape[1] * PAGE_SIZE

    w_kv = w_kv_up[l].reshape(KV_LORA, MLA_HEADS, MLA_NOPE + MLA_V)
    w_uk, w_uv = w_kv[..., :MLA_NOPE], w_kv[..., MLA_NOPE:]

    mod = _adaln(jnp.concatenate([c_prompt, c_sample], axis=0), w_ada[l].astype(BF16), b_ada[l])
    mod_p = [m.reshape(B, 1, D) for m in jnp.split(mod[:B], 6, axis=-1)]
    mod_s = jnp.split(mod[B:], 6, axis=-1)

    pw = _prep_proj_weights(w_in[l], g_q_lora[l], w_q_up[l], g_kv_lora[l], w_uk, g_mla_qn[l], g_mla_qr[l],
                            g_mla_kn[l], g_mla_kr[l], g_dsa_q[l], g_dsa_k[l], g_attn_norm[l])
    pp = _project(x_prompt.reshape(B * T, D), mod_p[0], mod_p[1], pw, _rope_tables(jnp.arange(T)), 256, T)
    ps = _project(x_sample.reshape(ns, D), mod_s[0], mod_s[1], pw,
                  _rope_tables(jnp.tile(past + jnp.arange(TS), DB)), ns, 0)

    r3 = lambda a: a.reshape(B, T, a.shape[-1])
    o_mla_p = _mla_prompt(r3(pp["qall"]), r3(pp["kall"]), r3(pp["ckvb"]), _pad_wuv(w_uv))
    o_dsa_p = _dsa_prompt(rel_bias, r3(pp["qi"]), r3(pp["wib"]), r3(pp["kid"]), r3(pp["qd"]), r3(pp["kdb"]),
                          r3(pp["vdb"]))

    o_mla_s = _mla_sample(page_table, ps["qall"].reshape(ns, MLA_HEADS, LANES), ps["kall"].reshape(ns, 1, -1),
                          ps["ckvb"].reshape(ns, 1, KV_LORA), pw["wuk"], pw["gk"],
                          w_uv.reshape(KV_LORA, MLA_OUT).astype(BF16), cache_mla_latent[l],
                          jnp.transpose(cache_mla_krope[l], (0, 2, 1)))
    heads3 = lambda a: a.reshape(ns, DSA_HEADS, DSA_HEAD_DIM)
    o_dsa_s = _dsa_sample(page_table, rel_bias, ps["qi"].reshape(ns, IDX_HEADS, IDX_DIM),
                          ps["misc"][:, _M_WI:_M_WI + IDX_HEADS].reshape(ns, IDX_HEADS, 1),
                          ps["kid"][:, :IDX_DIM].reshape(ns, 1, IDX_DIM),
                          heads3(ps["qd"]), heads3(ps["kd"]), heads3(ps["vd"]),
                          jnp.transpose(cache_idx_k[l], (0, 2, 1)),
                          jnp.transpose(cache_dsa_k[l], (0, 2, 3, 1)), jnp.transpose(cache_dsa_v[l], (0, 2, 3, 1)))

    moe_w = (g_ffn_norm[l], w_out[l].astype(BF16), w_router[l].T.astype(BF16), b_router[l],
             w_s_gu[l].astype(BF16), w_s_down[l].astype(BF16),
             w_e_gu[l].astype(BF16), w_e_down[l].astype(BF16))
    xp = _moe(x_prompt.reshape(B * T, D), o_mla_p.reshape(B * T, MLA_OUT), o_dsa_p.reshape(B * T, DSA_OUT),
              (mod_p[2], mod_p[3], mod_p[4], mod_p[5]), *moe_w, min(1024, T), T).reshape(B, T, D)
    ns_pad = -(-ns // LANES) * LANES
    pad_rows = lambda a: jnp.pad(a, ((0, ns_pad - ns), (0, 0)))
    xs = _moe(pad_rows(x_sample.reshape(ns, D)), pad_rows(o_mla_s.reshape(ns, MLA_OUT)),
              pad_rows(o_dsa_s.reshape(ns, DSA_OUT)), tuple(pad_rows(mod_s[k]) for k in (2, 3, 4, 5)),
              *moe_w, ns_pad, 0)[:ns].reshape(DB, TS, D)

    def caches(p, nb, nt):
        return (p["ckv"].reshape(1, nb, nt, KV_LORA),
                p["misc"][:, :MLA_ROPE].reshape(1, nb, nt, MLA_ROPE),
                p["kd"].reshape(1, nb, nt, DSA_HEADS, DSA_HEAD_DIM),
                p["vd"].reshape(1, nb, nt, DSA_HEADS, DSA_HEAD_DIM),
                p["misc"][:, _M_KI:_M_KI + IDX_DIM].reshape(1, nb, nt, IDX_DIM))

    return (xp, xs) + caches(pp, B, T) + caches(ps, DB, TS)
```

```python
import functools
import math

import jax
import jax.numpy as jnp
import numpy as np
from jax import lax
from jax.experimental import pallas as pl
from jax.experimental.pallas import tpu as pltpu

F32 = jnp.float32
BF16 = jnp.bfloat16
I32 = jnp.int32

D_MODEL = 1024
PAGE_SIZE = 128
EPS = 1e-6
MLA_HEADS = 8
MLA_NOPE = 64
MLA_ROPE = 32
MLA_V = 64
Q_LORA = 256
KV_LORA = 128
ROPE_BASE = 10000.0
MLA_SCALE = (MLA_NOPE + MLA_ROPE) ** -0.5
DSA_HEADS = 8
DSA_HEAD_DIM = 64
DSA_SCALE = DSA_HEAD_DIM ** -0.5
IDX_HEADS = 8
IDX_DIM = 64
IDX_TOPK_MAX = 256
REL_BUCKETS = 32
REL_MAX_DIST = 128
N_EXPERTS = 64
TOP_K = 6
N_GROUPS = 8
TOPK_GROUPS = 4
D_EXPERT = 256
D_SHARED = 256
ROUTED_SCALE = 2.5
MLA_OUT = MLA_HEADS * MLA_V
DSA_OUT = DSA_HEADS * DSA_HEAD_DIM
IN_SIZES = (Q_LORA, KV_LORA, MLA_ROPE, DSA_OUT, DSA_OUT, DSA_OUT, IDX_HEADS * IDX_DIM, IDX_DIM, IDX_HEADS)

LANES = 128
INT_MIN = -(2 ** 31)
NEG_INF = float("-inf")
VMEM_LIMIT = 56 * 1024 * 1024


def _cparams(sem):
    return pltpu.CompilerParams(dimension_semantics=sem, vmem_limit_bytes=VMEM_LIMIT)


def _split_dot(x, m01, passes=3):
    acc = None
    r = x
    for p in range(passes):
        hi = r.astype(BF16)
        part = jnp.dot(hi, m01, preferred_element_type=F32)
        acc = part if acc is None else acc + part
        if p + 1 < passes:
            r = r - hi.astype(F32)
    return acc


def _group_mean(sq, bmat):
    outs = [_split_dot(sq[:, s * LANES:(s + 1) * LANES], bmat) for s in range(sq.shape[1] // LANES)]
    return outs[0] if len(outs) == 1 else jnp.concatenate(outs, axis=1)


def _rope_slabs(x, cos, sin):
    lane = lax.broadcasted_iota(I32, (x.shape[0], LANES), 1)
    first_half = (lane % MLA_ROPE) < (MLA_ROPE // 2)
    outs = []
    for s in range(x.shape[1] // LANES):
        xs = x[:, s * LANES:(s + 1) * LANES]
        rot = jnp.where(first_half, pltpu.roll(xs, LANES - MLA_ROPE // 2, 1), pltpu.roll(xs, MLA_ROPE // 2, 1))
        outs.append(xs * cos + rot * sin)
    return outs[0] if len(outs) == 1 else jnp.concatenate(outs, axis=1)


def _nt_dot(a, b):
    return lax.dot_general(a, b, (((1,), (1,)), ((), ())), preferred_element_type=F32)


def _tn_dot(a, b):
    return lax.dot_general(a, b, (((0,), (0,)), ((), ())), preferred_element_type=F32)


def _fold_lanes(x, op):
    acc = x[:, :LANES]
    for s in range(1, x.shape[1] // LANES):
        acc = op(acc, x[:, s * LANES:(s + 1) * LANES])
    return acc


def _adaln_kernel(c_ref, w_ref, b_ref, o_ref):
    c = c_ref[...]
    s = (c * jax.nn.sigmoid(c)).astype(BF16)
    o_ref[...] = jnp.dot(s, w_ref[...], preferred_element_type=F32) + b_ref[...]


def _adaln(c, w_bf, b):
    rows = c.shape[0]
    n = w_bf.shape[1]
    tn = 1536
    return pl.pallas_call(
        _adaln_kernel,
        grid=(n // tn,),
        in_specs=[pl.BlockSpec((rows, D_MODEL), lambda j: (0, 0)),
                  pl.BlockSpec((D_MODEL, tn), lambda j: (0, j)),
                  pl.BlockSpec((1, tn), lambda j: (0, j))],
        out_specs=pl.BlockSpec((rows, tn), lambda j: (0, j)),
        out_shape=jax.ShapeDtypeStruct((rows, n), F32),
        compiler_params=_cparams(("arbitrary",)),
        name="adaln",
    )(c, w_bf, b.reshape(1, n))


_C_QLAT = 0
_C_KV = _C_QLAT + Q_LORA
_C_QD = _C_KV + KV_LORA
_C_KD = _C_QD + DSA_OUT
_C_VD = _C_KD + DSA_OUT
_C_QI = _C_VD + DSA_OUT
_C_MISC = _C_QI + IDX_HEADS * IDX_DIM
_C_KIDUP = _C_MISC + LANES
_C_END = _C_KIDUP + LANES
_M_KI = MLA_ROPE
_M_WI = MLA_ROPE + IDX_DIM


def _proj_kernel(x_ref, sh_ref, sc_ref, ga_ref, win_ref, gql_ref, wqu_ref, gkv_ref, wuk_ref,
                 gq_ref, gk_ref, gm_ref, gdq_ref, gdk_ref, cq_ref, sq_ref, cm_ref, sm_ref,
                 bq_ref, b64_ref, bm_ref, ex_ref,
                 qall_ref, kall_ref, ckv_ref, ckvb_ref, misc_ref, qd_ref, kd_ref, kdb_ref, vd_ref, vdb_ref,
                 qi_ref, kid_ref, wib_ref):
    x = x_ref[...]
    xn = x * lax.rsqrt(jnp.mean(x * x, axis=-1, keepdims=True) + EPS) * ga_ref[...]
    h = xn * (1.0 + sc_ref[...]) + sh_ref[...]
    p = jnp.dot(h.astype(BF16), win_ref[...], preferred_element_type=F32)

    ql = p[:, _C_QLAT:_C_KV]
    qln = ql * lax.rsqrt(jnp.mean(ql * ql, axis=-1, keepdims=True) + EPS) * gql_ref[...]
    q = jnp.dot(qln.astype(BF16), wqu_ref[...], preferred_element_type=F32)
    qn = q * lax.rsqrt(_group_mean(q * q, bq_ref[...]) + EPS) * gq_ref[...]
    qall_ref[...] = _rope_slabs(qn, cq_ref[...], sq_ref[...]).astype(BF16)

    kv = p[:, _C_KV:_C_QD]
    ckv = kv * lax.rsqrt(jnp.mean(kv * kv, axis=-1, keepdims=True) + EPS) * gkv_ref[...]
    ckv_ref[...] = ckv
    ckvb = ckv.astype(BF16)
    ckvb_ref[...] = ckvb
    kn = jnp.dot(ckvb, wuk_ref[...], preferred_element_type=F32)
    kn = kn * lax.rsqrt(_group_mean(kn * kn, bq_ref[...]) + EPS) * gk_ref[...]

    m = p[:, _C_MISC:_C_KIDUP]
    lane = lax.broadcasted_iota(I32, m.shape, 1)
    is_kr = lane < MLA_ROPE
    mm = _split_dot(m * m, bm_ref[...])
    mn = jnp.where(is_kr, m * lax.rsqrt(mm + EPS) * gm_ref[...], m)
    mr = _rope_slabs(mn, cm_ref[...], sm_ref[...])
    is_wi = (lane >= _M_WI) & (lane < _M_WI + IDX_HEADS)
    misc = jnp.where(is_wi, mr * (IDX_HEADS ** -0.5), mr)
    misc_ref[...] = misc
    wib_ref[...] = _split_dot(misc, ex_ref[...])
    kr_placed = jnp.where((lane >= MLA_NOPE) & (lane < MLA_NOPE + MLA_ROPE), pltpu.roll(mr, MLA_NOPE, 1), 0.0)
    kall_ref[...] = jnp.concatenate(
        [kn[:, s * LANES:(s + 1) * LANES] + kr_placed for s in range(MLA_HEADS)], axis=1).astype(BF16)

    qd = p[:, _C_QD:_C_KD]
    qd_ref[...] = qd * lax.rsqrt(_group_mean(qd * qd, b64_ref[...]) + EPS) * gdq_ref[...]
    kd = p[:, _C_KD:_C_VD]
    kdn = kd * lax.rsqrt(_group_mean(kd * kd, b64_ref[...]) + EPS) * gdk_ref[...]
    kd_ref[...] = kdn
    kdb_ref[...] = kdn.astype(BF16)
    vd = p[:, _C_VD:_C_QI]
    vd_ref[...] = vd
    vdb_ref[...] = vd.astype(BF16)
    qi_ref[...] = p[:, _C_QI:_C_MISC].astype(BF16)
    kid_ref[...] = p[:, _C_KIDUP:_C_END].astype(BF16)


def _block_mean_matrix(blocks):
    m = np.zeros((LANES, LANES), np.float32)
    for start, size in blocks:
        m[start:start + size, start:start + size] = 1.0 / size
    return jnp.asarray(m, BF16)


def _head_weight_expander():
    assert IDX_DIM == 64
    m = np.zeros((LANES, IDX_HEADS * LANES), np.float32)
    for h in range(IDX_HEADS):
        m[_M_WI + h, h * LANES:(h + 1) * LANES] = IDX_DIM ** -0.5
    return jnp.asarray(m, BF16)


def _prep_proj_weights(w_in, g_q_lora, w_q_up, g_kv_lora, w_uk, g_mla_qn, g_mla_qr, g_mla_kn, g_mla_kr,
                       g_dsa_q, g_dsa_k, g_attn_norm):
    offs = np.cumsum((0,) + IN_SIZES)
    sec = lambda k: w_in[:, offs[k]:offs[k + 1]]
    zeros = lambda n: jnp.zeros((D_MODEL, n), w_in.dtype)
    misc = jnp.concatenate([sec(2), sec(7), sec(8), zeros(LANES - MLA_ROPE - IDX_DIM - IDX_HEADS)], axis=1)
    w_in_r = jnp.concatenate([sec(0), sec(1), sec(3), sec(4), sec(5), sec(6), misc, sec(7), sec(7)], axis=1)
    wq = w_q_up.reshape(Q_LORA, MLA_HEADS, MLA_NOPE + MLA_ROPE)
    wq = jnp.pad(wq, ((0, 0), (0, 0), (0, LANES - MLA_NOPE - MLA_ROPE))).reshape(Q_LORA, MLA_HEADS * LANES)
    wk = jnp.pad(w_uk, ((0, 0), (0, 0), (0, LANES - MLA_NOPE))).reshape(KV_LORA, MLA_HEADS * LANES)
    pad1 = lambda v, n: jnp.pad(v, (0, n - v.shape[0]))
    gq = jnp.tile(pad1(jnp.concatenate([g_mla_qn, g_mla_qr]), LANES), MLA_HEADS).reshape(1, -1)
    gk = jnp.tile(pad1(g_mla_kn, LANES), MLA_HEADS).reshape(1, -1)
    gm = jnp.concatenate([g_mla_kr, jnp.ones((LANES - MLA_ROPE,), F32)]).reshape(1, -1)
    return dict(
        win=w_in_r.astype(BF16), wqu=wq.astype(BF16), wuk=wk.astype(BF16),
        ga=g_attn_norm.reshape(1, -1), gql=g_q_lora.reshape(1, -1), gkv=g_kv_lora.reshape(1, -1),
        gq=gq, gk=gk, gm=gm,
        gdq=jnp.tile(g_dsa_q, DSA_HEADS).reshape(1, -1), gdk=jnp.tile(g_dsa_k, DSA_HEADS).reshape(1, -1),
        bq=_block_mean_matrix([(0, MLA_NOPE), (MLA_NOPE, MLA_ROPE)]),
        b64=_block_mean_matrix([(0, DSA_HEAD_DIM), (DSA_HEAD_DIM, DSA_HEAD_DIM)]),
        bm=_block_mean_matrix([(0, MLA_ROPE)]),
        ex=_head_weight_expander(),
    )


def _rope_tables(pos):
    half = MLA_ROPE // 2
    inv = ROPE_BASE ** (-jnp.arange(half, dtype=F32) / half)
    ang = pos.astype(F32)[:, None] * inv
    cos, sin = jnp.cos(ang), jnp.sin(ang)
    cos32 = jnp.concatenate([cos, cos], axis=1)
    sin32 = jnp.concatenate([-sin, sin], axis=1)
    n = pos.shape[0]
    ones = lambda w: jnp.ones((n, w), F32)
    zeros = lambda w: jnp.zeros((n, w), F32)
    cq = jnp.concatenate([ones(MLA_NOPE), cos32, ones(LANES - MLA_NOPE - MLA_ROPE)], axis=1)
    sq = jnp.concatenate([zeros(MLA_NOPE), sin32, zeros(LANES - MLA_NOPE - MLA_ROPE)], axis=1)
    cm = jnp.concatenate([cos32, ones(LANES - MLA_ROPE)], axis=1)
    sm = jnp.concatenate([sin32, zeros(LANES - MLA_ROPE)], axis=1)
    return cq, sq, cm, sm


def _project(x2d, shift, scale, pw, tables, tm, rows_per_mod):
    n = x2d.shape[0]
    nt = n // tm
    cq, sq, cm, sm = tables
    tpos = cq.shape[0] // tm
    const = lambda shape: pl.BlockSpec(shape, lambda i: (0,) * len(shape))
    row = lambda w: pl.BlockSpec((tm, w), lambda i: (i, 0))
    if rows_per_mod:
        per = rows_per_mod // tm
        mod_spec = pl.BlockSpec((None, 1, D_MODEL), lambda i: (i // per, 0, 0))
    else:
        mod_spec = row(D_MODEL)
    tab = pl.BlockSpec((tm, LANES), lambda i: (i % tpos, 0))
    in_specs = [row(D_MODEL), mod_spec, mod_spec, const((1, D_MODEL)), const((D_MODEL, _C_END)),
                const((1, Q_LORA)), const((Q_LORA, MLA_HEADS * LANES)), const((1, KV_LORA)),
                const((KV_LORA, MLA_HEADS * LANES)), const((1, MLA_HEADS * LANES)), const((1, MLA_HEADS * LANES)),
                const((1, LANES)), const((1, DSA_OUT)), const((1, DSA_OUT)), tab, tab, tab, tab,
                const((LANES, LANES)), const((LANES, LANES)), const((LANES, LANES)),
                const((LANES, IDX_HEADS * LANES))]
    widths = [(MLA_HEADS * LANES, BF16), (MLA_HEADS * LANES, BF16), (KV_LORA, F32), (KV_LORA, BF16), (LANES, F32),
              (DSA_OUT, F32), (DSA_OUT, F32), (DSA_OUT, BF16), (DSA_OUT, F32), (DSA_OUT, BF16),
              (IDX_HEADS * IDX_DIM, BF16), (LANES, BF16), (IDX_HEADS * LANES, F32)]
    outs = pl.pallas_call(
        _proj_kernel,
        grid=(nt,),
        in_specs=in_specs,
        out_specs=[row(w) for w, _ in widths],
        out_shape=[jax.ShapeDtypeStruct((n, w), dt) for w, dt in widths],
        compiler_params=_cparams(("arbitrary",)),
        name="project",
    )(x2d, shift, scale, pw["ga"], pw["win"], pw["gql"], pw["wqu"], pw["gkv"], pw["wuk"],
      pw["gq"], pw["gk"], pw["gm"], pw["gdq"], pw["gdk"], cq, sq, cm, sm, pw["bq"], pw["b64"], pw["bm"],
      pw["ex"])
    names = ("qall", "kall", "ckv", "ckvb", "misc", "qd", "kd", "kdb", "vd", "vdb", "qi", "kid", "wib")
    return dict(zip(names, outs))


_MLA_TQ = 256
_MLA_TK = 512


def _mla_prompt_kernel(q_ref, k_ref, c_ref, wuv_ref, o_ref, s_ref):
    tq, tk = _MLA_TQ, _MLA_TK
    i = pl.program_id(1)
    q0 = i * tq
    nkb = (q0 + tq + tk - 1) // tk
    row = q0 + lax.broadcasted_iota(I32, (tq, tk), 0)
    col0 = lax.broadcasted_iota(I32, (tq, tk), 1)
    heads = []
    for pr in range(MLA_HEADS // 2):
        qhs = [q_ref[:, (2 * pr + half) * LANES:(2 * pr + half + 1) * LANES] for half in range(2)]

        def pass_a(j, mrun, pr=pr, qhs=qhs):
            k0 = pl.multiple_of(j * tk, tk)
            causal = col0 + k0 <= row
            out = []
            for half in range(2):
                h = 2 * pr + half
                s = _nt_dot(qhs[half], k_ref[pl.ds(k0, tk), h * LANES:(h + 1) * LANES]) * MLA_SCALE
                s = jnp.where(causal, s, NEG_INF)
                s_ref[half, :, pl.ds(k0, tk)] = s
                out.append(jnp.maximum(mrun[half], _fold_lanes(s, jnp.maximum)))
            return tuple(out)

        ninf = jnp.full((tq, LANES), NEG_INF, F32)
        mrun = lax.fori_loop(0, nkb, pass_a, (ninf, ninf))
        ms = [jnp.max(m, axis=1, keepdims=True) for m in mrun]

        def pass_b(j, carry, ms=ms):
            k0 = pl.multiple_of(j * tk, tk)
            cb = c_ref[pl.ds(k0, tk), :]
            out = []
            for half in range(2):
                l, acc = carry[half]
                p = jnp.exp(s_ref[half, :, pl.ds(k0, tk)] - ms[half])
                out.append((l + _fold_lanes(p, jnp.add),
                            acc + jnp.dot(p.astype(BF16), cb, preferred_element_type=F32)))
            return tuple(out)

        zero = jnp.zeros((tq, LANES), F32)
        for l, acc in lax.fori_loop(0, nkb, pass_b, ((zero, zero), (zero, zero))):
            heads.append((acc / jnp.sum(l, axis=1, keepdims=True)).astype(BF16))
    for pr in range(MLA_HEADS // 2):
        o = (jnp.dot(heads[2 * pr], wuv_ref[2 * pr], preferred_element_type=F32)
             + jnp.dot(heads[2 * pr + 1], wuv_ref[2 * pr + 1], preferred_element_type=F32))
        o_ref[:, pr * LANES:(pr + 1) * LANES] = o.astype(BF16)


def _pad_wuv(w_uv):
    w = jnp.transpose(w_uv, (1, 0, 2))
    even = jnp.pad(w, ((0, 0), (0, 0), (0, LANES - MLA_V)))
    odd = jnp.pad(w, ((0, 0), (0, 0), (LANES - MLA_V, 0)))
    is_odd = (jnp.arange(MLA_HEADS) % 2 == 1)[:, None, None]
    return jnp.where(is_odd, odd, even).astype(BF16)


def _mla_prompt(qall, kall, ckvb, wuv_pad):
    b, t, _ = qall.shape
    tq = _MLA_TQ
    assert t % _MLA_TK == 0
    return pl.pallas_call(
        _mla_prompt_kernel,
        grid=(b, t // tq),
        in_specs=[pl.BlockSpec((None, tq, MLA_HEADS * LANES), lambda bi, i: (bi, i, 0)),
                  pl.BlockSpec((None, t, MLA_HEADS * LANES), lambda bi, i: (bi, 0, 0)),
                  pl.BlockSpec((None, t, KV_LORA), lambda bi, i: (bi, 0, 0)),
                  pl.BlockSpec((MLA_HEADS, KV_LORA, LANES), lambda bi, i: (0, 0, 0))],
        out_specs=pl.BlockSpec((None, tq, MLA_OUT), lambda bi, i: (bi, i, 0)),
        out_shape=jax.ShapeDtypeStruct((b, t, MLA_OUT), BF16),
        scratch_shapes=[pltpu.VMEM((2, tq, t), F32)],
        compiler_params=_cparams(("arbitrary", "arbitrary")),
        name="mla_prompt",
    )(qall, kall, ckvb, wuv_pad)


_DSA_T = 128
_DSA_TQ = 256
_DSA_CW = 512


def _sortable_key(score):
    bits = pltpu.bitcast(score, I32)
    key = jnp.where(bits < 0, bits ^ jnp.int32(0x7FFFFFFF), bits)
    return jnp.where(score == 0.0, 0, key)


def _bias_tiles(rb_ref, tz_ref):
    t = _DSA_T
    r = lax.broadcasted_iota(I32, (t, t), 0)
    c = lax.broadcasted_iota(I32, (t, t), 1)
    exact = REL_BUCKETS // 2
    for which in range(2):
        d = jnp.maximum(r - c + t * which, 0)
        logd = jnp.log(jnp.maximum(d, 1).astype(F32) / exact) / math.log(REL_MAX_DIST / exact)
        far = jnp.minimum(exact + (logd * (REL_BUCKETS - exact)).astype(I32), REL_BUCKETS - 1)
        bucket = jnp.where(d < exact, d, far)
        for h in range(DSA_HEADS):
            tile = jnp.zeros((t, t), F32)
            for bk in range(REL_BUCKETS):
                tile = jnp.where(bucket == bk, rb_ref[bk, h], tile)
            tz_ref[h, which] = tile


def _select_mask(keys_ref, mask_ref, tri_ref, nch, n_sel, rows):
    cw = _DSA_CW

    def count(pred):
        def body(j, acc):
            kb = keys_ref[:, pl.ds(pl.multiple_of(j * cw, cw), cw)]
            return acc + _fold_lanes(jnp.where(pred(kb), 1.0, 0.0), jnp.add)
        acc = lax.fori_loop(0, nch, body, jnp.zeros((rows, LANES), F32))
        return jnp.sum(acc, axis=1, keepdims=True)

    kf = jnp.float32(n_sel)
    zero = jnp.zeros((rows, 1), I32)
    thr = jnp.where(count(lambda kb: kb >= zero) >= kf, zero, jnp.full((rows, 1), INT_MIN, I32))

    def bit_body(bi, thr):
        cand = thr + lax.shift_left(jnp.int32(1), 30 - bi)
        return jnp.where(count(lambda kb: kb >= cand) >= kf, cand, thr)

    thr = lax.fori_loop(0, 31, bit_body, thr)
    need = kf - count(lambda kb: kb > thr)

    def mask_body(j, carry):
        k0 = pl.multiple_of(j * cw, cw)
        kb = keys_ref[:, pl.ds(k0, cw)]
        eq = kb == thr
        eqf = jnp.where(eq, 1.0, 0.0)
        before = jnp.dot(eqf.astype(BF16), tri_ref[...], preferred_element_type=F32) + carry
        keep = ((kb > thr) | (eq & (before < need))) & (kb > INT_MIN)
        mask_ref[:, pl.ds(k0, cw)] = jnp.where(keep, 0.0, NEG_INF)
        return carry + jnp.sum(eqf, axis=1, keepdims=True)

    lax.fori_loop(0, nch, mask_body, jnp.zeros((rows, 1), F32))


def _dsa_prompt_kernel(rb_ref, qi_ref, wib_ref, kid_ref, qd_ref, kd_ref, vd_ref, tri_ref, o_ref,
                       keys_ref, mask_ref, tz_ref, s_ref, *, n_sel):
    t, tq, cw = _DSA_T, _DSA_TQ, _DSA_CW
    sub_n = cw // t
    q_sub = tq // t
    bi = pl.program_id(0)
    i = pl.program_id(1)
    nch = ((i + 1) * tq + cw - 1) // cw

    @pl.when((bi == 0) & (i == 0))
    def _():
        _bias_tiles(rb_ref, tz_ref)

    row = i * tq + lax.broadcasted_iota(I32, (tq, cw), 0)
    col0 = lax.broadcasted_iota(I32, (tq, cw), 1)
    lane = lax.broadcasted_iota(I32, (tq, LANES), 1)
    low = lane < DSA_HEAD_DIM

    def head_halves(ref, pr):
        qs = ref[:, pr * LANES:(pr + 1) * LANES].astype(BF16)
        zero = jnp.zeros_like(qs)
        return jnp.where(low, qs, zero), jnp.where(low, zero, qs)

    qim = [q for pr in range(IDX_HEADS // 2) for q in head_halves(qi_ref, pr)]

    def idx_body(c, _):
        k0 = pl.multiple_of(c * cw, cw)
        kk = kid_ref[pl.ds(k0, cw), :]
        acc = jnp.zeros((tq, cw), F32)
        for h in range(IDX_HEADS):
            r = jnp.maximum(_nt_dot(qim[h], kk), 0.0)
            w = wib_ref[:, h * LANES:(h + 1) * LANES]
            acc = acc + jnp.concatenate([w] * sub_n, axis=1) * r
        keys_ref[:, pl.ds(k0, cw)] = jnp.where(col0 + k0 <= row, _sortable_key(acc), INT_MIN)
        return 0

    lax.fori_loop(0, nch, idx_body, 0)
    few_keys = (i + 1) * tq <= n_sel

    @pl.when(few_keys)
    def _():
        def keep_all(c, _):
            k0 = pl.multiple_of(c * cw, cw)
            mask_ref[:, pl.ds(k0, cw)] = jnp.where(keys_ref[:, pl.ds(k0, cw)] > INT_MIN, 0.0, NEG_INF)
            return 0
        lax.fori_loop(0, nch, keep_all, 0)

    @pl.when(jnp.logical_not(few_keys))
    def _():
        _select_mask(keys_ref, mask_ref, tri_ref, nch, n_sel, tq)

    for pr in range(DSA_HEADS // 2):
        qms = head_halves(qd_ref, pr)
        fars = [rb_ref[REL_BUCKETS - 1, 2 * pr + half] for half in range(2)]

        def pass_a(c, mrun, pr=pr, qms=qms, fars=fars):
            k0 = pl.multiple_of(c * cw, cw)
            kb = kd_ref[pl.ds(k0, cw), pr * LANES:(pr + 1) * LANES]
            mk = mask_ref[:, pl.ds(k0, cw)]
            out = []
            for half in range(2):
                h = 2 * pr + half
                s = _nt_dot(qms[half], kb) * DSA_SCALE
                bands = []
                for a in range(q_sub):
                    qblk = i * q_sub + a
                    parts = []
                    for sub in range(sub_n):
                        blk = c * sub_n + sub
                        bias = jnp.where(blk == qblk, tz_ref[h, 0],
                                         jnp.where(blk == qblk - 1, tz_ref[h, 1], fars[half]))
                        parts.append(s[a * t:(a + 1) * t, sub * t:(sub + 1) * t] + bias)
                    bands.append(jnp.concatenate(parts, axis=1))
                s = jnp.concatenate(bands, axis=0) + mk
                s_ref[half, :, pl.ds(k0, cw)] = s
                out.append(jnp.maximum(mrun[half], _fold_lanes(s, jnp.maximum)))
            return tuple(out)

        ninf = jnp.full((tq, LANES), NEG_INF, F32)
        mrun = lax.fori_loop(0, nch, pass_a, (ninf, ninf))
        ms = [jnp.max(m, axis=1, keepdims=True) for m in mrun]

        def pass_b(c, carry, pr=pr, ms=ms):
            k0 = pl.multiple_of(c * cw, cw)
            vb = vd_ref[pl.ds(k0, cw), pr * LANES:(pr + 1) * LANES]
            out = []
            for half in range(2):
                l, acc = carry[half]
                p = jnp.exp(s_ref[half, :, pl.ds(k0, cw)] - ms[half])
                out.append((l + _fold_lanes(p, jnp.add),
                            acc + jnp.dot(p.astype(BF16), vb, preferred_element_type=F32)))
            return tuple(out)

        zero = jnp.zeros((tq, LANES), F32)
        (l0, a0), (l1, a1) = lax.fori_loop(0, nch, pass_b, ((zero, zero), (zero, zero)))
        o0 = a0 / jnp.sum(l0, axis=1, keepdims=True)
        o1 = a1 / jnp.sum(l1, axis=1, keepdims=True)
        o_ref[:, pr * LANES:(pr + 1) * LANES] = jnp.where(low, o0, o1).astype(BF16)


def _strict_upper(n):
    return jnp.asarray(np.triu(np.ones((n, n), np.float32), 1), BF16)


def _dsa_prompt(rel_bias, qi, wib, kid, qd, kdb, vdb):
    b, t, _ = qi.shape
    tq, cw = _DSA_TQ, _DSA_CW
    assert t % cw == 0 and t % tq == 0 and tq % _DSA_T == 0
    n_sel = min(IDX_TOPK_MAX, t // 4)
    blk = lambda w: pl.BlockSpec((None, tq, w), lambda bi, i: (bi, i, 0))
    full = lambda w: pl.BlockSpec((None, t, w), lambda bi, i: (bi, 0, 0))
    return pl.pallas_call(
        functools.partial(_dsa_prompt_kernel, n_sel=n_sel),
        grid=(b, t // tq),
        in_specs=[pl.BlockSpec(memory_space=pltpu.SMEM),
                  blk(IDX_HEADS * IDX_DIM), blk(IDX_HEADS * LANES), full(LANES), blk(DSA_OUT), full(DSA_OUT),
                  full(DSA_OUT), pl.BlockSpec((cw, cw), lambda bi, i: (0, 0))],
        out_specs=blk(DSA_OUT),
        out_shape=jax.ShapeDtypeStruct((b, t, DSA_OUT), BF16),
        scratch_shapes=[pltpu.VMEM((tq, t), I32), pltpu.VMEM((tq, t), F32),
                        pltpu.VMEM((DSA_HEADS, 2, _DSA_T, _DSA_T), F32), pltpu.VMEM((2, tq, t), F32)],
        compiler_params=_cparams(("arbitrary", "arbitrary")),
        name="dsa_prompt",
    )(rel_bias, qi, wib, kid, qd, kdb, vdb, _strict_upper(cw))


_EXPERTS_PER_GROUP = N_EXPERTS // N_GROUPS
_MOE_GROUP = 4


def _first_index_of_max(v, idx, axis, sentinel):
    mx = jnp.max(v, axis=axis, keepdims=True)
    first = jnp.min(jnp.where(v == mx, idx, sentinel), axis=axis, keepdims=True)
    return mx, first


def _route(logits_t, bias_col):
    n_tok = logits_t.shape[1]
    scores = jax.nn.sigmoid(logits_t)
    biased = scores + bias_col
    b3 = biased.reshape(N_GROUPS, _EXPERTS_PER_GROUP, n_tok)
    j3 = lax.broadcasted_iota(I32, b3.shape, 1)
    m1, f1 = _first_index_of_max(b3, j3, 1, _EXPERTS_PER_GROUP)
    m2 = jnp.max(jnp.where(j3 == f1, NEG_INF, b3), axis=1, keepdims=True)
    gs = (m1 + m2).reshape(N_GROUPS, n_tok)
    gi = lax.broadcasted_iota(I32, gs.shape, 0)
    gsel = jnp.zeros(gs.shape, jnp.bool_)
    for _ in range(TOPK_GROUPS):
        _, first = _first_index_of_max(gs, gi, 0, N_GROUPS)
        hit = gi == first
        gsel = gsel | hit
        gs = jnp.where(hit, NEG_INF, gs)
    gsel3 = jnp.broadcast_to(gsel.reshape(N_GROUPS, 1, n_tok), b3.shape)
    masked = jnp.where(gsel3, b3, NEG_INF).reshape(N_EXPERTS, n_tok)
    ei = lax.broadcasted_iota(I32, masked.shape, 0)
    sel = jnp.zeros(masked.shape, jnp.bool_)
    for _ in range(TOP_K):
        _, first = _first_index_of_max(masked, ei, 0, N_EXPERTS)
        hit = ei == first
        sel = sel | hit
        masked = jnp.where(hit, NEG_INF, masked)
    w = jnp.where(sel, scores, 0.0)
    gate = w / jnp.sum(w, axis=0, keepdims=True) * ROUTED_SCALE
    return sel, gate


def _swiglu_bf(x_bf, wgu_ref, wdown_ref, d_hidden):
    gu = jnp.dot(x_bf, wgu_ref[...], preferred_element_type=F32)
    g, u = gu[:, :d_hidden], gu[:, d_hidden:]
    act = (g * jax.nn.sigmoid(g)) * u
    return jnp.dot(act.astype(BF16), wdown_ref[...], preferred_element_type=F32)


def _moe_kernel(x_ref, oa_ref, ob_ref, ga_ref, sf_ref, cf_ref, gf_ref, gn_ref, wo_ref, wr_ref, br_ref,
                wsg_ref, wsd_ref, tri_ref, weg_ref, wed_ref, y_ref, xt_ref, acc_ref, gate_ref, rank_ref,
                *, cap):
    tm = x_ref.shape[0]
    e = pl.program_id(1)

    @pl.when(e == 0)
    def _():
        half = oa_ref.shape[1]
        attn = (jnp.dot(oa_ref[...], wo_ref[:half, :], preferred_element_type=F32)
                + jnp.dot(ob_ref[...], wo_ref[half:, :], preferred_element_type=F32))
        x1 = x_ref[...] + ga_ref[...] * attn
        y_ref[...] = x1
        hn = x1 * lax.rsqrt(jnp.mean(x1 * x1, axis=-1, keepdims=True) + EPS) * gn_ref[...]
        xt = (hn * (1.0 + cf_ref[...]) + sf_ref[...]).astype(BF16)
        xt_ref[...] = xt
        sel, gate = _route(_nt_dot(wr_ref[...], xt), br_ref[...])
        ind = jnp.where(sel, 1.0, 0.0)
        before = jnp.dot(ind.astype(BF16), tri_ref[...], preferred_element_type=F32)
        rank_ref[...] = jnp.where(sel, before, -1.0)
        gate_ref[...] = jnp.where(sel, gate, 0.0)
        acc_ref[...] = jnp.zeros_like(acc_ref)

    grp = _MOE_GROUP
    rank_rows = [rank_ref[pl.ds(e * grp + j, 1), :] for j in range(grp)]
    gate_rows = [gate_ref[pl.ds(e * grp + j, 1), :] for j in range(grp)]
    top = rank_rows[0]
    for j in range(1, grp):
        top = jnp.maximum(top, rank_rows[j])
    count = (jnp.max(top) + 1.0).astype(I32)
    n_chunks = (count + cap - 1) // cap
    slot = lax.broadcasted_iota(I32, (cap, tm), 0).astype(F32)

    def chunk(c, _):
        base = slot + (c * cap).astype(F32)
        picks = [jnp.where(base == rank_rows[j], 1.0, 0.0) for j in range(grp)]
        pick_b = jnp.concatenate(picks, axis=0).astype(BF16)
        xe = jnp.dot(pick_b, xt_ref[...], preferred_element_type=F32).astype(BF16)
        ys = []
        for j in range(grp):
            ye = _swiglu_bf(xe[j * cap:(j + 1) * cap], weg_ref.at[j], wed_ref.at[j], D_EXPERT)
            ge = jnp.sum(picks[j] * gate_rows[j], axis=1, keepdims=True)
            ys.append((ye * ge).astype(BF16))
        acc_ref[...] += _tn_dot(pick_b, jnp.concatenate(ys, axis=0))
        return 0

    lax.fori_loop(0, n_chunks, chunk, 0)

    @pl.when(e == N_EXPERTS // grp - 1)
    def _():
        shared = _swiglu_bf(xt_ref[...], wsg_ref, wsd_ref, D_SHARED)
        y_ref[...] = y_ref[...] + gf_ref[...] * (acc_ref[...] + shared)


def _moe(x2d, oa, ob, mods, g_ffn, wo_bf, wr_t_bf, b_router, wsg_bf, wsd_bf, weg_bf, wed_bf, tm, rows_per_mod):
    n = x2d.shape[0]
    assert n % tm == 0 and tm % LANES == 0
    nt = n // tm
    cap = LANES
    const = lambda shape: pl.BlockSpec(shape, lambda t, e: (0,) * len(shape))
    row = lambda w: pl.BlockSpec((tm, w), lambda t, e: (t, 0))
    if rows_per_mod:
        per = rows_per_mod // tm
        mod_spec = pl.BlockSpec((None, 1, D_MODEL), lambda t, e: (t // per, 0, 0))
    else:
        mod_spec = row(D_MODEL)
    half = oa.shape[1]
    return pl.pallas_call(
        functools.partial(_moe_kernel, cap=cap),
        grid=(nt, N_EXPERTS // _MOE_GROUP),
        in_specs=[row(D_MODEL), row(half), row(half), mod_spec, mod_spec, mod_spec, mod_spec,
                  const((1, D_MODEL)), const((2 * half, D_MODEL)), const((N_EXPERTS, D_MODEL)),
                  const((N_EXPERTS, 1)), const((D_MODEL, 2 * D_SHARED)), const((D_SHARED, D_MODEL)),
                  const((tm, tm)),
                  pl.BlockSpec((_MOE_GROUP, D_MODEL, 2 * D_EXPERT), lambda t, e: (e, 0, 0)),
                  pl.BlockSpec((_MOE_GROUP, D_EXPERT, D_MODEL), lambda t, e: (e, 0, 0))],
        out_specs=row(D_MODEL),
        out_shape=jax.ShapeDtypeStruct((n, D_MODEL), F32),
        scratch_shapes=[pltpu.VMEM((tm, D_MODEL), BF16), pltpu.VMEM((tm, D_MODEL), F32),
                        pltpu.VMEM((N_EXPERTS, tm), F32), pltpu.VMEM((N_EXPERTS, tm), F32)],
        compiler_params=_cparams(("arbitrary", "arbitrary")),
        name="moe",
    )(x2d, oa, ob, *mods, g_ffn.reshape(1, -1), wo_bf, wr_t_bf, b_router.reshape(-1, 1), wsg_bf, wsd_bf,
      _strict_upper(tm), weg_bf, wed_bf)


def _col_blocks(q):
    nh = q.shape[0]
    r = lax.broadcasted_iota(I32, (nh, LANES), 0)
    c = lax.broadcasted_iota(I32, (nh, LANES), 1)
    blocks = [_tn_dot(q, jnp.where((r == h) & (c == h), 1.0, 0.0).astype(q.dtype)) for h in range(nh)]
    return jnp.concatenate(blocks, axis=0)


def _rows_to_cols(row):
    r8 = jnp.broadcast_to(row, (8, row.shape[1]))
    e0 = jnp.where(lax.broadcasted_iota(I32, (8, LANES), 0) == 0, 1.0, 0.0).astype(row.dtype)
    return _tn_dot(r8, e0)


def _head_diag(acc, width):
    r = lax.broadcasted_iota(I32, acc.shape, 0)
    c = lax.broadcasted_iota(I32, acc.shape, 1)
    return jnp.sum(jnp.where(r == c // width, acc, 0.0), axis=0, keepdims=True)


class _PageStream:
    def __init__(self, hbm_ref, buf_ref, sem_ref, pt_ref, seq, pages_per_chunk):
        self.hbm, self.buf, self.sem, self.pt, self.seq, self.ppc = hbm_ref, buf_ref, sem_ref, pt_ref, seq, pages_per_chunk

    def _copy(self, page, slot, r):
        return pltpu.make_async_copy(self.hbm.at[page], self.buf.at[slot, r], self.sem.at[slot])

    def start(self, chunk, slot):
        for r in range(self.ppc):
            self._copy(self.pt[self.seq, chunk * self.ppc + r], slot, r).start()

    def wait(self, slot):
        for r in range(self.ppc):
            self._copy(0, slot, r).wait()


_STREAM_DEPTH = 8


def _stream_loop(streams, n_chunks, body):
    depth = _STREAM_DEPTH
    for d in range(min(depth - 1, n_chunks)):
        for st in streams:
            st.start(d, d)

    def step(c, _):
        slot = c % depth
        nxt = c + depth - 1

        @pl.when(nxt < n_chunks)
        def _():
            for st in streams:
                st.start(nxt, nxt % depth)

        for st in streams:
            st.wait(slot)
        body(c, slot)
        return 0

    lax.fori_loop(0, n_chunks, step, 0)


def _softmax_stats(sc_ref, n_rows, blk):
    nb = n_rows // blk
    tail = n_rows - nb * blk

    def mx_body(j, m):
        return jnp.maximum(m, jnp.max(sc_ref[pl.ds(pl.multiple_of(j * blk, blk), blk), :], axis=0, keepdims=True))

    m = lax.fori_loop(0, nb, mx_body, jnp.full((1, LANES), NEG_INF, F32))
    if tail:
        m = jnp.maximum(m, jnp.max(sc_ref[pl.ds(nb * blk, tail), :], axis=0, keepdims=True))

    def sum_body(j, l):
        return l + jnp.sum(jnp.exp(sc_ref[pl.ds(pl.multiple_of(j * blk, blk), blk), :] - m), axis=0, keepdims=True)

    l = lax.fori_loop(0, nb, sum_body, jnp.zeros((1, LANES), F32))
    if tail:
        l = l + jnp.sum(jnp.exp(sc_ref[pl.ds(nb * blk, tail), :] - m), axis=0, keepdims=True)
    return m, l


_MLA_S_PAGES = 8


def _mla_sample_kernel(pt_ref, q_ref, knew_ref, cnew_ref, wuk_ref, gk_ref, wuv_ref, lat_hbm, kr_hbm, o_ref,
                       latbuf, krbuf, sems, sc_ref, latbf_ref, *, n_pages):
    s_id = pl.program_id(0)
    ppc = math.gcd(n_pages, _MLA_S_PAGES)
    ck = ppc * PAGE_SIZE
    n_chunks = n_pages // ppc
    past = n_pages * PAGE_SIZE
    row8 = lax.broadcasted_iota(I32, (8, LANES), 0)

    qblk = _col_blocks(q_ref[...]).astype(BF16)
    qr = qblk[MLA_NOPE:MLA_NOPE + MLA_ROPE, :].astype(F32)
    for h in range(1, MLA_HEADS):
        qr = qr + qblk[h * LANES + MLA_NOPE:h * LANES + MLA_NOPE + MLA_ROPE, :].astype(F32)
    qr = qr.astype(BF16)
    lat_stream = _PageStream(lat_hbm, latbuf, sems.at[0], pt_ref, s_id, ppc)
    kr_stream = _PageStream(kr_hbm, krbuf, sems.at[1], pt_ref, s_id, ppc)

    def score_chunk(c, slot):
        lat = latbuf[slot].reshape(ck, KV_LORA).astype(BF16)
        latbf_ref[pl.ds(pl.multiple_of(c * ck, ck), ck), :] = lat
        kraw = jnp.dot(lat, wuk_ref[...], preferred_element_type=F32)
        slabs = []
        for h in range(MLA_HEADS):
            x = kraw[:, h * LANES:(h + 1) * LANES]
            ms = jnp.sum(x * x, axis=1, keepdims=True) * (1.0 / MLA_NOPE)
            slabs.append((x * lax.rsqrt(ms + EPS) * gk_ref[:, h * LANES:(h + 1) * LANES]).astype(BF16))
        kn = jnp.concatenate(slabs, axis=1)
        s_rope = jnp.concatenate([_tn_dot(krbuf[slot, r].astype(BF16), qr) for r in range(ppc)], axis=0)
        s = (jnp.dot(kn, qblk, preferred_element_type=F32) + s_rope) * MLA_SCALE
        sc_ref[pl.ds(pl.multiple_of(c * ck, ck), ck), :] = s

    _stream_loop((lat_stream, kr_stream), n_chunks, score_chunk)
    s_new = jnp.dot(jnp.broadcast_to(knew_ref[...], (8, knew_ref.shape[1])), qblk,
                    preferred_element_type=F32) * MLA_SCALE
    sc_ref[pl.ds(past, 8), :] = jnp.where(row8 == 0, s_new, NEG_INF)
    m, l = _softmax_stats(sc_ref, past + 8, ck)

    def pv_chunk(c, acc):
        k0 = pl.multiple_of(c * ck, ck)
        p = jnp.exp(sc_ref[pl.ds(k0, ck), :] - m) / l
        return acc + _tn_dot(p.astype(BF16), latbf_ref[pl.ds(k0, ck), :])

    acc = lax.fori_loop(0, n_chunks, pv_chunk, jnp.zeros((LANES, KV_LORA), F32))
    p_new = jnp.exp(sc_ref[pl.ds(past, 8), :] - m) / l
    acc = acc + _tn_dot(p_new.astype(BF16), jnp.broadcast_to(cnew_ref[...], (8, KV_LORA)))
    out = jnp.dot(acc.astype(BF16), wuv_ref[...], preferred_element_type=F32)
    o_ref[...] = _head_diag(out, MLA_V).astype(BF16)


def _mla_sample(page_table, q8, knew, cnew, wuk_pad, gk, wuv_flat, cache_lat, cache_kr):
    db, n_pages = page_table.shape
    ppc = math.gcd(n_pages, _MLA_S_PAGES)
    past = n_pages * PAGE_SIZE
    per_seq = lambda shape: pl.BlockSpec((None,) + shape, lambda s, pt: (s,) + (0,) * len(shape))
    const = lambda shape: pl.BlockSpec(shape, lambda s, pt: (0,) * len(shape))
    grid_spec = pltpu.PrefetchScalarGridSpec(
        num_scalar_prefetch=1, grid=(db,),
        in_specs=[per_seq((MLA_HEADS, LANES)), per_seq((1, MLA_HEADS * LANES)), per_seq((1, KV_LORA)),
                  const((KV_LORA, MLA_HEADS * LANES)), const((1, MLA_HEADS * LANES)), const((KV_LORA, MLA_OUT)),
                  pl.BlockSpec(memory_space=pl.ANY), pl.BlockSpec(memory_space=pl.ANY)],
        out_specs=per_seq((1, MLA_OUT)),
        scratch_shapes=[pltpu.VMEM((_STREAM_DEPTH, ppc, PAGE_SIZE, KV_LORA), F32),
                        pltpu.VMEM((_STREAM_DEPTH, ppc, MLA_ROPE, PAGE_SIZE), F32),
                        pltpu.SemaphoreType.DMA((2, _STREAM_DEPTH)), pltpu.VMEM((past + 8, LANES), F32),
                        pltpu.VMEM((past, KV_LORA), BF16)])
    return pl.pallas_call(
        functools.partial(_mla_sample_kernel, n_pages=n_pages),
        grid_spec=grid_spec,
        out_shape=jax.ShapeDtypeStruct((db, 1, MLA_OUT), BF16),
        compiler_params=_cparams(("arbitrary",)),
        name="mla_sample",
    )(page_table, q8, knew, cnew, wuk_pad, gk, wuv_flat, cache_lat, cache_kr)


_DSA_S_IDX_PAGES = 16


def _select_flat(keys, tri_u, tri_l, n_sel):
    def count(pred):
        c = jnp.sum(jnp.where(pred, 1.0, 0.0), axis=1, keepdims=True)
        return jnp.sum(c, axis=0, keepdims=True)

    kf = jnp.float32(n_sel)
    zero = jnp.zeros((1, 1), I32)
    thr = jnp.where(count(keys >= zero) >= kf, zero, jnp.full((1, 1), INT_MIN, I32))

    def bit_body(bi, thr):
        cand = thr + lax.shift_left(jnp.int32(1), 30 - bi)
        return jnp.where(count(keys >= cand) >= kf, cand, thr)

    thr = lax.fori_loop(0, 31, bit_body, thr)
    need = kf - count(keys > thr)
    eq = keys == thr
    eqf = jnp.where(eq, 1.0, 0.0)
    within = jnp.dot(eqf.astype(BF16), tri_u, preferred_element_type=F32)
    rowcount = jnp.broadcast_to(jnp.sum(eqf, axis=1, keepdims=True), eqf.shape)
    carry = jnp.dot(tri_l, rowcount.astype(BF16), preferred_element_type=F32)
    keep = ((keys > thr) | (eq & (within + carry < need))) & (keys > INT_MIN)
    return jnp.where(keep, 1.0, 0.0)


_DSA_S_KV_PAGES = 4


def _round_bf16(x):
    return x.astype(BF16).astype(F32)


def _dsa_sample_kernel(pt_ref, rbt_ref, qi_ref, wi_ref, kinew_ref, qd_ref, kdnew_ref, vdnew_ref,
                       triu_ref, tril_ref, idx_hbm, k_hbm, v_hbm, o_ref,
                       idxbuf, kvbuf, sems, keys_ref, mask_ref, s_ref, qcol_ref, acc_ref, *, n_pages, n_sel):
    s_id = pl.program_id(0)
    ipc = math.gcd(n_pages, _DSA_S_IDX_PAGES)
    kpc = math.gcd(n_pages, _DSA_S_KV_PAGES)
    lane1 = lax.broadcasted_iota(I32, (1, LANES), 1)
    qi = qi_ref[...]
    wi = _round_bf16(wi_ref[...])

    keys_ref[...] = jnp.full(keys_ref.shape, INT_MIN, I32)
    idx_stream = _PageStream(idx_hbm, idxbuf, sems.at[0], pt_ref, s_id, ipc)

    def idx_chunk(c, slot):
        for r in range(ipc):
            kk = idxbuf[slot, r].astype(BF16)
            rr = jnp.maximum(jnp.dot(qi, kk, preferred_element_type=F32) * (IDX_DIM ** -0.5), 0.0)
            sc = jnp.sum(wi * _round_bf16(rr), axis=0, keepdims=True)
            keys_ref[pl.ds(c * ipc + r, 1), :] = _sortable_key(sc)

    _stream_loop((idx_stream,), n_pages // ipc, idx_chunk)
    r_new = jnp.maximum(jnp.sum(qi.astype(F32) * kinew_ref[...].astype(F32), axis=1, keepdims=True)
                        * (IDX_DIM ** -0.5), 0.0)
    sc_new = jnp.sum(wi * _round_bf16(r_new), axis=0, keepdims=True)
    keys_ref[pl.ds(n_pages, 1), :] = jnp.where(lane1 == 0, _sortable_key(jnp.broadcast_to(sc_new, (1, LANES))),
                                               INT_MIN)

    mask_ref[...] = _select_flat(keys_ref[...], triu_ref[...], tril_ref[...], n_sel)

    qd = qd_ref[...]
    row8 = lax.broadcasted_iota(I32, (8, LANES), 0)
    e0 = jnp.where(row8 == 0, 1.0, 0.0).astype(BF16)
    for h in range(DSA_HEADS):
        resid = jnp.broadcast_to(qd[h:h + 1, :], (8, DSA_HEAD_DIM))
        col = jnp.zeros((DSA_HEAD_DIM, LANES), F32)
        for _ in range(3):
            piece = resid.astype(BF16)
            col = col + _tn_dot(piece, e0)
            resid = resid - piece.astype(F32)
        qcol_ref[h] = col

    far_col = rbt_ref[:, REL_BUCKETS - 1:REL_BUCKETS]
    exact = REL_BUCKETS // 2
    d_row = PAGE_SIZE - lane1
    logd = jnp.log(jnp.maximum(d_row, 1).astype(F32) / exact) / math.log(REL_MAX_DIST / exact)
    far_b = jnp.minimum(exact + (logd * (REL_BUCKETS - exact)).astype(I32), REL_BUCKETS - 1)
    bucket = jnp.where(d_row < exact, d_row, far_b)
    near = jnp.zeros((DSA_HEADS, LANES), F32)
    for bk in range(REL_BUCKETS):
        near = jnp.where(bucket == bk, rbt_ref[:, bk:bk + 1], near)

    k_stream = _PageStream(k_hbm, kvbuf, sems.at[1], pt_ref, s_id, kpc)
    v_stream = _PageStream(v_hbm, kvbuf, sems.at[1], pt_ref, s_id, kpc)

    def k_chunk(c, slot):
        for r in range(kpc):
            page = c * kpc + r
            for h in range(DSA_HEADS):
                s_ref[page, h:h + 1, :] = jnp.sum(kvbuf[slot, r, h] * qcol_ref[h], axis=0, keepdims=True)
            bias = jnp.where(page == n_pages - 1, near, far_col)
            keep = mask_ref[pl.ds(page, 1), :] > 0.5
            s_ref[page] = jnp.where(keep, s_ref[page] * DSA_SCALE + bias, NEG_INF)

    _stream_loop((k_stream,), n_pages // kpc, k_chunk)
    s_new = (jnp.sum(qd * kdnew_ref[...], axis=1, keepdims=True) * DSA_SCALE
             + rbt_ref[:, 0:1])
    keep_new = mask_ref[pl.ds(n_pages, 1), :][:, 0:1] > 0.5
    s_new = jnp.where(keep_new, s_new, NEG_INF)

    blk = 8
    def mx_body(j, m):
        return jnp.maximum(m, jnp.max(s_ref[pl.ds(pl.multiple_of(j * blk, blk), blk)], axis=0))

    m_t = lax.fori_loop(0, n_pages // blk, mx_body, jnp.full((DSA_HEADS, LANES), NEG_INF, F32))
    m = jnp.maximum(jnp.max(m_t, axis=1, keepdims=True), s_new)

    def sum_body(j, l):
        return l + jnp.sum(jnp.exp(s_ref[pl.ds(pl.multiple_of(j * blk, blk), blk)] - m), axis=0)

    l_t = lax.fori_loop(0, n_pages // blk, sum_body, jnp.zeros((DSA_HEADS, LANES), F32))
    l = jnp.sum(l_t, axis=1, keepdims=True) + jnp.exp(s_new - m)

    acc_ref[...] = jnp.zeros_like(acc_ref)

    def v_chunk(c, slot):
        for r in range(kpc):
            page = c * kpc + r
            p = jnp.exp(s_ref[page] - m) / l
            for h in range(DSA_HEADS):
                acc_ref[h] += p[h:h + 1, :] * kvbuf[slot, r, h]

    _stream_loop((v_stream,), n_pages // kpc, v_chunk)
    ones8 = jnp.ones((8, LANES), BF16)
    rowh = lax.broadcasted_iota(I32, (DSA_HEADS, DSA_HEAD_DIM), 0)
    out = jnp.zeros((DSA_HEADS, DSA_HEAD_DIM), F32)
    for h in range(DSA_HEADS):
        resid = acc_ref[h]
        tot = jnp.zeros((8, DSA_HEAD_DIM), F32)
        for _ in range(3):
            piece = resid.astype(BF16)
            tot = tot + _nt_dot(ones8, piece)
            resid = resid - piece.astype(F32)
        out = jnp.where(rowh == h, tot, out)
    p_new = jnp.exp(s_new - m) / l
    o_ref[...] = (out + p_new * vdnew_ref[...]).astype(BF16)


def _dsa_sample(page_table, rel_bias, qi8, wi8, kinew, qd8, kdnew, vdnew, cache_idx_t, cache_k_t, cache_v_t):
    db, n_pages = page_table.shape
    n_sel = min(IDX_TOPK_MAX, (n_pages * PAGE_SIZE + 1) // 4)
    assert n_pages % 8 == 0
    ipc = math.gcd(n_pages, _DSA_S_IDX_PAGES)
    kpc = math.gcd(n_pages, _DSA_S_KV_PAGES)
    rows = -(-(n_pages + 1) // LANES) * LANES
    tri_l = jnp.asarray(np.tril(np.ones((rows, rows), np.float32), -1), BF16)
    per_seq = lambda shape: pl.BlockSpec((None,) + shape, lambda s, pt: (s,) + (0,) * len(shape))
    const = lambda shape: pl.BlockSpec(shape, lambda s, pt: (0,) * len(shape))
    any_spec = pl.BlockSpec(memory_space=pl.ANY)
    head_tile = (DSA_HEADS, DSA_HEAD_DIM)
    page_tile = (DSA_HEADS, DSA_HEAD_DIM, PAGE_SIZE)
    grid_spec = pltpu.PrefetchScalarGridSpec(
        num_scalar_prefetch=1, grid=(db,),
        in_specs=[const((DSA_HEADS, REL_BUCKETS)),
                  per_seq((IDX_HEADS, IDX_DIM)), per_seq((IDX_HEADS, 1)), per_seq((1, IDX_DIM)),
                  per_seq(head_tile), per_seq(head_tile), per_seq(head_tile),
                  const((LANES, LANES)), const((rows, rows)), any_spec, any_spec, any_spec],
        out_specs=per_seq(head_tile),
        scratch_shapes=[pltpu.VMEM((_STREAM_DEPTH, ipc, IDX_DIM, PAGE_SIZE), F32),
                        pltpu.VMEM((_STREAM_DEPTH, kpc) + page_tile, F32),
                        pltpu.SemaphoreType.DMA((2, _STREAM_DEPTH)),
                        pltpu.VMEM((rows, LANES), I32), pltpu.VMEM((rows, LANES), F32),
                        pltpu.VMEM((n_pages, DSA_HEADS, LANES), F32),
                        pltpu.VMEM(page_tile, F32), pltpu.VMEM(page_tile, F32)])
    return pl.pallas_call(
        functools.partial(_dsa_sample_kernel, n_pages=n_pages, n_sel=n_sel),
        grid_spec=grid_spec,
        out_shape=jax.ShapeDtypeStruct((db,) + head_tile, BF16),
        compiler_params=_cparams(("arbitrary",)),
        name="dsa_sample",
    )(page_table, rel_bias.T, qi8, wi8, kinew, qd8, kdnew, vdnew, _strict_upper(LANES), tri_l,
      cache_idx_t, cache_k_t, cache_v_t)


def kernel(x_prompt, x_sample, cache_mla_latent, cache_mla_krope, cache_dsa_k, cache_dsa_v, cache_idx_k, page_table, c_prompt, c_sample, rel_bias, w_ada, b_ada, g_attn_norm, w_in, g_q_lora, w_q_up, g_kv_lora, w_kv_up, g_mla_qn, g_mla_qr, g_mla_kn, g_mla_kr, g_dsa_q, g_dsa_k, w_out, g_ffn_norm, w_router, b_router, w_e_gu, w_e_down, w_s_gu, w_s_down):
    depth = w_ada.shape[0]
    assert depth == 1, "single-layer trunk"
    l = 0
    B, T, D = x_prompt.shape
    DB, TS, _ = x_sample.shape
    assert TS == 1, "one new token per sampled sequence"
    ns = DB * TS
    past = page_table.shape[1] * PAGE_SIZE

    w_kv = w_kv_up[l].reshape(KV_LORA, MLA_HEADS, MLA_NOPE + MLA_V)
    w_uk, w_uv = w_kv[..., :MLA_NOPE], w_kv[..., MLA_NOPE:]

    mod = _adaln(jnp.concatenate([c_prompt, c_sample], axis=0), w_ada[l].astype(BF16), b_ada[l])
    mod_p = [m.reshape(B, 1, D) for m in jnp.split(mod[:B], 6, axis=-1)]
    mod_s = jnp.split(mod[B:], 6, axis=-1)

    pw = _prep_proj_weights(w_in[l], g_q_lora[l], w_q_up[l], g_kv_lora[l], w_uk, g_mla_qn[l], g_mla_qr[l],
                            g_mla_kn[l], g_mla_kr[l], g_dsa_q[l], g_dsa_k[l], g_attn_norm[l])
    pp = _project(x_prompt.reshape(B * T, D), mod_p[0], mod_p[1], pw, _rope_tables(jnp.arange(T)), 256, T)
    ps = _project(x_sample.reshape(ns, D), mod_s[0], mod_s[1], pw,
                  _rope_tables(jnp.tile(past + jnp.arange(TS), DB)), ns, 0)

    r3 = lambda a: a.reshape(B, T, a.shape[-1])
    o_mla_p = _mla_prompt(r3(pp["qall"]), r3(pp["kall"]), r3(pp["ckvb"]), _pad_wuv(w_uv))
    o_dsa_p = _dsa_prompt(rel_bias, r3(pp["qi"]), r3(pp["wib"]), r3(pp["kid"]), r3(pp["qd"]), r3(pp["kdb"]),
                          r3(pp["vdb"]))

    o_mla_s = _mla_sample(page_table, ps["qall"].reshape(ns, MLA_HEADS, LANES), ps["kall"].reshape(ns, 1, -1),
                          ps["ckvb"].reshape(ns, 1, KV_LORA), pw["wuk"], pw["gk"],
                          w_uv.reshape(KV_LORA, MLA_OUT).astype(BF16), cache_mla_latent[l],
                          jnp.transpose(cache_mla_krope[l], (0, 2, 1)))
    heads3 = lambda a: a.reshape(ns, DSA_HEADS, DSA_HEAD_DIM)
    o_dsa_s = _dsa_sample(page_table, rel_bias, ps["qi"].reshape(ns, IDX_HEADS, IDX_DIM),
                          ps["misc"][:, _M_WI:_M_WI + IDX_HEADS].reshape(ns, IDX_HEADS, 1),
                          ps["kid"][:, :IDX_DIM].reshape(ns, 1, IDX_DIM),
                          heads3(ps["qd"]), heads3(ps["kd"]), heads3(ps["vd"]),
                          jnp.transpose(cache_idx_k[l], (0, 2, 1)),
                          jnp.transpose(cache_dsa_k[l], (0, 2, 3, 1)), jnp.transpose(cache_dsa_v[l], (0, 2, 3, 1)))

    moe_w = (g_ffn_norm[l], w_out[l].astype(BF16), w_router[l].T.astype(BF16), b_router[l],
             w_s_gu[l].astype(BF16), w_s_down[l].astype(BF16),
             w_e_gu[l].astype(BF16), w_e_down[l].astype(BF16))
    xp = _moe(x_prompt.reshape(B * T, D), o_mla_p.reshape(B * T, MLA_OUT), o_dsa_p.reshape(B * T, DSA_OUT),
              (mod_p[2], mod_p[3], mod_p[4], mod_p[5]), *moe_w, min(1024, T), T).reshape(B, T, D)
    ns_pad = -(-ns // LANES) * LANES
    pad_rows = lambda a: jnp.pad(a, ((0, ns_pad - ns), (0, 0)))
    xs = _moe(pad_rows(x_sample.reshape(ns, D)), pad_rows(o_mla_s.reshape(ns, MLA_OUT)),
              pad_rows(o_dsa_s.reshape(ns, DSA_OUT)), tuple(pad_rows(mod_s[k]) for k in (2, 3, 4, 5)),
              *moe_w, ns_pad, 0)[:ns].reshape(DB, TS, D)

    def caches(p, nb, nt):
        return (p["ckv"].reshape(1, nb, nt, KV_LORA),
                p["misc"][:, :MLA_ROPE].reshape(1, nb, nt, MLA_ROPE),
                p["kd"].reshape(1, nb, nt, DSA_HEADS, DSA_HEAD_DIM),
                p["vd"].reshape(1, nb, nt, DSA_HEADS, DSA_HEAD_DIM),
                p["misc"][:, _M_KI:_M_KI + IDX_DIM].reshape(1, nb, nt, IDX_DIM))

    return (xp, xs) + caches(pp, B, T) + caches(ps, DB, TS)
```

```python
import functools
import math

import jax
import jax.numpy as jnp
import numpy as np
from jax import lax
from jax.experimental import pallas as pl
from jax.experimental.pallas import tpu as pltpu

F32 = jnp.float32
BF16 = jnp.bfloat16
I32 = jnp.int32

D_MODEL = 1024
PAGE_SIZE = 128
EPS = 1e-6
MLA_HEADS = 8
MLA_NOPE = 64
MLA_ROPE = 32
MLA_V = 64
Q_LORA = 256
KV_LORA = 128
ROPE_BASE = 10000.0
MLA_SCALE = (MLA_NOPE + MLA_ROPE) ** -0.5
DSA_HEADS = 8
DSA_HEAD_DIM = 64
DSA_SCALE = DSA_HEAD_DIM ** -0.5
IDX_HEADS = 8
IDX_DIM = 64
IDX_TOPK_MAX = 256
REL_BUCKETS = 32
REL_MAX_DIST = 128
N_EXPERTS = 64
TOP_K = 6
N_GROUPS = 8
TOPK_GROUPS = 4
D_EXPERT = 256
D_SHARED = 256
ROUTED_SCALE = 2.5
MLA_OUT = MLA_HEADS * MLA_V
DSA_OUT = DSA_HEADS * DSA_HEAD_DIM
IN_SIZES = (Q_LORA, KV_LORA, MLA_ROPE, DSA_OUT, DSA_OUT, DSA_OUT, IDX_HEADS * IDX_DIM, IDX_DIM, IDX_HEADS)

LANES = 128
INT_MIN = -(2 ** 31)
NEG_INF = float("-inf")
VMEM_LIMIT = 56 * 1024 * 1024


def _cparams(sem):
    return pltpu.CompilerParams(dimension_semantics=sem, vmem_limit_bytes=VMEM_LIMIT)


def _split_dot(x, m01, passes=3):
    acc = None
    r = x
    for p in range(passes):
        hi = r.astype(BF16)
        part = jnp.dot(hi, m01, preferred_element_type=F32)
        acc = part if acc is None else acc + part
        if p + 1 < passes:
            r = r - hi.astype(F32)
    return acc


def _group_mean(sq, bmat):
    outs = [_split_dot(sq[:, s * LANES:(s + 1) * LANES], bmat) for s in range(sq.shape[1] // LANES)]
    return outs[0] if len(outs) == 1 else jnp.concatenate(outs, axis=1)


def _rope_slabs(x, cos, sin):
    lane = lax.broadcasted_iota(I32, (x.shape[0], LANES), 1)
    first_half = (lane % MLA_ROPE) < (MLA_ROPE // 2)
    outs = []
    for s in range(x.shape[1] // LANES):
        xs = x[:, s * LANES:(s + 1) * LANES]
        rot = jnp.where(first_half, pltpu.roll(xs, LANES - MLA_ROPE // 2, 1), pltpu.roll(xs, MLA_ROPE // 2, 1))
        outs.append(xs * cos + rot * sin)
    return outs[0] if len(outs) == 1 else jnp.concatenate(outs, axis=1)


def _nt_dot(a, b):
    return lax.dot_general(a, b, (((1,), (1,)), ((), ())), preferred_element_type=F32)


def _tn_dot(a, b):
    return lax.dot_general(a, b, (((0,), (0,)), ((), ())), preferred_element_type=F32)


def _fold_lanes(x, op):
    acc = x[:, :LANES]
    for s in range(1, x.shape[1] // LANES):
        acc = op(acc, x[:, s * LANES:(s + 1) * LANES])
    return acc


def _adaln_kernel(c_ref, w_ref, b_ref, o_ref):
    c = c_ref[...]
    s = (c * jax.nn.sigmoid(c)).astype(BF16)
    o_ref[...] = jnp.dot(s, w_ref[...], preferred_element_type=F32) + b_ref[...]


def _adaln(c, w_bf, b):
    rows = c.shape[0]
    n = w_bf.shape[1]
    tn = 1536
    return pl.pallas_call(
        _adaln_kernel,
        grid=(n // tn,),
        in_specs=[pl.BlockSpec((rows, D_MODEL), lambda j: (0, 0)),
                  pl.BlockSpec((D_MODEL, tn), lambda j: (0, j)),
                  pl.BlockSpec((1, tn), lambda j: (0, j))],
        out_specs=pl.BlockSpec((rows, tn), lambda j: (0, j)),
        out_shape=jax.ShapeDtypeStruct((rows, n), F32),
        compiler_params=_cparams(("arbitrary",)),
        name="adaln",
    )(c, w_bf, b.reshape(1, n))


_C_QLAT = 0
_C_KV = _C_QLAT + Q_LORA
_C_QD = _C_KV + KV_LORA
_C_KD = _C_QD + DSA_OUT
_C_VD = _C_KD + DSA_OUT
_C_QI = _C_VD + DSA_OUT
_C_MISC = _C_QI + IDX_HEADS * IDX_DIM
_C_KIDUP = _C_MISC + LANES
_C_END = _C_KIDUP + LANES
_M_KI = MLA_ROPE
_M_WI = MLA_ROPE + IDX_DIM


def _proj_kernel(x_ref, sh_ref, sc_ref, ga_ref, win_ref, gql_ref, wqu_ref, gkv_ref, wuk_ref,
                 gq_ref, gk_ref, gm_ref, gdq_ref, gdk_ref, cq_ref, sq_ref, cm_ref, sm_ref,
                 bq_ref, b64_ref, bm_ref, ex_ref,
                 qall_ref, kall_ref, ckv_ref, ckvb_ref, misc_ref, qd_ref, kd_ref, kdb_ref, vd_ref, vdb_ref,
                 qi_ref, kid_ref, wib_ref):
    x = x_ref[...]
    xn = x * lax.rsqrt(jnp.mean(x * x, axis=-1, keepdims=True) + EPS) * ga_ref[...]
    h = xn * (1.0 + sc_ref[...]) + sh_ref[...]
    p = jnp.dot(h.astype(BF16), win_ref[...], preferred_element_type=F32)

    ql = p[:, _C_QLAT:_C_KV]
    qln = ql * lax.rsqrt(jnp.mean(ql * ql, axis=-1, keepdims=True) + EPS) * gql_ref[...]
    q = jnp.dot(qln.astype(BF16), wqu_ref[...], preferred_element_type=F32)
    qn = q * lax.rsqrt(_group_mean(q * q, bq_ref[...]) + EPS) * gq_ref[...]
    qall_ref[...] = _rope_slabs(qn, cq_ref[...], sq_ref[...]).astype(BF16)

    kv = p[:, _C_KV:_C_QD]
    ckv = kv * lax.rsqrt(jnp.mean(kv * kv, axis=-1, keepdims=True) + EPS) * gkv_ref[...]
    ckv_ref[...] = ckv
    ckvb = ckv.astype(BF16)
    ckvb_ref[...] = ckvb
    kn = jnp.dot(ckvb, wuk_ref[...], preferred_element_type=F32)
    kn = kn * lax.rsqrt(_group_mean(kn * kn, bq_ref[...]) + EPS) * gk_ref[...]

    m = p[:, _C_MISC:_C_KIDUP]
    lane = lax.broadcasted_iota(I32, m.shape, 1)
    is_kr = lane < MLA_ROPE
    mm = _split_dot(m * m, bm_ref[...])
    mn = jnp.where(is_kr, m * lax.rsqrt(mm + EPS) * gm_ref[...], m)
    mr = _rope_slabs(mn, cm_ref[...], sm_ref[...])
    is_wi = (lane >= _M_WI) & (lane < _M_WI + IDX_HEADS)
    misc = jnp.where(is_wi, mr * (IDX_HEADS ** -0.5), mr)
    misc_ref[...] = misc
    wib_ref[...] = _split_dot(misc, ex_ref[...])
    kr_placed = jnp.where((lane >= MLA_NOPE) & (lane < MLA_NOPE + MLA_ROPE), pltpu.roll(mr, MLA_NOPE, 1), 0.0)
    kall_ref[...] = jnp.concatenate(
        [kn[:, s * LANES:(s + 1) * LANES] + kr_placed for s in range(MLA_HEADS)], axis=1).astype(BF16)

    qd = p[:, _C_QD:_C_KD]
    qd_ref[...] = qd * lax.rsqrt(_group_mean(qd * qd, b64_ref[...]) + EPS) * gdq_ref[...]
    kd = p[:, _C_KD:_C_VD]
    kdn = kd * lax.rsqrt(_group_mean(kd * kd, b64_ref[...]) + EPS) * gdk_ref[...]
    kd_ref[...] = kdn
    kdb_ref[...] = kdn.astype(BF16)
    vd = p[:, _C_VD:_C_QI]
    vd_ref[...] = vd
    vdb_ref[...] = vd.astype(BF16)
    qi_ref[...] = p[:, _C_QI:_C_MISC].astype(BF16)
    kid_ref[...] = p[:, _C_KIDUP:_C_END].astype(BF16)


def _block_mean_matrix(blocks):
    m = np.zeros((LANES, LANES), np.float32)
    for start, size in blocks:
        m[start:start + size, start:start + size] = 1.0 / size
    return jnp.asarray(m, BF16)


def _head_weight_expander():
    assert IDX_DIM == 64
    m = np.zeros((LANES, IDX_HEADS * LANES), np.float32)
    for h in range(IDX_HEADS):
        m[_M_WI + h, h * LANES:(h + 1) * LANES] = IDX_DIM ** -0.5
    return jnp.asarray(m, BF16)


def _prep_proj_weights(w_in, g_q_lora, w_q_up, g_kv_lora, w_uk, g_mla_qn, g_mla_qr, g_mla_kn, g_mla_kr,
                       g_dsa_q, g_dsa_k, g_attn_norm):
    offs = np.cumsum((0,) + IN_SIZES)
    sec = lambda k: w_in[:, offs[k]:offs[k + 1]]
    zeros = lambda n: jnp.zeros((D_MODEL, n), w_in.dtype)
    misc = jnp.concatenate([sec(2), sec(7), sec(8), zeros(LANES - MLA_ROPE - IDX_DIM - IDX_HEADS)], axis=1)
    w_in_r = jnp.concatenate([sec(0), sec(1), sec(3), sec(4), sec(5), sec(6), misc, sec(7), sec(7)], axis=1)
    wq = w_q_up.reshape(Q_LORA, MLA_HEADS, MLA_NOPE + MLA_ROPE)
    wq = jnp.pad(wq, ((0, 0), (0, 0), (0, LANES - MLA_NOPE - MLA_ROPE))).reshape(Q_LORA, MLA_HEADS * LANES)
    wk = jnp.pad(w_uk, ((0, 0), (0, 0), (0, LANES - MLA_NOPE))).reshape(KV_LORA, MLA_HEADS * LANES)
    pad1 = lambda v, n: jnp.pad(v, (0, n - v.shape[0]))
    gq = jnp.tile(pad1(jnp.concatenate([g_mla_qn, g_mla_qr]), LANES), MLA_HEADS).reshape(1, -1)
    gk = jnp.tile(pad1(g_mla_kn, LANES), MLA_HEADS).reshape(1, -1)
    gm = jnp.concatenate([g_mla_kr, jnp.ones((LANES - MLA_ROPE,), F32)]).reshape(1, -1)
    return dict(
        win=w_in_r.astype(BF16), wqu=wq.astype(BF16), wuk=wk.astype(BF16),
        ga=g_attn_norm.reshape(1, -1), gql=g_q_lora.reshape(1, -1), gkv=g_kv_lora.reshape(1, -1),
        gq=gq, gk=gk, gm=gm,
        gdq=jnp.tile(g_dsa_q, DSA_HEADS).reshape(1, -1), gdk=jnp.tile(g_dsa_k, DSA_HEADS).reshape(1, -1),
        bq=_block_mean_matrix([(0, MLA_NOPE), (MLA_NOPE, MLA_ROPE)]),
        b64=_block_mean_matrix([(0, DSA_HEAD_DIM), (DSA_HEAD_DIM, DSA_HEAD_DIM)]),
        bm=_block_mean_matrix([(0, MLA_ROPE)]),
        ex=_head_weight_expander(),
    )


def _rope_tables(pos):
    half = MLA_ROPE // 2
    inv = ROPE_BASE ** (-jnp.arange(half, dtype=F32) / half)
    ang = pos.astype(F32)[:, None] * inv
    cos, sin = jnp.cos(ang), jnp.sin(ang)
    cos32 = jnp.concatenate([cos, cos], axis=1)
    sin32 = jnp.concatenate([-sin, sin], axis=1)
    n = pos.shape[0]
    ones = lambda w: jnp.ones((n, w), F32)
    zeros = lambda w: jnp.zeros((n, w), F32)
    cq = jnp.concatenate([ones(MLA_NOPE), cos32, ones(LANES - MLA_NOPE - MLA_ROPE)], axis=1)
    sq = jnp.concatenate([zeros(MLA_NOPE), sin32, zeros(LANES - MLA_NOPE - MLA_ROPE)], axis=1)
    cm = jnp.concatenate([cos32, ones(LANES - MLA_ROPE)], axis=1)
    sm = jnp.concatenate([sin32, zeros(LANES - MLA_ROPE)], axis=1)
    return cq, sq, cm, sm


def _project(x2d, shift, scale, pw, tables, tm, rows_per_mod):
    n = x2d.shape[0]
    nt = n // tm
    cq, sq, cm, sm = tables
    tpos = cq.shape[0] // tm
    const = lambda shape: pl.BlockSpec(shape, lambda i: (0,) * len(shape))
    row = lambda w: pl.BlockSpec((tm, w), lambda i: (i, 0))
    if rows_per_mod:
        per = rows_per_mod // tm
        mod_spec = pl.BlockSpec((None, 1, D_MODEL), lambda i: (i // per, 0, 0))
    else:
        mod_spec = row(D_MODEL)
    tab = pl.BlockSpec((tm, LANES), lambda i: (i % tpos, 0))
    in_specs = [row(D_MODEL), mod_spec, mod_spec, const((1, D_MODEL)), const((D_MODEL, _C_END)),
                const((1, Q_LORA)), const((Q_LORA, MLA_HEADS * LANES)), const((1, KV_LORA)),
                const((KV_LORA, MLA_HEADS * LANES)), const((1, MLA_HEADS * LANES)), const((1, MLA_HEADS * LANES)),
                const((1, LANES)), const((1, DSA_OUT)), const((1, DSA_OUT)), tab, tab, tab, tab,
                const((LANES, LANES)), const((LANES, LANES)), const((LANES, LANES)),
                const((LANES, IDX_HEADS * LANES))]
    widths = [(MLA_HEADS * LANES, BF16), (MLA_HEADS * LANES, BF16), (KV_LORA, F32), (KV_LORA, BF16), (LANES, F32),
              (DSA_OUT, F32), (DSA_OUT, F32), (DSA_OUT, BF16), (DSA_OUT, F32), (DSA_OUT, BF16),
              (IDX_HEADS * IDX_DIM, BF16), (LANES, BF16), (IDX_HEADS * LANES, F32)]
    outs = pl.pallas_call(
        _proj_kernel,
        grid=(nt,),
        in_specs=in_specs,
        out_specs=[row(w) for w, _ in widths],
        out_shape=[jax.ShapeDtypeStruct((n, w), dt) for w, dt in widths],
        compiler_params=_cparams(("arbitrary",)),
        name="project",
    )(x2d, shift, scale, pw["ga"], pw["win"], pw["gql"], pw["wqu"], pw["gkv"], pw["wuk"],
      pw["gq"], pw["gk"], pw["gm"], pw["gdq"], pw["gdk"], cq, sq, cm, sm, pw["bq"], pw["b64"], pw["bm"],
      pw["ex"])
    names = ("qall", "kall", "ckv", "ckvb", "misc", "qd", "kd", "kdb", "vd", "vdb", "qi", "kid", "wib")
    return dict(zip(names, outs))


_MLA_TQ = 256
_MLA_TK = 512


def _mla_prompt_kernel(q_ref, k_ref, c_ref, wuv_ref, o_ref, s_ref):
    tq, tk = _MLA_TQ, _MLA_TK
    i = pl.program_id(1)
    q0 = i * tq
    nkb = (q0 + tq + tk - 1) // tk
    row = q0 + lax.broadcasted_iota(I32, (tq, tk), 0)
    col0 = lax.broadcasted_iota(I32, (tq, tk), 1)
    heads = []
    for pr in range(MLA_HEADS // 2):
        qhs = [q_ref[:, (2 * pr + half) * LANES:(2 * pr + half + 1) * LANES] for half in range(2)]

        def pass_a(j, mrun, pr=pr, qhs=qhs):
            k0 = pl.multiple_of(j * tk, tk)
            causal = col0 + k0 <= row
            out = []
            for half in range(2):
                h = 2 * pr + half
                s = _nt_dot(qhs[half], k_ref[pl.ds(k0, tk), h * LANES:(h + 1) * LANES]) * MLA_SCALE
                s = jnp.where(causal, s, NEG_INF)
                s_ref[half, :, pl.ds(k0, tk)] = s
                out.append(jnp.maximum(mrun[half], _fold_lanes(s, jnp.maximum)))
            return tuple(out)

        ninf = jnp.full((tq, LANES), NEG_INF, F32)
        mrun = lax.fori_loop(0, nkb, pass_a, (ninf, ninf))
        ms = [jnp.max(m, axis=1, keepdims=True) for m in mrun]

        def pass_b(j, carry, ms=ms):
            k0 = pl.multiple_of(j * tk, tk)
            cb = c_ref[pl.ds(k0, tk), :]
            out = []
            for half in range(2):
                l, acc = carry[half]
                p = jnp.exp(s_ref[half, :, pl.ds(k0, tk)] - ms[half])
                out.append((l + _fold_lanes(p, jnp.add),
                            acc + jnp.dot(p.astype(BF16), cb, preferred_element_type=F32)))
            return tuple(out)

        zero = jnp.zeros((tq, LANES), F32)
        for l, acc in lax.fori_loop(0, nkb, pass_b, ((zero, zero), (zero, zero))):
            heads.append((acc / jnp.sum(l, axis=1, keepdims=True)).astype(BF16))
    for pr in range(MLA_HEADS // 2):
        o = (jnp.dot(heads[2 * pr], wuv_ref[2 * pr], preferred_element_type=F32)
             + jnp.dot(heads[2 * pr + 1], wuv_ref[2 * pr + 1], preferred_element_type=F32))
        o_ref[:, pr * LANES:(pr + 1) * LANES] = o.astype(BF16)


def _pad_wuv(w_uv):
    w = jnp.transpose(w_uv, (1, 0, 2))
    even = jnp.pad(w, ((0, 0), (0, 0), (0, LANES - MLA_V)))
    odd = jnp.pad(w, ((0, 0), (0, 0), (LANES - MLA_V, 0)))
    is_odd = (jnp.arange(MLA_HEADS) % 2 == 1)[:, None, None]
    return jnp.where(is_odd, odd, even).astype(BF16)


def _mla_prompt(qall, kall, ckvb, wuv_pad):
    b, t, _ = qall.shape
    tq = _MLA_TQ
    assert t % _MLA_TK == 0
    return pl.pallas_call(
        _mla_prompt_kernel,
        grid=(b, t // tq),
        in_specs=[pl.BlockSpec((None, tq, MLA_HEADS * LANES), lambda bi, i: (bi, i, 0)),
                  pl.BlockSpec((None, t, MLA_HEADS * LANES), lambda bi, i: (bi, 0, 0)),
                  pl.BlockSpec((None, t, KV_LORA), lambda bi, i: (bi, 0, 0)),
                  pl.BlockSpec((MLA_HEADS, KV_LORA, LANES), lambda bi, i: (0, 0, 0))],
        out_specs=pl.BlockSpec((None, tq, MLA_OUT), lambda bi, i: (bi, i, 0)),
        out_shape=jax.ShapeDtypeStruct((b, t, MLA_OUT), BF16),
        scratch_shapes=[pltpu.VMEM((2, tq, t), F32)],
        compiler_params=_cparams(("arbitrary", "arbitrary")),
        name="mla_prompt",
    )(qall, kall, ckvb, wuv_pad)


_DSA_T = 128
_DSA_TQ = 256
_DSA_CW = 512


def _sortable_key(score):
    bits = pltpu.bitcast(score, I32)
    key = jnp.where(bits < 0, bits ^ jnp.int32(0x7FFFFFFF), bits)
    return jnp.where(score == 0.0, 0, key)


def _bias_tiles(rb_ref, tz_ref):
    t = _DSA_T
    r = lax.broadcasted_iota(I32, (t, t), 0)
    c = lax.broadcasted_iota(I32, (t, t), 1)
    exact = REL_BUCKETS // 2
    for which in range(2):
        d = jnp.maximum(r - c + t * which, 0)
        logd = jnp.log(jnp.maximum(d, 1).astype(F32) / exact) / math.log(REL_MAX_DIST / exact)
        far = jnp.minimum(exact + (logd * (REL_BUCKETS - exact)).astype(I32), REL_BUCKETS - 1)
        bucket = jnp.where(d < exact, d, far)
        for h in range(DSA_HEADS):
            tile = jnp.zeros((t, t), F32)
            for bk in range(REL_BUCKETS):
                tile = jnp.where(bucket == bk, rb_ref[bk, h], tile)
            tz_ref[h, which] = tile


def _select_mask(keys_ref, mask_ref, tri_ref, nch, n_sel, rows):
    cw = _DSA_CW

    def count(pred):
        def body(j, acc):
            kb = keys_ref[:, pl.ds(pl.multiple_of(j * cw, cw), cw)]
            return acc + _fold_lanes(jnp.where(pred(kb), 1.0, 0.0), jnp.add)
        acc = lax.fori_loop(0, nch, body, jnp.zeros((rows, LANES), F32))
        return jnp.sum(acc, axis=1, keepdims=True)

    kf = jnp.float32(n_sel)
    zero = jnp.zeros((rows, 1), I32)
    thr = jnp.where(count(lambda kb: kb >= zero) >= kf, zero, jnp.full((rows, 1), INT_MIN, I32))

    def bit_body(bi, thr):
        cand = thr + lax.shift_left(jnp.int32(1), 30 - bi)
        return jnp.where(count(lambda kb: kb >= cand) >= kf, cand, thr)

    thr = lax.fori_loop(0, 31, bit_body, thr)
    need = kf - count(lambda kb: kb > thr)

    def mask_body(j, carry):
        k0 = pl.multiple_of(j * cw, cw)
        kb = keys_ref[:, pl.ds(k0, cw)]
        eq = kb == thr
        eqf = jnp.where(eq, 1.0, 0.0)
        before = jnp.dot(eqf.astype(BF16), tri_ref[...], preferred_element_type=F32) + carry
        keep = ((kb > thr) | (eq & (before < need))) & (kb > INT_MIN)
        mask_ref[:, pl.ds(k0, cw)] = jnp.where(keep, 0.0, NEG_INF)
        return carry + jnp.sum(eqf, axis=1, keepdims=True)

    lax.fori_loop(0, nch, mask_body, jnp.zeros((rows, 1), F32))


def _dsa_prompt_kernel(rb_ref, qi_ref, wib_ref, kid_ref, qd_ref, kd_ref, vd_ref, tri_ref, o_ref,
                       keys_ref, mask_ref, tz_ref, s_ref, *, n_sel):
    t, tq, cw = _DSA_T, _DSA_TQ, _DSA_CW
    sub_n = cw // t
    q_sub = tq // t
    bi = pl.program_id(0)
    i = pl.program_id(1)
    nch = ((i + 1) * tq + cw - 1) // cw

    @pl.when((bi == 0) & (i == 0))
    def _():
        _bias_tiles(rb_ref, tz_ref)

    row = i * tq + lax.broadcasted_iota(I32, (tq, cw), 0)
    col0 = lax.broadcasted_iota(I32, (tq, cw), 1)
    lane = lax.broadcasted_iota(I32, (tq, LANES), 1)
    low = lane < DSA_HEAD_DIM

    def head_halves(ref, pr):
        qs = ref[:, pr * LANES:(pr + 1) * LANES].astype(BF16)
        zero = jnp.zeros_like(qs)
        return jnp.where(low, qs, zero), jnp.where(low, zero, qs)

    qim = [q for pr in range(IDX_HEADS // 2) for q in head_halves(qi_ref, pr)]

    def idx_body(c, _):
        k0 = pl.multiple_of(c * cw, cw)
        kk = kid_ref[pl.ds(k0, cw), :]
        acc = jnp.zeros((tq, cw), F32)
        for h in range(IDX_HEADS):
            r = jnp.maximum(_nt_dot(qim[h], kk), 0.0)
            w = wib_ref[:, h * LANES:(h + 1) * LANES]
            acc = acc + jnp.concatenate([w] * sub_n, axis=1) * r
        keys_ref[:, pl.ds(k0, cw)] = jnp.where(col0 + k0 <= row, _sortable_key(acc), INT_MIN)
        return 0

    lax.fori_loop(0, nch, idx_body, 0)
    few_keys = (i + 1) * tq <= n_sel

    @pl.when(few_keys)
    def _():
        def keep_all(c, _):
            k0 = pl.multiple_of(c * cw, cw)
            mask_ref[:, pl.ds(k0, cw)] = jnp.where(keys_ref[:, pl.ds(k0, cw)] > INT_MIN, 0.0, NEG_INF)
            return 0
        lax.fori_loop(0, nch, keep_all, 0)

    @pl.when(jnp.logical_not(few_keys))
    def _():
        _select_mask(keys_ref, mask_ref, tri_ref, nch, n_sel, tq)

    for pr in range(DSA_HEADS // 2):
        qms = head_halves(qd_ref, pr)
        fars = [rb_ref[REL_BUCKETS - 1, 2 * pr + half] for half in range(2)]

        def pass_a(c, mrun, pr=pr, qms=qms, fars=fars):
            k0 = pl.multiple_of(c * cw, cw)
            kb = kd_ref[pl.ds(k0, cw), pr * LANES:(pr + 1) * LANES]
            mk = mask_ref[:, pl.ds(k0, cw)]
            out = []
            for half in range(2):
                h = 2 * pr + half
                s = _nt_dot(qms[half], kb) * DSA_SCALE
                bands = []
                for a in range(q_sub):
                    qblk = i * q_sub + a
                    parts = []
                    for sub in range(sub_n):
                        blk = c * sub_n + sub
                        bias = jnp.where(blk == qblk, tz_ref[h, 0],
                                         jnp.where(blk == qblk - 1, tz_ref[h, 1], fars[half]))
                        parts.append(s[a * t:(a + 1) * t, sub * t:(sub + 1) * t] + bias)
                    bands.append(jnp.concatenate(parts, axis=1))
                s = jnp.concatenate(bands, axis=0) + mk
                s_ref[half, :, pl.ds(k0, cw)] = s
                out.append(jnp.maximum(mrun[half], _fold_lanes(s, jnp.maximum)))
            return tuple(out)

        ninf = jnp.full((tq, LANES), NEG_INF, F32)
        mrun = lax.fori_loop(0, nch, pass_a, (ninf, ninf))
        ms = [jnp.max(m, axis=1, keepdims=True) for m in mrun]

        def pass_b(c, carry, pr=pr, ms=ms):
            k0 = pl.multiple_of(c * cw, cw)
            vb = vd_ref[pl.ds(k0, cw), pr * LANES:(pr + 1) * LANES]
            out = []
            for half in range(2):
                l, acc = carry[half]
                p = jnp.exp(s_ref[half, :, pl.ds(k0, cw)] - ms[half])
                out.append((l + _fold_lanes(p, jnp.add),
                            acc + jnp.dot(p.astype(BF16), vb, preferred_element_type=F32)))
            return tuple(out)

        zero = jnp.zeros((tq, LANES), F32)
        (l0, a0), (l1, a1) = lax.fori_loop(0, nch, pass_b, ((zero, zero), (zero, zero)))
        o0 = a0 / jnp.sum(l0, axis=1, keepdims=True)
        o1 = a1 / jnp.sum(l1, axis=1, keepdims=True)
        o_ref[:, pr * LANES:(pr + 1) * LANES] = jnp.where(low, o0, o1).astype(BF16)


def _strict_upper(n):
    return jnp.asarray(np.triu(np.ones((n, n), np.float32), 1), BF16)


def _dsa_prompt(rel_bias, qi, wib, kid, qd, kdb, vdb):
    b, t, _ = qi.shape
    tq, cw = _DSA_TQ, _DSA_CW
    assert t % cw == 0 and t % tq == 0 and tq % _DSA_T == 0
    n_sel = min(IDX_TOPK_MAX, t // 4)
    blk = lambda w: pl.BlockSpec((None, tq, w), lambda bi, i: (bi, i, 0))
    full = lambda w: pl.BlockSpec((None, t, w), lambda bi, i: (bi, 0, 0))
    return pl.pallas_call(
        functools.partial(_dsa_prompt_kernel, n_sel=n_sel),
        grid=(b, t // tq),
        in_specs=[pl.BlockSpec(memory_space=pltpu.SMEM),
                  blk(IDX_HEADS * IDX_DIM), blk(IDX_HEADS * LANES), full(LANES), blk(DSA_OUT), full(DSA_OUT),
                  full(DSA_OUT), pl.BlockSpec((cw, cw), lambda bi, i: (0, 0))],
        out_specs=blk(DSA_OUT),
        out_shape=jax.ShapeDtypeStruct((b, t, DSA_OUT), BF16),
        scratch_shapes=[pltpu.VMEM((tq, t), I32), pltpu.VMEM((tq, t), F32),
                        pltpu.VMEM((DSA_HEADS, 2, _DSA_T, _DSA_T), F32), pltpu.VMEM((2, tq, t), F32)],
        compiler_params=_cparams(("arbitrary", "arbitrary")),
        name="dsa_prompt",
    )(rel_bias, qi, wib, kid, qd, kdb, vdb, _strict_upper(cw))


_EXPERTS_PER_GROUP = N_EXPERTS // N_GROUPS
_MOE_GROUP = 4


def _first_index_of_max(v, idx, axis, sentinel):
    mx = jnp.max(v, axis=axis, keepdims=True)
    first = jnp.min(jnp.where(v == mx, idx, sentinel), axis=axis, keepdims=True)
    return mx, first


def _route(logits_t, bias_col):
    n_tok = logits_t.shape[1]
    scores = jax.nn.sigmoid(logits_t)
    biased = scores + bias_col
    b3 = biased.reshape(N_GROUPS, _EXPERTS_PER_GROUP, n_tok)
    j3 = lax.broadcasted_iota(I32, b3.shape, 1)
    m1, f1 = _first_index_of_max(b3, j3, 1, _EXPERTS_PER_GROUP)
    m2 = jnp.max(jnp.where(j3 == f1, NEG_INF, b3), axis=1, keepdims=True)
    gs = (m1 + m2).reshape(N_GROUPS, n_tok)
    gi = lax.broadcasted_iota(I32, gs.shape, 0)
    gsel = jnp.zeros(gs.shape, jnp.bool_)
    for _ in range(TOPK_GROUPS):
        _, first = _first_index_of_max(gs, gi, 0, N_GROUPS)
        hit = gi == first
        gsel = gsel | hit
        gs = jnp.where(hit, NEG_INF, gs)
    gsel3 = jnp.broadcast_to(gsel.reshape(N_GROUPS, 1, n_tok), b3.shape)
    masked = jnp.where(gsel3, b3, NEG_INF).reshape(N_EXPERTS, n_tok)
    ei = lax.broadcasted_iota(I32, masked.shape, 0)
    sel = jnp.zeros(masked.shape, jnp.bool_)
    for _ in range(TOP_K):
        _, first = _first_index_of_max(masked, ei, 0, N_EXPERTS)
        hit = ei == first
        sel = sel | hit
        masked = jnp.where(hit, NEG_INF, masked)
    w = jnp.where(sel, scores, 0.0)
    gate = w / jnp.sum(w, axis=0, keepdims=True) * ROUTED_SCALE
    return sel, gate


def _swiglu_bf(x_bf, wgu_ref, wdown_ref, d_hidden):
    gu = jnp.dot(x_bf, wgu_ref[...], preferred_element_type=F32)
    g, u = gu[:, :d_hidden], gu[:, d_hidden:]
    act = (g * jax.nn.sigmoid(g)) * u
    return jnp.dot(act.astype(BF16), wdown_ref[...], preferred_element_type=F32)


def _moe_kernel(x_ref, oa_ref, ob_ref, ga_ref, sf_ref, cf_ref, gf_ref, gn_ref, wo_ref, wr_ref, br_ref,
                wsg_ref, wsd_ref, tri_ref, weg_ref, wed_ref, y_ref, xt_ref, acc_ref, gate_ref, rank_ref,
                *, cap):
    tm = x_ref.shape[0]
    e = pl.program_id(1)

    @pl.when(e == 0)
    def _():
        half = oa_ref.shape[1]
        attn = (jnp.dot(oa_ref[...], wo_ref[:half, :], preferred_element_type=F32)
                + jnp.dot(ob_ref[...], wo_ref[half:, :], preferred_element_type=F32))
        x1 = x_ref[...] + ga_ref[...] * attn
        y_ref[...] = x1
        hn = x1 * lax.rsqrt(jnp.mean(x1 * x1, axis=-1, keepdims=True) + EPS) * gn_ref[...]
        xt = (hn * (1.0 + cf_ref[...]) + sf_ref[...]).astype(BF16)
        xt_ref[...] = xt
        sel, gate = _route(_nt_dot(wr_ref[...], xt), br_ref[...])
        ind = jnp.where(sel, 1.0, 0.0)
        before = jnp.dot(ind.astype(BF16), tri_ref[...], preferred_element_type=F32)
        rank_ref[...] = jnp.where(sel, before, -1.0)
        gate_ref[...] = jnp.where(sel, gate, 0.0)
        acc_ref[...] = jnp.zeros_like(acc_ref)

    grp = _MOE_GROUP
    rank_rows = [rank_ref[pl.ds(e * grp + j, 1), :] for j in range(grp)]
    gate_rows = [gate_ref[pl.ds(e * grp + j, 1), :] for j in range(grp)]
    top = rank_rows[0]
    for j in range(1, grp):
        top = jnp.maximum(top, rank_rows[j])
    count = (jnp.max(top) + 1.0).astype(I32)
    n_chunks = (count + cap - 1) // cap
    slot = lax.broadcasted_iota(I32, (cap, tm), 0).astype(F32)

    def chunk(c, _):
        base = slot + (c * cap).astype(F32)
        picks = [jnp.where(base == rank_rows[j], 1.0, 0.0) for j in range(grp)]
        pick_b = jnp.concatenate(picks, axis=0).astype(BF16)
        xe = jnp.dot(pick_b, xt_ref[...], preferred_element_type=F32).astype(BF16)
        ys = []
        for j in range(grp):
            ye = _swiglu_bf(xe[j * cap:(j + 1) * cap], weg_ref.at[j], wed_ref.at[j], D_EXPERT)
            ge = jnp.sum(picks[j] * gate_rows[j], axis=1, keepdims=True)
            ys.append((ye * ge).astype(BF16))
        acc_ref[...] += _tn_dot(pick_b, jnp.concatenate(ys, axis=0))
        return 0

    lax.fori_loop(0, n_chunks, chunk, 0)

    @pl.when(e == N_EXPERTS // grp - 1)
    def _():
        shared = _swiglu_bf(xt_ref[...], wsg_ref, wsd_ref, D_SHARED)
        y_ref[...] = y_ref[...] + gf_ref[...] * (acc_ref[...] + shared)


def _moe(x2d, oa, ob, mods, g_ffn, wo_bf, wr_t_bf, b_router, wsg_bf, wsd_bf, weg_bf, wed_bf, tm, rows_per_mod):
    n = x2d.shape[0]
    assert n % tm == 0 and tm % LANES == 0
    nt = n // tm
    cap = LANES
    const = lambda shape: pl.BlockSpec(shape, lambda t, e: (0,) * len(shape))
    row = lambda w: pl.BlockSpec((tm, w), lambda t, e: (t, 0))
    if rows_per_mod:
        per = rows_per_mod // tm
        mod_spec = pl.BlockSpec((None, 1, D_MODEL), lambda t, e: (t // per, 0, 0))
    else:
        mod_spec = row(D_MODEL)
    half = oa.shape[1]
    return pl.pallas_call(
        functools.partial(_moe_kernel, cap=cap),
        grid=(nt, N_EXPERTS // _MOE_GROUP),
        in_specs=[row(D_MODEL), row(half), row(half), mod_spec, mod_spec, mod_spec, mod_spec,
                  const((1, D_MODEL)), const((2 * half, D_MODEL)), const((N_EXPERTS, D_MODEL)),
                  const((N_EXPERTS, 1)), const((D_MODEL, 2 * D_SHARED)), const((D_SHARED, D_MODEL)),
                  const((tm, tm)),
                  pl.BlockSpec((_MOE_GROUP, D_MODEL, 2 * D_EXPERT), lambda t, e: (e, 0, 0)),
                  pl.BlockSpec((_MOE_GROUP, D_EXPERT, D_MODEL), lambda t, e: (e, 0, 0))],
        out_specs=row(D_MODEL),
        out_shape=jax.ShapeDtypeStruct((n, D_MODEL), F32),
        scratch_shapes=[pltpu.VMEM((tm, D_MODEL), BF16), pltpu.VMEM((tm, D_MODEL), F32),
                        pltpu.VMEM((N_EXPERTS, tm), F32), pltpu.VMEM((N_EXPERTS, tm), F32)],
        compiler_params=_cparams(("arbitrary", "arbitrary")),
        name="moe",
    )(x2d, oa, ob, *mods, g_ffn.reshape(1, -1), wo_bf, wr_t_bf, b_router.reshape(-1, 1), wsg_bf, wsd_bf,
      _strict_upper(tm), weg_bf, wed_bf)


def _col_blocks(q):
    nh = q.shape[0]
    r = lax.broadcasted_iota(I32, (nh, LANES), 0)
    c = lax.broadcasted_iota(I32, (nh, LANES), 1)
    blocks = [_tn_dot(q, jnp.where((r == h) & (c == h), 1.0, 0.0).astype(q.dtype)) for h in range(nh)]
    return jnp.concatenate(blocks, axis=0)


def _rows_to_cols(row):
    r8 = jnp.broadcast_to(row, (8, row.shape[1]))
    e0 = jnp.where(lax.broadcasted_iota(I32, (8, LANES), 0) == 0, 1.0, 0.0).astype(row.dtype)
    return _tn_dot(r8, e0)


def _head_diag(acc, width):
    r = lax.broadcasted_iota(I32, acc.shape, 0)
    c = lax.broadcasted_iota(I32, acc.shape, 1)
    return jnp.sum(jnp.where(r == c // width, acc, 0.0), axis=0, keepdims=True)


class _PageStream:
    def __init__(self, hbm_ref, buf_ref, sem_ref, pt_ref, seq, pages_per_chunk):
        self.hbm, self.buf, self.sem, self.pt, self.seq, self.ppc = hbm_ref, buf_ref, sem_ref, pt_ref, seq, pages_per_chunk

    def _copy(self, page, slot, r):
        return pltpu.make_async_copy(self.hbm.at[page], self.buf.at[slot, r], self.sem.at[slot])

    def start(self, chunk, slot):
        for r in range(self.ppc):
            self._copy(self.pt[self.seq, chunk * self.ppc + r], slot, r).start()

    def wait(self, slot):
        for r in range(self.ppc):
            self._copy(0, slot, r).wait()


_STREAM_DEPTH = 8


def _stream_prime(streams, n_chunks):
    for d in range(min(_STREAM_DEPTH - 1, n_chunks)):
        for st in streams:
            st.start(d, d)


def _stream_loop(streams, n_chunks, body, primed=False):
    depth = _STREAM_DEPTH
    if not primed:
        _stream_prime(streams, n_chunks)

    def step(c, _):
        slot = c % depth
        nxt = c + depth - 1

        @pl.when(nxt < n_chunks)
        def _():
            for st in streams:
                st.start(nxt, nxt % depth)

        for st in streams:
            st.wait(slot)
        body(c, slot)
        return 0

    lax.fori_loop(0, n_chunks, step, 0)


def _softmax_stats(sc_ref, n_rows, blk):
    nb = n_rows // blk
    tail = n_rows - nb * blk

    def mx_body(j, m):
        return jnp.maximum(m, jnp.max(sc_ref[pl.ds(pl.multiple_of(j * blk, blk), blk), :], axis=0, keepdims=True))

    m = lax.fori_loop(0, nb, mx_body, jnp.full((1, LANES), NEG_INF, F32))
    if tail:
        m = jnp.maximum(m, jnp.max(sc_ref[pl.ds(nb * blk, tail), :], axis=0, keepdims=True))

    def sum_body(j, l):
        return l + jnp.sum(jnp.exp(sc_ref[pl.ds(pl.multiple_of(j * blk, blk), blk), :] - m), axis=0, keepdims=True)

    l = lax.fori_loop(0, nb, sum_body, jnp.zeros((1, LANES), F32))
    if tail:
        l = l + jnp.sum(jnp.exp(sc_ref[pl.ds(nb * blk, tail), :] - m), axis=0, keepdims=True)
    return m, l


_MLA_S_PAGES = 8


def _mla_sample_kernel(pt_ref, q_ref, knew_ref, cnew_ref, wuk_ref, gk_ref, wuv_ref, lat_hbm, kr_hbm, o_ref,
                       latbuf, krbuf, sems, sc_ref, latbf_ref, *, n_pages):
    s_id = pl.program_id(0)
    ppc = math.gcd(n_pages, _MLA_S_PAGES)
    ck = ppc * PAGE_SIZE
    n_chunks = n_pages // ppc
    past = n_pages * PAGE_SIZE
    row8 = lax.broadcasted_iota(I32, (8, LANES), 0)

    qblk = _col_blocks(q_ref[...]).astype(BF16)
    qr = qblk[MLA_NOPE:MLA_NOPE + MLA_ROPE, :].astype(F32)
    for h in range(1, MLA_HEADS):
        qr = qr + qblk[h * LANES + MLA_NOPE:h * LANES + MLA_NOPE + MLA_ROPE, :].astype(F32)
    qr = qr.astype(BF16)
    lat_stream = _PageStream(lat_hbm, latbuf, sems.at[0], pt_ref, s_id, ppc)
    kr_stream = _PageStream(kr_hbm, krbuf, sems.at[1], pt_ref, s_id, ppc)

    def score_chunk(c, slot):
        lat = latbuf[slot].reshape(ck, KV_LORA).astype(BF16)
        latbf_ref[pl.ds(pl.multiple_of(c * ck, ck), ck), :] = lat
        kraw = jnp.dot(lat, wuk_ref[...], preferred_element_type=F32)
        slabs = []
        for h in range(MLA_HEADS):
            x = kraw[:, h * LANES:(h + 1) * LANES]
            ms = jnp.sum(x * x, axis=1, keepdims=True) * (1.0 / MLA_NOPE)
            slabs.append((x * lax.rsqrt(ms + EPS) * gk_ref[:, h * LANES:(h + 1) * LANES]).astype(BF16))
        kn = jnp.concatenate(slabs, axis=1)
        s_rope = jnp.concatenate([_tn_dot(krbuf[slot, r].astype(BF16), qr) for r in range(ppc)], axis=0)
        s = (jnp.dot(kn, qblk, preferred_element_type=F32) + s_rope) * MLA_SCALE
        sc_ref[pl.ds(pl.multiple_of(c * ck, ck), ck), :] = s

    _stream_loop((lat_stream, kr_stream), n_chunks, score_chunk)
    s_new = jnp.dot(jnp.broadcast_to(knew_ref[...], (8, knew_ref.shape[1])), qblk,
                    preferred_element_type=F32) * MLA_SCALE
    sc_ref[pl.ds(past, 8), :] = jnp.where(row8 == 0, s_new, NEG_INF)
    m, l = _softmax_stats(sc_ref, past + 8, ck)

    def pv_chunk(c, acc):
        k0 = pl.multiple_of(c * ck, ck)
        p = jnp.exp(sc_ref[pl.ds(k0, ck), :] - m) / l
        return acc + _tn_dot(p.astype(BF16), latbf_ref[pl.ds(k0, ck), :])

    acc = lax.fori_loop(0, n_chunks, pv_chunk, jnp.zeros((LANES, KV_LORA), F32))
    p_new = jnp.exp(sc_ref[pl.ds(past, 8), :] - m) / l
    acc = acc + _tn_dot(p_new.astype(BF16), jnp.broadcast_to(cnew_ref[...], (8, KV_LORA)))
    out = jnp.dot(acc.astype(BF16), wuv_ref[...], preferred_element_type=F32)
    o_ref[...] = _head_diag(out, MLA_V).astype(BF16)


def _mla_sample(page_table, q8, knew, cnew, wuk_pad, gk, wuv_flat, cache_lat, cache_kr):
    db, n_pages = page_table.shape
    ppc = math.gcd(n_pages, _MLA_S_PAGES)
    past = n_pages * PAGE_SIZE
    per_seq = lambda shape: pl.BlockSpec((None,) + shape, lambda s, pt: (s,) + (0,) * len(shape))
    const = lambda shape: pl.BlockSpec(shape, lambda s, pt: (0,) * len(shape))
    grid_spec = pltpu.PrefetchScalarGridSpec(
        num_scalar_prefetch=1, grid=(db,),
        in_specs=[per_seq((MLA_HEADS, LANES)), per_seq((1, MLA_HEADS * LANES)), per_seq((1, KV_LORA)),
                  const((KV_LORA, MLA_HEADS * LANES)), const((1, MLA_HEADS * LANES)), const((KV_LORA, MLA_OUT)),
                  pl.BlockSpec(memory_space=pl.ANY), pl.BlockSpec(memory_space=pl.ANY)],
        out_specs=per_seq((1, MLA_OUT)),
        scratch_shapes=[pltpu.VMEM((_STREAM_DEPTH, ppc, PAGE_SIZE, KV_LORA), F32),
                        pltpu.VMEM((_STREAM_DEPTH, ppc, MLA_ROPE, PAGE_SIZE), F32),
                        pltpu.SemaphoreType.DMA((2, _STREAM_DEPTH)), pltpu.VMEM((past + 8, LANES), F32),
                        pltpu.VMEM((past, KV_LORA), BF16)])
    return pl.pallas_call(
        functools.partial(_mla_sample_kernel, n_pages=n_pages),
        grid_spec=grid_spec,
        out_shape=jax.ShapeDtypeStruct((db, 1, MLA_OUT), BF16),
        compiler_params=_cparams(("arbitrary",)),
        name="mla_sample",
    )(page_table, q8, knew, cnew, wuk_pad, gk, wuv_flat, cache_lat, cache_kr)


_DSA_S_IDX_PAGES = 16


def _select_flat(keys, tri_u, tri_l, n_sel):
    def count(pred):
        c = jnp.sum(jnp.where(pred, 1.0, 0.0), axis=1, keepdims=True)
        return jnp.sum(c, axis=0, keepdims=True)

    kf = jnp.float32(n_sel)
    zero = jnp.zeros((1, 1), I32)
    thr = jnp.where(count(keys >= zero) >= kf, zero, jnp.full((1, 1), INT_MIN, I32))

    def bit_body(bi, thr):
        cand = thr + lax.shift_left(jnp.int32(1), 30 - bi)
        return jnp.where(count(keys >= cand) >= kf, cand, thr)

    thr = lax.fori_loop(0, 31, bit_body, thr)
    need = kf - count(keys > thr)
    eq = keys == thr
    eqf = jnp.where(eq, 1.0, 0.0)
    within = jnp.dot(eqf.astype(BF16), tri_u, preferred_element_type=F32)
    rowcount = jnp.broadcast_to(jnp.sum(eqf, axis=1, keepdims=True), eqf.shape)
    carry = jnp.dot(tri_l, rowcount.astype(BF16), preferred_element_type=F32)
    keep = ((keys > thr) | (eq & (within + carry < need))) & (keys > INT_MIN)
    return jnp.where(keep, 1.0, 0.0)


_DSA_S_KV_PAGES = 4


def _round_bf16(x):
    return x.astype(BF16).astype(F32)


def _dsa_sample_kernel(pt_ref, rbt_ref, qi_ref, wi_ref, kinew_ref, qd_ref, kdnew_ref, vdnew_ref,
                       triu_ref, tril_ref, idx_hbm, k_hbm, v_hbm, o_ref,
                       idxbuf, kvbuf, vbuf, sems, keys_ref, mask_ref, s_ref, qcol_ref, acc_ref, *, n_pages, n_sel):
    s_id = pl.program_id(0)
    ipc = math.gcd(n_pages, _DSA_S_IDX_PAGES)
    kpc = math.gcd(n_pages, _DSA_S_KV_PAGES)
    lane1 = lax.broadcasted_iota(I32, (1, LANES), 1)
    qi = qi_ref[...]
    wi = _round_bf16(wi_ref[...])

    keys_ref[...] = jnp.full(keys_ref.shape, INT_MIN, I32)
    idx_stream = _PageStream(idx_hbm, idxbuf, sems.at[0], pt_ref, s_id, ipc)

    def idx_chunk(c, slot):
        for r in range(ipc):
            kk = idxbuf[slot, r].astype(BF16)
            rr = jnp.maximum(jnp.dot(qi, kk, preferred_element_type=F32) * (IDX_DIM ** -0.5), 0.0)
            sc = jnp.sum(wi * _round_bf16(rr), axis=0, keepdims=True)
            keys_ref[pl.ds(c * ipc + r, 1), :] = _sortable_key(sc)

    _stream_loop((idx_stream,), n_pages // ipc, idx_chunk)
    k_stream = _PageStream(k_hbm, kvbuf, sems.at[1], pt_ref, s_id, kpc)
    v_stream = _PageStream(v_hbm, vbuf, sems.at[2], pt_ref, s_id, kpc)
    _stream_prime((k_stream,), n_pages // kpc)
    r_new = jnp.maximum(jnp.sum(qi.astype(F32) * kinew_ref[...].astype(F32), axis=1, keepdims=True)
                        * (IDX_DIM ** -0.5), 0.0)
    sc_new = jnp.sum(wi * _round_bf16(r_new), axis=0, keepdims=True)
    keys_ref[pl.ds(n_pages, 1), :] = jnp.where(lane1 == 0, _sortable_key(jnp.broadcast_to(sc_new, (1, LANES))),
                                               INT_MIN)

    mask_ref[...] = _select_flat(keys_ref[...], triu_ref[...], tril_ref[...], n_sel)

    qd = qd_ref[...]
    row8 = lax.broadcasted_iota(I32, (8, LANES), 0)
    e0 = jnp.where(row8 == 0, 1.0, 0.0).astype(BF16)
    for h in range(DSA_HEADS):
        resid = jnp.broadcast_to(qd[h:h + 1, :], (8, DSA_HEAD_DIM))
        col = jnp.zeros((DSA_HEAD_DIM, LANES), F32)
        for _ in range(3):
            piece = resid.astype(BF16)
            col = col + _tn_dot(piece, e0)
            resid = resid - piece.astype(F32)
        qcol_ref[h] = col

    far_col = rbt_ref[:, REL_BUCKETS - 1:REL_BUCKETS]
    exact = REL_BUCKETS // 2
    d_row = PAGE_SIZE - lane1
    logd = jnp.log(jnp.maximum(d_row, 1).astype(F32) / exact) / math.log(REL_MAX_DIST / exact)
    far_b = jnp.minimum(exact + (logd * (REL_BUCKETS - exact)).astype(I32), REL_BUCKETS - 1)
    bucket = jnp.where(d_row < exact, d_row, far_b)
    near = jnp.zeros((DSA_HEADS, LANES), F32)
    for bk in range(REL_BUCKETS):
        near = jnp.where(bucket == bk, rbt_ref[:, bk:bk + 1], near)

    _stream_prime((v_stream,), n_pages // kpc)

    def k_chunk(c, slot):
        for r in range(kpc):
            page = c * kpc + r
            for h in range(DSA_HEADS):
                s_ref[page, h:h + 1, :] = jnp.sum(kvbuf[slot, r, h] * qcol_ref[h], axis=0, keepdims=True)
            bias = jnp.where(page == n_pages - 1, near, far_col)
            keep = mask_ref[pl.ds(page, 1), :] > 0.5
            s_ref[page] = jnp.where(keep, s_ref[page] * DSA_SCALE + bias, NEG_INF)

    _stream_loop((k_stream,), n_pages // kpc, k_chunk, primed=True)
    s_new = (jnp.sum(qd * kdnew_ref[...], axis=1, keepdims=True) * DSA_SCALE
             + rbt_ref[:, 0:1])
    keep_new = mask_ref[pl.ds(n_pages, 1), :][:, 0:1] > 0.5
    s_new = jnp.where(keep_new, s_new, NEG_INF)

    blk = 8
    def mx_body(j, m):
        return jnp.maximum(m, jnp.max(s_ref[pl.ds(pl.multiple_of(j * blk, blk), blk)], axis=0))

    m_t = lax.fori_loop(0, n_pages // blk, mx_body, jnp.full((DSA_HEADS, LANES), NEG_INF, F32))
    m = jnp.maximum(jnp.max(m_t, axis=1, keepdims=True), s_new)

    def sum_body(j, l):
        return l + jnp.sum(jnp.exp(s_ref[pl.ds(pl.multiple_of(j * blk, blk), blk)] - m), axis=0)

    l_t = lax.fori_loop(0, n_pages // blk, sum_body, jnp.zeros((DSA_HEADS, LANES), F32))
    l = jnp.sum(l_t, axis=1, keepdims=True) + jnp.exp(s_new - m)

    acc_ref[...] = jnp.zeros_like(acc_ref)

    def v_chunk(c, slot):
        for r in range(kpc):
            page = c * kpc + r
            p = jnp.exp(s_ref[page] - m) / l
            for h in range(DSA_HEADS):
                acc_ref[h] += p[h:h + 1, :] * vbuf[slot, r, h]

    _stream_loop((v_stream,), n_pages // kpc, v_chunk, primed=True)
    ones8 = jnp.ones((8, LANES), BF16)
    rowh = lax.broadcasted_iota(I32, (DSA_HEADS, DSA_HEAD_DIM), 0)
    out = jnp.zeros((DSA_HEADS, DSA_HEAD_DIM), F32)
    for h in range(DSA_HEADS):
        resid = acc_ref[h]
        tot = jnp.zeros((8, DSA_HEAD_DIM), F32)
        for _ in range(3):
            piece = resid.astype(BF16)
            tot = tot + _nt_dot(ones8, piece)
            resid = resid - piece.astype(F32)
        out = jnp.where(rowh == h, tot, out)
    p_new = jnp.exp(s_new - m) / l
    o_ref[...] = (out + p_new * vdnew_ref[...]).astype(BF16)


def _dsa_sample(page_table, rel_bias, qi8, wi8, kinew, qd8, kdnew, vdnew, cache_idx_t, cache_k_t, cache_v_t):
    db, n_pages = page_table.shape
    n_sel = min(IDX_TOPK_MAX, (n_pages * PAGE_SIZE + 1) // 4)
    assert n_pages % 8 == 0
    ipc = math.gcd(n_pages, _DSA_S_IDX_PAGES)
    kpc = math.gcd(n_pages, _DSA_S_KV_PAGES)
    rows = -(-(n_pages + 1) // LANES) * LANES
    tri_l = jnp.asarray(np.tril(np.ones((rows, rows), np.float32), -1), BF16)
    per_seq = lambda shape: pl.BlockSpec((None,) + shape, lambda s, pt: (s,) + (0,) * len(shape))
    const = lambda shape: pl.BlockSpec(shape, lambda s, pt: (0,) * len(shape))
    any_spec = pl.BlockSpec(memory_space=pl.ANY)
    head_tile = (DSA_HEADS, DSA_HEAD_DIM)
    page_tile = (DSA_HEADS, DSA_HEAD_DIM, PAGE_SIZE)
    grid_spec = pltpu.PrefetchScalarGridSpec(
        num_scalar_prefetch=1, grid=(db,),
        in_specs=[const((DSA_HEADS, REL_BUCKETS)),
                  per_seq((IDX_HEADS, IDX_DIM)), per_seq((IDX_HEADS, 1)), per_seq((1, IDX_DIM)),
                  per_seq(head_tile), per_seq(head_tile), per_seq(head_tile),
                  const((LANES, LANES)), const((rows, rows)), any_spec, any_spec, any_spec],
        out_specs=per_seq(head_tile),
        scratch_shapes=[pltpu.VMEM((_STREAM_DEPTH, ipc, IDX_DIM, PAGE_SIZE), F32),
                        pltpu.VMEM((_STREAM_DEPTH, kpc) + page_tile, F32),
                        pltpu.VMEM((_STREAM_DEPTH, kpc) + page_tile, F32),
                        pltpu.SemaphoreType.DMA((3, _STREAM_DEPTH)),
                        pltpu.VMEM((rows, LANES), I32), pltpu.VMEM((rows, LANES), F32),
                        pltpu.VMEM((n_pages, DSA_HEADS, LANES), F32),
                        pltpu.VMEM(page_tile, F32), pltpu.VMEM(page_tile, F32)])
    return pl.pallas_call(
        functools.partial(_dsa_sample_kernel, n_pages=n_pages, n_sel=n_sel),
        grid_spec=grid_spec,
        out_shape=jax.ShapeDtypeStruct((db,) + head_tile, BF16),
        compiler_params=_cparams(("arbitrary",)),
        name="dsa_sample",
    )(page_table, rel_bias.T, qi8, wi8, kinew, qd8, kdnew, vdnew, _strict_upper(LANES), tri_l,
      cache_idx_t, cache_k_t, cache_v_t)


def kernel(x_prompt, x_sample, cache_mla_latent, cache_mla_krope, cache_dsa_k, cache_dsa_v, cache_idx_k, page_table, c_prompt, c_sample, rel_bias, w_ada, b_ada, g_attn_norm, w_in, g_q_lora, w_q_up, g_kv_lora, w_kv_up, g_mla_qn, g_mla_qr, g_mla_kn, g_mla_kr, g_dsa_q, g_dsa_k, w_out, g_ffn_norm, w_router, b_router, w_e_gu, w_e_down, w_s_gu, w_s_down):
    depth = w_ada.shape[0]
    assert depth == 1, "single-layer trunk"
    l = 0
    B, T, D = x_prompt.shape
    DB, TS, _ = x_sample.shape
    assert TS == 1, "one new token per sampled sequence"
    ns = DB * TS
    past = page_table.shape[1] * PAGE_SIZE

    w_kv = w_kv_up[l].reshape(KV_LORA, MLA_HEADS, MLA_NOPE + MLA_V)
    w_uk, w_uv = w_kv[..., :MLA_NOPE], w_kv[..., MLA_NOPE:]

    mod = _adaln(jnp.concatenate([c_prompt, c_sample], axis=0), w_ada[l].astype(BF16), b_ada[l])
    mod_p = [m.reshape(B, 1, D) for m in jnp.split(mod[:B], 6, axis=-1)]
    mod_s = jnp.split(mod[B:], 6, axis=-1)

    pw = _prep_proj_weights(w_in[l], g_q_lora[l], w_q_up[l], g_kv_lora[l], w_uk, g_mla_qn[l], g_mla_qr[l],
                            g_mla_kn[l], g_mla_kr[l], g_dsa_q[l], g_dsa_k[l], g_attn_norm[l])
    pp = _project(x_prompt.reshape(B * T, D), mod_p[0], mod_p[1], pw, _rope_tables(jnp.arange(T)), 256, T)
    ps = _project(x_sample.reshape(ns, D), mod_s[0], mod_s[1], pw,
                  _rope_tables(jnp.tile(past + jnp.arange(TS), DB)), ns, 0)

    r3 = lambda a: a.reshape(B, T, a.shape[-1])
    o_mla_p = _mla_prompt(r3(pp["qall"]), r3(pp["kall"]), r3(pp["ckvb"]), _pad_wuv(w_uv))
    o_dsa_p = _dsa_prompt(rel_bias, r3(pp["qi"]), r3(pp["wib"]), r3(pp["kid"]), r3(pp["qd"]), r3(pp["kdb"]),
                          r3(pp["vdb"]))

    o_mla_s = _mla_sample(page_table, ps["qall"].reshape(ns, MLA_HEADS, LANES), ps["kall"].reshape(ns, 1, -1),
                          ps["ckvb"].reshape(ns, 1, KV_LORA), pw["wuk"], pw["gk"],
                          w_uv.reshape(KV_LORA, MLA_OUT).astype(BF16), cache_mla_latent[l],
                          jnp.transpose(cache_mla_krope[l], (0, 2, 1)))
    heads3 = lambda a: a.reshape(ns, DSA_HEADS, DSA_HEAD_DIM)
    o_dsa_s = _dsa_sample(page_table, rel_bias, ps["qi"].reshape(ns, IDX_HEADS, IDX_DIM),
                          ps["misc"][:, _M_WI:_M_WI + IDX_HEADS].reshape(ns, IDX_HEADS, 1),
                          ps["kid"][:, :IDX_DIM].reshape(ns, 1, IDX_DIM),
                          heads3(ps["qd"]), heads3(ps["kd"]), heads3(ps["vd"]),
                          jnp.transpose(cache_idx_k[l], (0, 2, 1)),
                          jnp.transpose(cache_dsa_k[l], (0, 2, 3, 1)), jnp.transpose(cache_dsa_v[l], (0, 2, 3, 1)))

    moe_w = (g_ffn_norm[l], w_out[l].astype(BF16), w_router[l].T.astype(BF16), b_router[l],
             w_s_gu[l].astype(BF16), w_s_down[l].astype(BF16),
             w_e_gu[l].astype(BF16), w_e_down[l].astype(BF16))
    xp = _moe(x_prompt.reshape(B * T, D), o_mla_p.reshape(B * T, MLA_OUT), o_dsa_p.reshape(B * T, DSA_OUT),
              (mod_p[2], mod_p[3], mod_p[4], mod_p[5]), *moe_w, min(1024, T), T).reshape(B, T, D)
    ns_pad = -(-ns // LANES) * LANES
    pad_rows = lambda a: jnp.pad(a, ((0, ns_pad - ns), (0, 0)))
    xs = _moe(pad_rows(x_sample.reshape(ns, D)), pad_rows(o_mla_s.reshape(ns, MLA_OUT)),
              pad_rows(o_dsa_s.reshape(ns, DSA_OUT)), tuple(pad_rows(mod_s[k]) for k in (2, 3, 4, 5)),
              *moe_w, ns_pad, 0)[:ns].reshape(DB, TS, D)

    def caches(p, nb, nt):
        return (p["ckv"].reshape(1, nb, nt, KV_LORA),
                p["misc"][:, :MLA_ROPE].reshape(1, nb, nt, MLA_ROPE),
                p["kd"].reshape(1, nb, nt, DSA_HEADS, DSA_HEAD_DIM),
                p["vd"].reshape(1, nb, nt, DSA_HEADS, DSA_HEAD_DIM),
                p["misc"][:, _M_KI:_M_KI + IDX_DIM].reshape(1, nb, nt, IDX_DIM))

    return (xp, xs) + caches(pp, B, T) + caches(ps, DB, TS)
```

```python
import functools
import math

import jax
import jax.numpy as jnp
import numpy as np
from jax import lax
from jax.experimental import pallas as pl
from jax.experimental.pallas import tpu as pltpu

F32 = jnp.float32
BF16 = jnp.bfloat16
I32 = jnp.int32

D_MODEL = 1024
PAGE_SIZE = 128
EPS = 1e-6
MLA_HEADS = 8
MLA_NOPE = 64
MLA_ROPE = 32
MLA_V = 64
Q_LORA = 256
KV_LORA = 128
ROPE_BASE = 10000.0
MLA_SCALE = (MLA_NOPE + MLA_ROPE) ** -0.5
DSA_HEADS = 8
DSA_HEAD_DIM = 64
DSA_SCALE = DSA_HEAD_DIM ** -0.5
IDX_HEADS = 8
IDX_DIM = 64
IDX_TOPK_MAX = 256
REL_BUCKETS = 32
REL_MAX_DIST = 128
N_EXPERTS = 64
TOP_K = 6
N_GROUPS = 8
TOPK_GROUPS = 4
D_EXPERT = 256
D_SHARED = 256
ROUTED_SCALE = 2.5
MLA_OUT = MLA_HEADS * MLA_V
DSA_OUT = DSA_HEADS * DSA_HEAD_DIM
IN_SIZES = (Q_LORA, KV_LORA, MLA_ROPE, DSA_OUT, DSA_OUT, DSA_OUT, IDX_HEADS * IDX_DIM, IDX_DIM, IDX_HEADS)

LANES = 128
INT_MIN = -(2 ** 31)
NEG_INF = float("-inf")
VMEM_LIMIT = 56 * 1024 * 1024


def _cparams(sem):
    return pltpu.CompilerParams(dimension_semantics=sem, vmem_limit_bytes=VMEM_LIMIT)


def _split_dot(x, m01, passes=3):
    acc = None
    r = x
    for p in range(passes):
        hi = r.astype(BF16)
        part = jnp.dot(hi, m01, preferred_element_type=F32)
        acc = part if acc is None else acc + part
        if p + 1 < passes:
            r = r - hi.astype(F32)
    return acc


def _group_mean(sq, bmat):
    outs = [_split_dot(sq[:, s * LANES:(s + 1) * LANES], bmat) for s in range(sq.shape[1] // LANES)]
    return outs[0] if len(outs) == 1 else jnp.concatenate(outs, axis=1)


def _rope_slabs(x, cos, sin):
    lane = lax.broadcasted_iota(I32, (x.shape[0], LANES), 1)
    first_half = (lane % MLA_ROPE) < (MLA_ROPE // 2)
    outs = []
    for s in range(x.shape[1] // LANES):
        xs = x[:, s * LANES:(s + 1) * LANES]
        rot = jnp.where(first_half, pltpu.roll(xs, LANES - MLA_ROPE // 2, 1), pltpu.roll(xs, MLA_ROPE // 2, 1))
        outs.append(xs * cos + rot * sin)
    return outs[0] if len(outs) == 1 else jnp.concatenate(outs, axis=1)


def _nt_dot(a, b):
    return lax.dot_general(a, b, (((1,), (1,)), ((), ())), preferred_element_type=F32)


def _tn_dot(a, b):
    return lax.dot_general(a, b, (((0,), (0,)), ((), ())), preferred_element_type=F32)


def _fold_lanes(x, op):
    acc = x[:, :LANES]
    for s in range(1, x.shape[1] // LANES):
        acc = op(acc, x[:, s * LANES:(s + 1) * LANES])
    return acc


def _adaln_kernel(c_ref, w_ref, b_ref, o_ref):
    c = c_ref[...]
    s = (c * jax.nn.sigmoid(c)).astype(BF16)
    o_ref[...] = jnp.dot(s, w_ref[...], preferred_element_type=F32) + b_ref[...]


def _adaln(c, w_bf, b):
    rows = c.shape[0]
    n = w_bf.shape[1]
    tn = 1536
    return pl.pallas_call(
        _adaln_kernel,
        grid=(n // tn,),
        in_specs=[pl.BlockSpec((rows, D_MODEL), lambda j: (0, 0)),
                  pl.BlockSpec((D_MODEL, tn), lambda j: (0, j)),
                  pl.BlockSpec((1, tn), lambda j: (0, j))],
        out_specs=pl.BlockSpec((rows, tn), lambda j: (0, j)),
        out_shape=jax.ShapeDtypeStruct((rows, n), F32),
        compiler_params=_cparams(("arbitrary",)),
        name="adaln",
    )(c, w_bf, b.reshape(1, n))


_C_QLAT = 0
_C_KV = _C_QLAT + Q_LORA
_C_QD = _C_KV + KV_LORA
_C_KD = _C_QD + DSA_OUT
_C_VD = _C_KD + DSA_OUT
_C_QI = _C_VD + DSA_OUT
_C_MISC = _C_QI + IDX_HEADS * IDX_DIM
_C_KIDUP = _C_MISC + LANES
_C_END = _C_KIDUP + LANES
_M_KI = MLA_ROPE
_M_WI = MLA_ROPE + IDX_DIM


def _proj_kernel(x_ref, sh_ref, sc_ref, ga_ref, win_ref, gql_ref, wqu_ref, gkv_ref, wuk_ref,
                 gq_ref, gk_ref, gm_ref, gdq_ref, gdk_ref, cq_ref, sq_ref, cm_ref, sm_ref,
                 bq_ref, b64_ref, bm_ref, ex_ref,
                 qall_ref, kall_ref, ckv_ref, ckvb_ref, misc_ref, qd_ref, kd_ref, kdb_ref, vd_ref, vdb_ref,
                 qi_ref, kid_ref, wib_ref):
    x = x_ref[...]
    xn = x * lax.rsqrt(jnp.mean(x * x, axis=-1, keepdims=True) + EPS) * ga_ref[...]
    h = xn * (1.0 + sc_ref[...]) + sh_ref[...]
    p = jnp.dot(h.astype(BF16), win_ref[...], preferred_element_type=F32)

    ql = p[:, _C_QLAT:_C_KV]
    qln = ql * lax.rsqrt(jnp.mean(ql * ql, axis=-1, keepdims=True) + EPS) * gql_ref[...]
    q = jnp.dot(qln.astype(BF16), wqu_ref[...], preferred_element_type=F32)
    qn = q * lax.rsqrt(_group_mean(q * q, bq_ref[...]) + EPS) * gq_ref[...]
    qall_ref[...] = _rope_slabs(qn, cq_ref[...], sq_ref[...]).astype(BF16)

    kv = p[:, _C_KV:_C_QD]
    ckv = kv * lax.rsqrt(jnp.mean(kv * kv, axis=-1, keepdims=True) + EPS) * gkv_ref[...]
    ckv_ref[...] = ckv
    ckvb = ckv.astype(BF16)
    ckvb_ref[...] = ckvb
    kn = jnp.dot(ckvb, wuk_ref[...], preferred_element_type=F32)
    kn = kn * lax.rsqrt(_group_mean(kn * kn, bq_ref[...]) + EPS) * gk_ref[...]

    m = p[:, _C_MISC:_C_KIDUP]
    lane = lax.broadcasted_iota(I32, m.shape, 1)
    is_kr = lane < MLA_ROPE
    mm = _split_dot(m * m, bm_ref[...])
    mn = jnp.where(is_kr, m * lax.rsqrt(mm + EPS) * gm_ref[...], m)
    mr = _rope_slabs(mn, cm_ref[...], sm_ref[...])
    is_wi = (lane >= _M_WI) & (lane < _M_WI + IDX_HEADS)
    misc = jnp.where(is_wi, mr * (IDX_HEADS ** -0.5), mr)
    misc_ref[...] = misc
    wib_ref[...] = _split_dot(misc, ex_ref[...])
    kr_placed = jnp.where((lane >= MLA_NOPE) & (lane < MLA_NOPE + MLA_ROPE), pltpu.roll(mr, MLA_NOPE, 1), 0.0)
    kall_ref[...] = jnp.concatenate(
        [kn[:, s * LANES:(s + 1) * LANES] + kr_placed for s in range(MLA_HEADS)], axis=1).astype(BF16)

    qd = p[:, _C_QD:_C_KD]
    qd_ref[...] = qd * lax.rsqrt(_group_mean(qd * qd, b64_ref[...]) + EPS) * gdq_ref[...]
    kd = p[:, _C_KD:_C_VD]
    kdn = kd * lax.rsqrt(_group_mean(kd * kd, b64_ref[...]) + EPS) * gdk_ref[...]
    kd_ref[...] = kdn
    kdb_ref[...] = kdn.astype(BF16)
    vd = p[:, _C_VD:_C_QI]
    vd_ref[...] = vd
    vdb_ref[...] = vd.astype(BF16)
    qi_ref[...] = p[:, _C_QI:_C_MISC].astype(BF16)
    kid_ref[...] = p[:, _C_KIDUP:_C_END].astype(BF16)


def _block_mean_matrix(blocks):
    m = np.zeros((LANES, LANES), np.float32)
    for start, size in blocks:
        m[start:start + size, start:start + size] = 1.0 / size
    return jnp.asarray(m, BF16)


def _head_weight_expander():
    assert IDX_DIM == 64
    m = np.zeros((LANES, IDX_HEADS * LANES), np.float32)
    for h in range(IDX_HEADS):
        m[_M_WI + h, h * LANES:(h + 1) * LANES] = IDX_DIM ** -0.5
    return jnp.asarray(m, BF16)


def _prep_proj_weights(w_in, g_q_lora, w_q_up, g_kv_lora, w_uk, g_mla_qn, g_mla_qr, g_mla_kn, g_mla_kr,
                       g_dsa_q, g_dsa_k, g_attn_norm):
    offs = np.cumsum((0,) + IN_SIZES)
    sec = lambda k: w_in[:, offs[k]:offs[k + 1]]
    zeros = lambda n: jnp.zeros((D_MODEL, n), w_in.dtype)
    misc = jnp.concatenate([sec(2), sec(7), sec(8), zeros(LANES - MLA_ROPE - IDX_DIM - IDX_HEADS)], axis=1)
    w_in_r = jnp.concatenate([sec(0), sec(1), sec(3), sec(4), sec(5), sec(6), misc, sec(7), sec(7)], axis=1)
    wq = w_q_up.reshape(Q_LORA, MLA_HEADS, MLA_NOPE + MLA_ROPE)
    wq = jnp.pad(wq, ((0, 0), (0, 0), (0, LANES - MLA_NOPE - MLA_ROPE))).reshape(Q_LORA, MLA_HEADS * LANES)
    wk = jnp.pad(w_uk, ((0, 0), (0, 0), (0, LANES - MLA_NOPE))).reshape(KV_LORA, MLA_HEADS * LANES)
    pad1 = lambda v, n: jnp.pad(v, (0, n - v.shape[0]))
    gq = jnp.tile(pad1(jnp.concatenate([g_mla_qn, g_mla_qr]), LANES), MLA_HEADS).reshape(1, -1)
    gk = jnp.tile(pad1(g_mla_kn, LANES), MLA_HEADS).reshape(1, -1)
    gm = jnp.concatenate([g_mla_kr, jnp.ones((LANES - MLA_ROPE,), F32)]).reshape(1, -1)
    return dict(
        win=w_in_r.astype(BF16), wqu=wq.astype(BF16), wuk=wk.astype(BF16),
        ga=g_attn_norm.reshape(1, -1), gql=g_q_lora.reshape(1, -1), gkv=g_kv_lora.reshape(1, -1),
        gq=gq, gk=gk, gm=gm,
        gdq=jnp.tile(g_dsa_q, DSA_HEADS).reshape(1, -1), gdk=jnp.tile(g_dsa_k, DSA_HEADS).reshape(1, -1),
        bq=_block_mean_matrix([(0, MLA_NOPE), (MLA_NOPE, MLA_ROPE)]),
        b64=_block_mean_matrix([(0, DSA_HEAD_DIM), (DSA_HEAD_DIM, DSA_HEAD_DIM)]),
        bm=_block_mean_matrix([(0, MLA_ROPE)]),
        ex=_head_weight_expander(),
    )


def _rope_tables(pos):
    half = MLA_ROPE // 2
    inv = ROPE_BASE ** (-jnp.arange(half, dtype=F32) / half)
    ang = pos.astype(F32)[:, None] * inv
    cos, sin = jnp.cos(ang), jnp.sin(ang)
    cos32 = jnp.concatenate([cos, cos], axis=1)
    sin32 = jnp.concatenate([-sin, sin], axis=1)
    n = pos.shape[0]
    ones = lambda w: jnp.ones((n, w), F32)
    zeros = lambda w: jnp.zeros((n, w), F32)
    cq = jnp.concatenate([ones(MLA_NOPE), cos32, ones(LANES - MLA_NOPE - MLA_ROPE)], axis=1)
    sq = jnp.concatenate([zeros(MLA_NOPE), sin32, zeros(LANES - MLA_NOPE - MLA_ROPE)], axis=1)
    cm = jnp.concatenate([cos32, ones(LANES - MLA_ROPE)], axis=1)
    sm = jnp.concatenate([sin32, zeros(LANES - MLA_ROPE)], axis=1)
    return cq, sq, cm, sm


def _project(x2d, shift, scale, pw, tables, tm, rows_per_mod):
    n = x2d.shape[0]
    nt = n // tm
    cq, sq, cm, sm = tables
    tpos = cq.shape[0] // tm
    const = lambda shape: pl.BlockSpec(shape, lambda i: (0,) * len(shape))
    row = lambda w: pl.BlockSpec((tm, w), lambda i: (i, 0))
    if rows_per_mod:
        per = rows_per_mod // tm
        mod_spec = pl.BlockSpec((None, 1, D_MODEL), lambda i: (i // per, 0, 0))
    else:
        mod_spec = row(D_MODEL)
    tab = pl.BlockSpec((tm, LANES), lambda i: (i % tpos, 0))
    in_specs = [row(D_MODEL), mod_spec, mod_spec, const((1, D_MODEL)), const((D_MODEL, _C_END)),
                const((1, Q_LORA)), const((Q_LORA, MLA_HEADS * LANES)), const((1, KV_LORA)),
                const((KV_LORA, MLA_HEADS * LANES)), const((1, MLA_HEADS * LANES)), const((1, MLA_HEADS * LANES)),
                const((1, LANES)), const((1, DSA_OUT)), const((1, DSA_OUT)), tab, tab, tab, tab,
                const((LANES, LANES)), const((LANES, LANES)), const((LANES, LANES)),
                const((LANES, IDX_HEADS * LANES))]
    widths = [(MLA_HEADS * LANES, BF16), (MLA_HEADS * LANES, BF16), (KV_LORA, F32), (KV_LORA, BF16), (LANES, F32),
              (DSA_OUT, F32), (DSA_OUT, F32), (DSA_OUT, BF16), (DSA_OUT, F32), (DSA_OUT, BF16),
              (IDX_HEADS * IDX_DIM, BF16), (LANES, BF16), (IDX_HEADS * LANES, F32)]
    outs = pl.pallas_call(
        _proj_kernel,
        grid=(nt,),
        in_specs=in_specs,
        out_specs=[row(w) for w, _ in widths],
        out_shape=[jax.ShapeDtypeStruct((n, w), dt) for w, dt in widths],
        compiler_params=_cparams(("arbitrary",)),
        name="project",
    )(x2d, shift, scale, pw["ga"], pw["win"], pw["gql"], pw["wqu"], pw["gkv"], pw["wuk"],
      pw["gq"], pw["gk"], pw["gm"], pw["gdq"], pw["gdk"], cq, sq, cm, sm, pw["bq"], pw["b64"], pw["bm"],
      pw["ex"])
    names = ("qall", "kall", "ckv", "ckvb", "misc", "qd", "kd", "kdb", "vd", "vdb", "qi", "kid", "wib")
    return dict(zip(names, outs))


_MLA_TQ = 256
_MLA_TK = 512


def _mla_prompt_kernel(q_ref, k_ref, c_ref, wuv_ref, o_ref, s_ref):
    tq, tk = _MLA_TQ, _MLA_TK
    i = pl.program_id(1)
    q0 = i * tq
    nkb = (q0 + tq + tk - 1) // tk
    row = q0 + lax.broadcasted_iota(I32, (tq, tk), 0)
    col0 = lax.broadcasted_iota(I32, (tq, tk), 1)
    heads = []
    for pr in range(MLA_HEADS // 2):
        qhs = [q_ref[:, (2 * pr + half) * LANES:(2 * pr + half + 1) * LANES] for half in range(2)]

        def pass_a(j, mrun, pr=pr, qhs=qhs):
            k0 = pl.multiple_of(j * tk, tk)
            causal = col0 + k0 <= row
            out = []
            for half in range(2):
                h = 2 * pr + half
                s = _nt_dot(qhs[half], k_ref[pl.ds(k0, tk), h * LANES:(h + 1) * LANES]) * MLA_SCALE
                s = jnp.where(causal, s, NEG_INF)
                s_ref[half, :, pl.ds(k0, tk)] = s
                out.append(jnp.maximum(mrun[half], _fold_lanes(s, jnp.maximum)))
            return tuple(out)

        ninf = jnp.full((tq, LANES), NEG_INF, F32)
        mrun = lax.fori_loop(0, nkb, pass_a, (ninf, ninf))
        ms = [jnp.max(m, axis=1, keepdims=True) for m in mrun]

        def pass_b(j, carry, ms=ms):
            k0 = pl.multiple_of(j * tk, tk)
            cb = c_ref[pl.ds(k0, tk), :]
            out = []
            for half in range(2):
                l, acc = carry[half]
                p = jnp.exp(s_ref[half, :, pl.ds(k0, tk)] - ms[half])
                out.append((l + _fold_lanes(p, jnp.add),
                            acc + jnp.dot(p.astype(BF16), cb, preferred_element_type=F32)))
            return tuple(out)

        zero = jnp.zeros((tq, LANES), F32)
        for l, acc in lax.fori_loop(0, nkb, pass_b, ((zero, zero), (zero, zero))):
            heads.append((acc / jnp.sum(l, axis=1, keepdims=True)).astype(BF16))
    for pr in range(MLA_HEADS // 2):
        o = (jnp.dot(heads[2 * pr], wuv_ref[2 * pr], preferred_element_type=F32)
             + jnp.dot(heads[2 * pr + 1], wuv_ref[2 * pr + 1], preferred_element_type=F32))
        o_ref[:, pr * LANES:(pr + 1) * LANES] = o.astype(BF16)


def _pad_wuv(w_uv):
    w = jnp.transpose(w_uv, (1, 0, 2))
    even = jnp.pad(w, ((0, 0), (0, 0), (0, LANES - MLA_V)))
    odd = jnp.pad(w, ((0, 0), (0, 0), (LANES - MLA_V, 0)))
    is_odd = (jnp.arange(MLA_HEADS) % 2 == 1)[:, None, None]
    return jnp.where(is_odd, odd, even).astype(BF16)


def _mla_prompt(qall, kall, ckvb, wuv_pad):
    b, t, _ = qall.shape
    tq = _MLA_TQ
    assert t % _MLA_TK == 0
    return pl.pallas_call(
        _mla_prompt_kernel,
        grid=(b, t // tq),
        in_specs=[pl.BlockSpec((None, tq, MLA_HEADS * LANES), lambda bi, i: (bi, i, 0)),
                  pl.BlockSpec((None, t, MLA_HEADS * LANES), lambda bi, i: (bi, 0, 0)),
                  pl.BlockSpec((None, t, KV_LORA), lambda bi, i: (bi, 0, 0)),
                  pl.BlockSpec((MLA_HEADS, KV_LORA, LANES), lambda bi, i: (0, 0, 0))],
        out_specs=pl.BlockSpec((None, tq, MLA_OUT), lambda bi, i: (bi, i, 0)),
        out_shape=jax.ShapeDtypeStruct((b, t, MLA_OUT), BF16),
        scratch_shapes=[pltpu.VMEM((2, tq, t), F32)],
        compiler_params=_cparams(("arbitrary", "arbitrary")),
        name="mla_prompt",
    )(qall, kall, ckvb, wuv_pad)


_DSA_T = 128
_DSA_TQ = 256
_DSA_CW = 512


def _sortable_key(score):
    bits = pltpu.bitcast(score, I32)
    key = jnp.where(bits < 0, bits ^ jnp.int32(0x7FFFFFFF), bits)
    return jnp.where(score == 0.0, 0, key)


def _bias_tiles(rb_ref, tz_ref):
    t = _DSA_T
    r = lax.broadcasted_iota(I32, (t, t), 0)
    c = lax.broadcasted_iota(I32, (t, t), 1)
    exact = REL_BUCKETS // 2
    for which in range(2):
        d = jnp.maximum(r - c + t * which, 0)
        logd = jnp.log(jnp.maximum(d, 1).astype(F32) / exact) / math.log(REL_MAX_DIST / exact)
        far = jnp.minimum(exact + (logd * (REL_BUCKETS - exact)).astype(I32), REL_BUCKETS - 1)
        bucket = jnp.where(d < exact, d, far)
        for h in range(DSA_HEADS):
            tile = jnp.zeros((t, t), F32)
            for bk in range(REL_BUCKETS):
                tile = jnp.where(bucket == bk, rb_ref[bk, h], tile)
            tz_ref[h, which] = tile


def _select_mask(keys_ref, mask_ref, tri_ref, nch, n_sel, rows):
    cw = _DSA_CW

    def count(pred):
        def body(j, acc):
            kb = keys_ref[:, pl.ds(pl.multiple_of(j * cw, cw), cw)]
            return acc + _fold_lanes(jnp.where(pred(kb), 1.0, 0.0), jnp.add)
        acc = lax.fori_loop(0, nch, body, jnp.zeros((rows, LANES), F32))
        return jnp.sum(acc, axis=1, keepdims=True)

    kf = jnp.float32(n_sel)
    zero = jnp.zeros((rows, 1), I32)
    thr = jnp.where(count(lambda kb: kb >= zero) >= kf, zero, jnp.full((rows, 1), INT_MIN, I32))

    def bit_body(bi, thr):
        cand = thr + lax.shift_left(jnp.int32(1), 30 - bi)
        return jnp.where(count(lambda kb: kb >= cand) >= kf, cand, thr)

    thr = lax.fori_loop(0, 31, bit_body, thr)
    need = kf - count(lambda kb: kb > thr)

    def mask_body(j, carry):
        k0 = pl.multiple_of(j * cw, cw)
        kb = keys_ref[:, pl.ds(k0, cw)]
        eq = kb == thr
        eqf = jnp.where(eq, 1.0, 0.0)
        before = jnp.dot(eqf.astype(BF16), tri_ref[...], preferred_element_type=F32) + carry
        keep = ((kb > thr) | (eq & (before < need))) & (kb > INT_MIN)
        mask_ref[:, pl.ds(k0, cw)] = jnp.where(keep, 0.0, NEG_INF)
        return carry + jnp.sum(eqf, axis=1, keepdims=True)

    lax.fori_loop(0, nch, mask_body, jnp.zeros((rows, 1), F32))


def _dsa_prompt_kernel(rb_ref, qi_ref, wib_ref, kid_ref, qd_ref, kd_ref, vd_ref, tri_ref, o_ref,
                       keys_ref, mask_ref, tz_ref, s_ref, *, n_sel):
    t, tq, cw = _DSA_T, _DSA_TQ, _DSA_CW
    sub_n = cw // t
    q_sub = tq // t
    bi = pl.program_id(0)
    i = pl.program_id(1)
    nch = ((i + 1) * tq + cw - 1) // cw

    @pl.when((bi == 0) & (i == 0))
    def _():
        _bias_tiles(rb_ref, tz_ref)

    row = i * tq + lax.broadcasted_iota(I32, (tq, cw), 0)
    col0 = lax.broadcasted_iota(I32, (tq, cw), 1)
    lane = lax.broadcasted_iota(I32, (tq, LANES), 1)
    low = lane < DSA_HEAD_DIM

    def head_halves(ref, pr):
        qs = ref[:, pr * LANES:(pr + 1) * LANES].astype(BF16)
        zero = jnp.zeros_like(qs)
        return jnp.where(low, qs, zero), jnp.where(low, zero, qs)

    qim = [q for pr in range(IDX_HEADS // 2) for q in head_halves(qi_ref, pr)]

    def idx_body(c, _):
        k0 = pl.multiple_of(c * cw, cw)
        kk = kid_ref[pl.ds(k0, cw), :]
        acc = jnp.zeros((tq, cw), F32)
        for h in range(IDX_HEADS):
            r = jnp.maximum(_nt_dot(qim[h], kk), 0.0)
            w = wib_ref[:, h * LANES:(h + 1) * LANES]
            acc = acc + jnp.concatenate([w] * sub_n, axis=1) * r
        keys_ref[:, pl.ds(k0, cw)] = jnp.where(col0 + k0 <= row, _sortable_key(acc), INT_MIN)
        return 0

    lax.fori_loop(0, nch, idx_body, 0)
    few_keys = (i + 1) * tq <= n_sel

    @pl.when(few_keys)
    def _():
        def keep_all(c, _):
            k0 = pl.multiple_of(c * cw, cw)
            mask_ref[:, pl.ds(k0, cw)] = jnp.where(keys_ref[:, pl.ds(k0, cw)] > INT_MIN, 0.0, NEG_INF)
            return 0
        lax.fori_loop(0, nch, keep_all, 0)

    @pl.when(jnp.logical_not(few_keys))
    def _():
        _select_mask(keys_ref, mask_ref, tri_ref, nch, n_sel, tq)

    for pr in range(DSA_HEADS // 2):
        qms = head_halves(qd_ref, pr)
        fars = [rb_ref[REL_BUCKETS - 1, 2 * pr + half] for half in range(2)]

        def pass_a(c, mrun, pr=pr, qms=qms, fars=fars):
            k0 = pl.multiple_of(c * cw, cw)
            kb = kd_ref[pl.ds(k0, cw), pr * LANES:(pr + 1) * LANES]
            mk = mask_ref[:, pl.ds(k0, cw)]
            out = []
            for half in range(2):
                h = 2 * pr + half
                s = _nt_dot(qms[half], kb) * DSA_SCALE
                bands = []
                for a in range(q_sub):
                    qblk = i * q_sub + a
                    parts = []
                    for sub in range(sub_n):
                        blk = c * sub_n + sub
                        bias = jnp.where(blk == qblk, tz_ref[h, 0],
                                         jnp.where(blk == qblk - 1, tz_ref[h, 1], fars[half]))
                        parts.append(s[a * t:(a + 1) * t, sub * t:(sub + 1) * t] + bias)
                    bands.append(jnp.concatenate(parts, axis=1))
                s = jnp.concatenate(bands, axis=0) + mk
                s_ref[half, :, pl.ds(k0, cw)] = s
                out.append(jnp.maximum(mrun[half], _fold_lanes(s, jnp.maximum)))
            return tuple(out)

        ninf = jnp.full((tq, LANES), NEG_INF, F32)
        mrun = lax.fori_loop(0, nch, pass_a, (ninf, ninf))
        ms = [jnp.max(m, axis=1, keepdims=True) for m in mrun]

        def pass_b(c, carry, pr=pr, ms=ms):
            k0 = pl.multiple_of(c * cw, cw)
            vb = vd_ref[pl.ds(k0, cw), pr * LANES:(pr + 1) * LANES]
            out = []
            for half in range(2):
                l, acc = carry[half]
                p = jnp.exp(s_ref[half, :, pl.ds(k0, cw)] - ms[half])
                out.append((l + _fold_lanes(p, jnp.add),
                            acc + jnp.dot(p.astype(BF16), vb, preferred_element_type=F32)))
            return tuple(out)

        zero = jnp.zeros((tq, LANES), F32)
        (l0, a0), (l1, a1) = lax.fori_loop(0, nch, pass_b, ((zero, zero), (zero, zero)))
        o0 = a0 / jnp.sum(l0, axis=1, keepdims=True)
        o1 = a1 / jnp.sum(l1, axis=1, keepdims=True)
        o_ref[:, pr * LANES:(pr + 1) * LANES] = jnp.where(low, o0, o1).astype(BF16)


def _strict_upper(n):
    return jnp.asarray(np.triu(np.ones((n, n), np.float32), 1), BF16)


def _dsa_prompt(rel_bias, qi, wib, kid, qd, kdb, vdb):
    b, t, _ = qi.shape
    tq, cw = _DSA_TQ, _DSA_CW
    assert t % cw == 0 and t % tq == 0 and tq % _DSA_T == 0
    n_sel = min(IDX_TOPK_MAX, t // 4)
    blk = lambda w: pl.BlockSpec((None, tq, w), lambda bi, i: (bi, i, 0))
    full = lambda w: pl.BlockSpec((None, t, w), lambda bi, i: (bi, 0, 0))
    return pl.pallas_call(
        functools.partial(_dsa_prompt_kernel, n_sel=n_sel),
        grid=(b, t // tq),
        in_specs=[pl.BlockSpec(memory_space=pltpu.SMEM),
                  blk(IDX_HEADS * IDX_DIM), blk(IDX_HEADS * LANES), full(LANES), blk(DSA_OUT), full(DSA_OUT),
                  full(DSA_OUT), pl.BlockSpec((cw, cw), lambda bi, i: (0, 0))],
        out_specs=blk(DSA_OUT),
        out_shape=jax.ShapeDtypeStruct((b, t, DSA_OUT), BF16),
        scratch_shapes=[pltpu.VMEM((tq, t), I32), pltpu.VMEM((tq, t), F32),
                        pltpu.VMEM((DSA_HEADS, 2, _DSA_T, _DSA_T), F32), pltpu.VMEM((2, tq, t), F32)],
        compiler_params=_cparams(("arbitrary", "arbitrary")),
        name="dsa_prompt",
    )(rel_bias, qi, wib, kid, qd, kdb, vdb, _strict_upper(cw))


_EXPERTS_PER_GROUP = N_EXPERTS // N_GROUPS
_MOE_GROUP = 4


def _first_index_of_max(v, idx, axis, sentinel):
    mx = jnp.max(v, axis=axis, keepdims=True)
    first = jnp.min(jnp.where(v == mx, idx, sentinel), axis=axis, keepdims=True)
    return mx, first


def _route(logits_t, bias_col):
    n_tok = logits_t.shape[1]
    scores = jax.nn.sigmoid(logits_t)
    biased = scores + bias_col
    b3 = biased.reshape(N_GROUPS, _EXPERTS_PER_GROUP, n_tok)
    j3 = lax.broadcasted_iota(I32, b3.shape, 1)
    m1, f1 = _first_index_of_max(b3, j3, 1, _EXPERTS_PER_GROUP)
    m2 = jnp.max(jnp.where(j3 == f1, NEG_INF, b3), axis=1, keepdims=True)
    gs = (m1 + m2).reshape(N_GROUPS, n_tok)
    gi = lax.broadcasted_iota(I32, gs.shape, 0)
    gsel = jnp.zeros(gs.shape, jnp.bool_)
    for _ in range(TOPK_GROUPS):
        _, first = _first_index_of_max(gs, gi, 0, N_GROUPS)
        hit = gi == first
        gsel = gsel | hit
        gs = jnp.where(hit, NEG_INF, gs)
    gsel3 = jnp.broadcast_to(gsel.reshape(N_GROUPS, 1, n_tok), b3.shape)
    masked = jnp.where(gsel3, b3, NEG_INF).reshape(N_EXPERTS, n_tok)
    ei = lax.broadcasted_iota(I32, masked.shape, 0)
    sel = jnp.zeros(masked.shape, jnp.bool_)
    for _ in range(TOP_K):
        _, first = _first_index_of_max(masked, ei, 0, N_EXPERTS)
        hit = ei == first
        sel = sel | hit
        masked = jnp.where(hit, NEG_INF, masked)
    w = jnp.where(sel, scores, 0.0)
    gate = w / jnp.sum(w, axis=0, keepdims=True) * ROUTED_SCALE
    return sel, gate


def _swiglu_bf(x_bf, wgu_ref, wdown_ref, d_hidden):
    gu = jnp.dot(x_bf, wgu_ref[...], preferred_element_type=F32)
    g, u = gu[:, :d_hidden], gu[:, d_hidden:]
    act = (g * jax.nn.sigmoid(g)) * u
    return jnp.dot(act.astype(BF16), wdown_ref[...], preferred_element_type=F32)


def _moe_kernel(x_ref, oa_ref, ob_ref, ga_ref, sf_ref, cf_ref, gf_ref, gn_ref, wo_ref, wr_ref, br_ref,
                wsg_ref, wsd_ref, tri_ref, weg_hbm, wed_hbm, y_ref, xt_ref, acc_ref, gate_ref, rank_ref,
                wgbuf, wdbuf, wsem, *, cap):
    tm = x_ref.shape[0]
    e = pl.program_id(1)
    n_grp = pl.num_programs(1)
    lin = pl.program_id(0) * n_grp + e
    total = pl.num_programs(0) * n_grp

    def w_copies(step, slot):
        g = step % n_grp
        return (pltpu.make_async_copy(weg_hbm.at[pl.ds(g * _MOE_GROUP, _MOE_GROUP)], wgbuf.at[slot], wsem.at[0, slot]),
                pltpu.make_async_copy(wed_hbm.at[pl.ds(g * _MOE_GROUP, _MOE_GROUP)], wdbuf.at[slot], wsem.at[1, slot]))

    @pl.when(lin == 0)
    def _():
        for s0 in range(2):
            @pl.when(s0 < total)
            def _():
                for cp in w_copies(s0, s0):
                    cp.start()

    @pl.when(lin + 2 < total)
    def _():
        for cp in w_copies(lin + 2, (lin + 2) % 3):
            cp.start()

    wslot = lin % 3
    for cp in w_copies(lin, wslot):
        cp.wait()

    @pl.when(e == 0)
    def _():
        half = oa_ref.shape[1]
        attn = (jnp.dot(oa_ref[...], wo_ref[:half, :], preferred_element_type=F32)
                + jnp.dot(ob_ref[...], wo_ref[half:, :], preferred_element_type=F32))
        x1 = x_ref[...] + ga_ref[...] * attn
        y_ref[...] = x1
        hn = x1 * lax.rsqrt(jnp.mean(x1 * x1, axis=-1, keepdims=True) + EPS) * gn_ref[...]
        xt = (hn * (1.0 + cf_ref[...]) + sf_ref[...]).astype(BF16)
        xt_ref[...] = xt
        sel, gate = _route(_nt_dot(wr_ref[...], xt), br_ref[...])
        ind = jnp.where(sel, 1.0, 0.0)
        before = jnp.dot(ind.astype(BF16), tri_ref[...], preferred_element_type=F32)
        rank_ref[...] = jnp.where(sel, before, -1.0)
        gate_ref[...] = jnp.where(sel, gate, 0.0)
        acc_ref[...] = jnp.zeros_like(acc_ref)

    grp = _MOE_GROUP
    rank_rows = [rank_ref[pl.ds(e * grp + j, 1), :] for j in range(grp)]
    gate_rows = [gate_ref[pl.ds(e * grp + j, 1), :] for j in range(grp)]
    top = rank_rows[0]
    for j in range(1, grp):
        top = jnp.maximum(top, rank_rows[j])
    count = (jnp.max(top) + 1.0).astype(I32)
    n_chunks = (count + cap - 1) // cap
    slot = lax.broadcasted_iota(I32, (cap, tm), 0).astype(F32)

    def chunk(c, _):
        base = slot + (c * cap).astype(F32)
        picks = [jnp.where(base == rank_rows[j], 1.0, 0.0) for j in range(grp)]
        pick_b = jnp.concatenate(picks, axis=0).astype(BF16)
        xe = jnp.dot(pick_b, xt_ref[...], preferred_element_type=F32).astype(BF16)
        ys = []
        for j in range(grp):
            ye = _swiglu_bf(xe[j * cap:(j + 1) * cap], wgbuf.at[wslot, j], wdbuf.at[wslot, j], D_EXPERT)
            ge = jnp.sum(picks[j] * gate_rows[j], axis=1, keepdims=True)
            ys.append((ye * ge).astype(BF16))
        acc_ref[...] += _tn_dot(pick_b, jnp.concatenate(ys, axis=0))
        return 0

    lax.fori_loop(0, n_chunks, chunk, 0)

    @pl.when(e == N_EXPERTS // grp - 1)
    def _():
        shared = _swiglu_bf(xt_ref[...], wsg_ref, wsd_ref, D_SHARED)
        y_ref[...] = y_ref[...] + gf_ref[...] * (acc_ref[...] + shared)


def _moe(x2d, oa, ob, mods, g_ffn, wo_bf, wr_t_bf, b_router, wsg_bf, wsd_bf, weg_bf, wed_bf, tm, rows_per_mod):
    n = x2d.shape[0]
    assert n % tm == 0 and tm % LANES == 0
    nt = n // tm
    cap = LANES
    const = lambda shape: pl.BlockSpec(shape, lambda t, e: (0,) * len(shape))
    row = lambda w: pl.BlockSpec((tm, w), lambda t, e: (t, 0))
    if rows_per_mod:
        per = rows_per_mod // tm
        mod_spec = pl.BlockSpec((None, 1, D_MODEL), lambda t, e: (t // per, 0, 0))
    else:
        mod_spec = row(D_MODEL)
    half = oa.shape[1]
    return pl.pallas_call(
        functools.partial(_moe_kernel, cap=cap),
        grid=(nt, N_EXPERTS // _MOE_GROUP),
        in_specs=[row(D_MODEL), row(half), row(half), mod_spec, mod_spec, mod_spec, mod_spec,
                  const((1, D_MODEL)), const((2 * half, D_MODEL)), const((N_EXPERTS, D_MODEL)),
                  const((N_EXPERTS, 1)), const((D_MODEL, 2 * D_SHARED)), const((D_SHARED, D_MODEL)),
                  const((tm, tm)),
                  pl.BlockSpec(memory_space=pl.ANY), pl.BlockSpec(memory_space=pl.ANY)],
        out_specs=row(D_MODEL),
        out_shape=jax.ShapeDtypeStruct((n, D_MODEL), F32),
        scratch_shapes=[pltpu.VMEM((tm, D_MODEL), BF16), pltpu.VMEM((tm, D_MODEL), F32),
                        pltpu.VMEM((N_EXPERTS, tm), F32), pltpu.VMEM((N_EXPERTS, tm), F32),
                        pltpu.VMEM((3, _MOE_GROUP, D_MODEL, 2 * D_EXPERT), BF16),
                        pltpu.VMEM((3, _MOE_GROUP, D_EXPERT, D_MODEL), BF16),
                        pltpu.SemaphoreType.DMA((2, 3))],
        compiler_params=_cparams(("arbitrary", "arbitrary")),
        name="moe",
    )(x2d, oa, ob, *mods, g_ffn.reshape(1, -1), wo_bf, wr_t_bf, b_router.reshape(-1, 1), wsg_bf, wsd_bf,
      _strict_upper(tm), weg_bf, wed_bf)


def _col_blocks(q):
    nh = q.shape[0]
    r = lax.broadcasted_iota(I32, (nh, LANES), 0)
    c = lax.broadcasted_iota(I32, (nh, LANES), 1)
    blocks = [_tn_dot(q, jnp.where((r == h) & (c == h), 1.0, 0.0).astype(q.dtype)) for h in range(nh)]
    return jnp.concatenate(blocks, axis=0)


def _rows_to_cols(row):
    r8 = jnp.broadcast_to(row, (8, row.shape[1]))
    e0 = jnp.where(lax.broadcasted_iota(I32, (8, LANES), 0) == 0, 1.0, 0.0).astype(row.dtype)
    return _tn_dot(r8, e0)


def _head_diag(acc, width):
    r = lax.broadcasted_iota(I32, acc.shape, 0)
    c = lax.broadcasted_iota(I32, acc.shape, 1)
    return jnp.sum(jnp.where(r == c // width, acc, 0.0), axis=0, keepdims=True)


class _PageStream:
    def __init__(self, hbm_ref, buf_ref, sem_ref, pt_ref, seq, pages_per_chunk):
        self.hbm, self.buf, self.sem, self.pt, self.seq, self.ppc = hbm_ref, buf_ref, sem_ref, pt_ref, seq, pages_per_chunk

    def _copy(self, page, slot, r):
        return pltpu.make_async_copy(self.hbm.at[page], self.buf.at[slot, r], self.sem.at[slot])

    def start(self, chunk, slot):
        for r in range(self.ppc):
            self._copy(self.pt[self.seq, chunk * self.ppc + r], slot, r).start()

    def wait(self, slot):
        for r in range(self.ppc):
            self._copy(0, slot, r).wait()


_STREAM_DEPTH = 8


def _stream_prime(streams, n_chunks):
    for d in range(min(_STREAM_DEPTH - 1, n_chunks)):
        for st in streams:
            st.start(d, d)


def _stream_loop(streams, n_chunks, body, primed=False):
    depth = _STREAM_DEPTH
    if not primed:
        _stream_prime(streams, n_chunks)

    def step(c, _):
        slot = c % depth
        nxt = c + depth - 1

        @pl.when(nxt < n_chunks)
        def _():
            for st in streams:
                st.start(nxt, nxt % depth)

        for st in streams:
            st.wait(slot)
        body(c, slot)
        return 0

    lax.fori_loop(0, n_chunks, step, 0)


def _softmax_stats(sc_ref, n_rows, blk):
    nb = n_rows // blk
    tail = n_rows - nb * blk

    def mx_body(j, m):
        return jnp.maximum(m, jnp.max(sc_ref[pl.ds(pl.multiple_of(j * blk, blk), blk), :], axis=0, keepdims=True))

    m = lax.fori_loop(0, nb, mx_body, jnp.full((1, LANES), NEG_INF, F32))
    if tail:
        m = jnp.maximum(m, jnp.max(sc_ref[pl.ds(nb * blk, tail), :], axis=0, keepdims=True))

    def sum_body(j, l):
        return l + jnp.sum(jnp.exp(sc_ref[pl.ds(pl.multiple_of(j * blk, blk), blk), :] - m), axis=0, keepdims=True)

    l = lax.fori_loop(0, nb, sum_body, jnp.zeros((1, LANES), F32))
    if tail:
        l = l + jnp.sum(jnp.exp(sc_ref[pl.ds(nb * blk, tail), :] - m), axis=0, keepdims=True)
    return m, l


_MLA_S_PAGES = 8


def _mla_sample_kernel(pt_ref, q_ref, knew_ref, cnew_ref, wuk_ref, gk_ref, wuv_ref, lat_hbm, kr_hbm, o_ref,
                       latbuf, krbuf, sems, sc_ref, latbf_ref, *, n_pages):
    s_id = pl.program_id(0)
    ppc = math.gcd(n_pages, _MLA_S_PAGES)
    ck = ppc * PAGE_SIZE
    n_chunks = n_pages // ppc
    past = n_pages * PAGE_SIZE
    row8 = lax.broadcasted_iota(I32, (8, LANES), 0)

    qblk = _col_blocks(q_ref[...]).astype(BF16)
    qr = qblk[MLA_NOPE:MLA_NOPE + MLA_ROPE, :].astype(F32)
    for h in range(1, MLA_HEADS):
        qr = qr + qblk[h * LANES + MLA_NOPE:h * LANES + MLA_NOPE + MLA_ROPE, :].astype(F32)
    qr = qr.astype(BF16)
    lat_stream = _PageStream(lat_hbm, latbuf, sems.at[0], pt_ref, s_id, ppc)
    kr_stream = _PageStream(kr_hbm, krbuf, sems.at[1], pt_ref, s_id, ppc)

    def score_chunk(c, slot):
        lat = latbuf[slot].reshape(ck, KV_LORA).astype(BF16)
        latbf_ref[pl.ds(pl.multiple_of(c * ck, ck), ck), :] = lat
        kraw = jnp.dot(lat, wuk_ref[...], preferred_element_type=F32)
        slabs = []
        for h in range(MLA_HEADS):
            x = kraw[:, h * LANES:(h + 1) * LANES]
            ms = jnp.sum(x * x, axis=1, keepdims=True) * (1.0 / MLA_NOPE)
            slabs.append((x * lax.rsqrt(ms + EPS) * gk_ref[:, h * LANES:(h + 1) * LANES]).astype(BF16))
        kn = jnp.concatenate(slabs, axis=1)
        s_rope = jnp.concatenate([_tn_dot(krbuf[slot, r].astype(BF16), qr) for r in range(ppc)], axis=0)
        s = (jnp.dot(kn, qblk, preferred_element_type=F32) + s_rope) * MLA_SCALE
        sc_ref[pl.ds(pl.multiple_of(c * ck, ck), ck), :] = s

    _stream_loop((lat_stream, kr_stream), n_chunks, score_chunk)
    s_new = jnp.dot(jnp.broadcast_to(knew_ref[...], (8, knew_ref.shape[1])), qblk,
                    preferred_element_type=F32) * MLA_SCALE
    sc_ref[pl.ds(past, 8), :] = jnp.where(row8 == 0, s_new, NEG_INF)
    m, l = _softmax_stats(sc_ref, past + 8, ck)

    def pv_chunk(c, acc):
        k0 = pl.multiple_of(c * ck, ck)
        p = jnp.exp(sc_ref[pl.ds(k0, ck), :] - m) / l
        return acc + _tn_dot(p.astype(BF16), latbf_ref[pl.ds(k0, ck), :])

    acc = lax.fori_loop(0, n_chunks, pv_chunk, jnp.zeros((LANES, KV_LORA), F32))
    p_new = jnp.exp(sc_ref[pl.ds(past, 8), :] - m) / l
    acc = acc + _tn_dot(p_new.astype(BF16), jnp.broadcast_to(cnew_ref[...], (8, KV_LORA)))
    out = jnp.dot(acc.astype(BF16), wuv_ref[...], preferred_element_type=F32)
    o_ref[...] = _head_diag(out, MLA_V).astype(BF16)


def _mla_sample(page_table, q8, knew, cnew, wuk_pad, gk, wuv_flat, cache_lat, cache_kr):
    db, n_pages = page_table.shape
    ppc = math.gcd(n_pages, _MLA_S_PAGES)
    past = n_pages * PAGE_SIZE
    per_seq = lambda shape: pl.BlockSpec((None,) + shape, lambda s, pt: (s,) + (0,) * len(shape))
    const = lambda shape: pl.BlockSpec(shape, lambda s, pt: (0,) * len(shape))
    grid_spec = pltpu.PrefetchScalarGridSpec(
        num_scalar_prefetch=1, grid=(db,),
        in_specs=[per_seq((MLA_HEADS, LANES)), per_seq((1, MLA_HEADS * LANES)), per_seq((1, KV_LORA)),
                  const((KV_LORA, MLA_HEADS * LANES)), const((1, MLA_HEADS * LANES)), const((KV_LORA, MLA_OUT)),
                  pl.BlockSpec(memory_space=pl.ANY), pl.BlockSpec(memory_space=pl.ANY)],
        out_specs=per_seq((1, MLA_OUT)),
        scratch_shapes=[pltpu.VMEM((_STREAM_DEPTH, ppc, PAGE_SIZE, KV_LORA), F32),
                        pltpu.VMEM((_STREAM_DEPTH, ppc, MLA_ROPE, PAGE_SIZE), F32),
                        pltpu.SemaphoreType.DMA((2, _STREAM_DEPTH)), pltpu.VMEM((past + 8, LANES), F32),
                        pltpu.VMEM((past, KV_LORA), BF16)])
    return pl.pallas_call(
        functools.partial(_mla_sample_kernel, n_pages=n_pages),
        grid_spec=grid_spec,
        out_shape=jax.ShapeDtypeStruct((db, 1, MLA_OUT), BF16),
        compiler_params=_cparams(("arbitrary",)),
        name="mla_sample",
    )(page_table, q8, knew, cnew, wuk_pad, gk, wuv_flat, cache_lat, cache_kr)


_DSA_S_IDX_PAGES = 16


def _select_flat(keys, tri_u, tri_l, n_sel):
    def count(pred):
        c = jnp.sum(jnp.where(pred, 1.0, 0.0), axis=1, keepdims=True)
        return jnp.sum(c, axis=0, keepdims=True)

    kf = jnp.float32(n_sel)
    zero = jnp.zeros((1, 1), I32)
    thr = jnp.where(count(keys >= zero) >= kf, zero, jnp.full((1, 1), INT_MIN, I32))

    def bit_body(bi, thr):
        cand = thr + lax.shift_left(jnp.int32(1), 30 - bi)
        return jnp.where(count(keys >= cand) >= kf, cand, thr)

    thr = lax.fori_loop(0, 31, bit_body, thr)
    need = kf - count(keys > thr)
    eq = keys == thr
    eqf = jnp.where(eq, 1.0, 0.0)
    within = jnp.dot(eqf.astype(BF16), tri_u, preferred_element_type=F32)
    rowcount = jnp.broadcast_to(jnp.sum(eqf, axis=1, keepdims=True), eqf.shape)
    carry = jnp.dot(tri_l, rowcount.astype(BF16), preferred_element_type=F32)
    keep = ((keys > thr) | (eq & (within + carry < need))) & (keys > INT_MIN)
    return jnp.where(keep, 1.0, 0.0)


_DSA_S_KV_PAGES = 4


def _round_bf16(x):
    return x.astype(BF16).astype(F32)


def _dsa_sample_kernel(pt_ref, rbt_ref, qi_ref, wi_ref, kinew_ref, qd_ref, kdnew_ref, vdnew_ref,
                       triu_ref, tril_ref, idx_hbm, k_hbm, v_hbm, o_ref,
                       idxbuf, kvbuf, vbuf, sems, keys_ref, mask_ref, s_ref, qcol_ref, acc_ref, *, n_pages, n_sel):
    s_id = pl.program_id(0)
    ipc = math.gcd(n_pages, _DSA_S_IDX_PAGES)
    kpc = math.gcd(n_pages, _DSA_S_KV_PAGES)
    lane1 = lax.broadcasted_iota(I32, (1, LANES), 1)
    qi = qi_ref[...]
    wi = _round_bf16(wi_ref[...])

    keys_ref[...] = jnp.full(keys_ref.shape, INT_MIN, I32)
    idx_stream = _PageStream(idx_hbm, idxbuf, sems.at[0], pt_ref, s_id, ipc)

    def idx_chunk(c, slot):
        for r in range(ipc):
            kk = idxbuf[slot, r].astype(BF16)
            rr = jnp.maximum(jnp.dot(qi, kk, preferred_element_type=F32) * (IDX_DIM ** -0.5), 0.0)
            sc = jnp.sum(wi * _round_bf16(rr), axis=0, keepdims=True)
            keys_ref[pl.ds(c * ipc + r, 1), :] = _sortable_key(sc)

    _stream_loop((idx_stream,), n_pages // ipc, idx_chunk)
    k_stream = _PageStream(k_hbm, kvbuf, sems.at[1], pt_ref, s_id, kpc)
    v_stream = _PageStream(v_hbm, vbuf, sems.at[2], pt_ref, s_id, kpc)
    _stream_prime((k_stream,), n_pages // kpc)
    r_new = jnp.maximum(jnp.sum(qi.astype(F32) * kinew_ref[...].astype(F32), axis=1, keepdims=True)
                        * (IDX_DIM ** -0.5), 0.0)
    sc_new = jnp.sum(wi * _round_bf16(r_new), axis=0, keepdims=True)
    keys_ref[pl.ds(n_pages, 1), :] = jnp.where(lane1 == 0, _sortable_key(jnp.broadcast_to(sc_new, (1, LANES))),
                                               INT_MIN)

    mask_ref[...] = _select_flat(keys_ref[...], triu_ref[...], tril_ref[...], n_sel)

    qd = qd_ref[...]
    row8 = lax.broadcasted_iota(I32, (8, LANES), 0)
    e0 = jnp.where(row8 == 0, 1.0, 0.0).astype(BF16)
    for h in range(DSA_HEADS):
        resid = jnp.broadcast_to(qd[h:h + 1, :], (8, DSA_HEAD_DIM))
        col = jnp.zeros((DSA_HEAD_DIM, LANES), F32)
        for _ in range(3):
            piece = resid.astype(BF16)
            col = col + _tn_dot(piece, e0)
            resid = resid - piece.astype(F32)
        qcol_ref[h] = col

    far_col = rbt_ref[:, REL_BUCKETS - 1:REL_BUCKETS]
    exact = REL_BUCKETS // 2
    d_row = PAGE_SIZE - lane1
    logd = jnp.log(jnp.maximum(d_row, 1).astype(F32) / exact) / math.log(REL_MAX_DIST / exact)
    far_b = jnp.minimum(exact + (logd * (REL_BUCKETS - exact)).astype(I32), REL_BUCKETS - 1)
    bucket = jnp.where(d_row < exact, d_row, far_b)
    near = jnp.zeros((DSA_HEADS, LANES), F32)
    for bk in range(REL_BUCKETS):
        near = jnp.where(bucket == bk, rbt_ref[:, bk:bk + 1], near)

    _stream_prime((v_stream,), n_pages // kpc)

    def k_chunk(c, slot):
        for r in range(kpc):
            page = c * kpc + r
            for h in range(DSA_HEADS):
                s_ref[page, h:h + 1, :] = jnp.sum(kvbuf[slot, r, h] * qcol_ref[h], axis=0, keepdims=True)
            bias = jnp.where(page == n_pages - 1, near, far_col)
            keep = mask_ref[pl.ds(page, 1), :] > 0.5
            s_ref[page] = jnp.where(keep, s_ref[page] * DSA_SCALE + bias, NEG_INF)

    _stream_loop((k_stream,), n_pages // kpc, k_chunk, primed=True)
    s_new = (jnp.sum(qd * kdnew_ref[...], axis=1, keepdims=True) * DSA_SCALE
             + rbt_ref[:, 0:1])
    keep_new = mask_ref[pl.ds(n_pages, 1), :][:, 0:1] > 0.5
    s_new = jnp.where(keep_new, s_new, NEG_INF)

    blk = 8
    def mx_body(j, m):
        return jnp.maximum(m, jnp.max(s_ref[pl.ds(pl.multiple_of(j * blk, blk), blk)], axis=0))

    m_t = lax.fori_loop(0, n_pages // blk, mx_body, jnp.full((DSA_HEADS, LANES), NEG_INF, F32))
    m = jnp.maximum(jnp.max(m_t, axis=1, keepdims=True), s_new)

    def sum_body(j, l):
        return l + jnp.sum(jnp.exp(s_ref[pl.ds(pl.multiple_of(j * blk, blk), blk)] - m), axis=0)

    l_t = lax.fori_loop(0, n_pages // blk, sum_body, jnp.zeros((DSA_HEADS, LANES), F32))
    l = jnp.sum(l_t, axis=1, keepdims=True) + jnp.exp(s_new - m)

    acc_ref[...] = jnp.zeros_like(acc_ref)

    def v_chunk(c, slot):
        for r in range(kpc):
            page = c * kpc + r
            p = jnp.exp(s_ref[page] - m) / l
            for h in range(DSA_HEADS):
                acc_ref[h] += p[h:h + 1, :] * vbuf[slot, r, h]

    _stream_loop((v_stream,), n_pages // kpc, v_chunk, primed=True)
    ones8 = jnp.ones((8, LANES), BF16)
    rowh = lax.broadcasted_iota(I32, (DSA_HEADS, DSA_HEAD_DIM), 0)
    out = jnp.zeros((DSA_HEADS, DSA_HEAD_DIM), F32)
    for h in range(DSA_HEADS):
        resid = acc_ref[h]
        tot = jnp.zeros((8, DSA_HEAD_DIM), F32)
        for _ in range(3):
            piece = resid.astype(BF16)
            tot = tot + _nt_dot(ones8, piece)
            resid = resid - piece.astype(F32)
        out = jnp.where(rowh == h, tot, out)
    p_new = jnp.exp(s_new - m) / l
    o_ref[...] = (out + p_new * vdnew_ref[...]).astype(BF16)


def _dsa_sample(page_table, rel_bias, qi8, wi8, kinew, qd8, kdnew, vdnew, cache_idx_t, cache_k_t, cache_v_t):
    db, n_pages = page_table.shape
    n_sel = min(IDX_TOPK_MAX, (n_pages * PAGE_SIZE + 1) // 4)
    assert n_pages % 8 == 0
    ipc = math.gcd(n_pages, _DSA_S_IDX_PAGES)
    kpc = math.gcd(n_pages, _DSA_S_KV_PAGES)
    rows = -(-(n_pages + 1) // LANES) * LANES
    tri_l = jnp.asarray(np.tril(np.ones((rows, rows), np.float32), -1), BF16)
    per_seq = lambda shape: pl.BlockSpec((None,) + shape, lambda s, pt: (s,) + (0,) * len(shape))
    const = lambda shape: pl.BlockSpec(shape, lambda s, pt: (0,) * len(shape))
    any_spec = pl.BlockSpec(memory_space=pl.ANY)
    head_tile = (DSA_HEADS, DSA_HEAD_DIM)
    page_tile = (DSA_HEADS, DSA_HEAD_DIM, PAGE_SIZE)
    grid_spec = pltpu.PrefetchScalarGridSpec(
        num_scalar_prefetch=1, grid=(db,),
        in_specs=[const((DSA_HEADS, REL_BUCKETS)),
                  per_seq((IDX_HEADS, IDX_DIM)), per_seq((IDX_HEADS, 1)), per_seq((1, IDX_DIM)),
                  per_seq(head_tile), per_seq(head_tile), per_seq(head_tile),
                  const((LANES, LANES)), const((rows, rows)), any_spec, any_spec, any_spec],
        out_specs=per_seq(head_tile),
        scratch_shapes=[pltpu.VMEM((_STREAM_DEPTH, ipc, IDX_DIM, PAGE_SIZE), F32),
                        pltpu.VMEM((_STREAM_DEPTH, kpc) + page_tile, F32),
                        pltpu.VMEM((_STREAM_DEPTH, kpc) + page_tile, F32),
                        pltpu.SemaphoreType.DMA((3, _STREAM_DEPTH)),
                        pltpu.VMEM((rows, LANES), I32), pltpu.VMEM((rows, LANES), F32),
                        pltpu.VMEM((n_pages, DSA_HEADS, LANES), F32),
                        pltpu.VMEM(page_tile, F32), pltpu.VMEM(page_tile, F32)])
    return pl.pallas_call(
        functools.partial(_dsa_sample_kernel, n_pages=n_pages, n_sel=n_sel),
        grid_spec=grid_spec,
        out_shape=jax.ShapeDtypeStruct((db,) + head_tile, BF16),
        compiler_params=_cparams(("arbitrary",)),
        name="dsa_sample",
    )(page_table, rel_bias.T, qi8, wi8, kinew, qd8, kdnew, vdnew, _strict_upper(LANES), tri_l,
      cache_idx_t, cache_k_t, cache_v_t)


def kernel(x_prompt, x_sample, cache_mla_latent, cache_mla_krope, cache_dsa_k, cache_dsa_v, cache_idx_k, page_table, c_prompt, c_sample, rel_bias, w_ada, b_ada, g_attn_norm, w_in, g_q_lora, w_q_up, g_kv_lora, w_kv_up, g_mla_qn, g_mla_qr, g_mla_kn, g_mla_kr, g_dsa_q, g_dsa_k, w_out, g_ffn_norm, w_router, b_router, w_e_gu, w_e_down, w_s_gu, w_s_down):
    depth = w_ada.shape[0]
    assert depth == 1, "single-layer trunk"
    l = 0
    B, T, D = x_prompt.shape
    DB, TS, _ = x_sample.shape
    assert TS == 1, "one new token per sampled sequence"
    ns = DB * TS
    past = page_table.shape[1] * PAGE_SIZE

    w_kv = w_kv_up[l].reshape(KV_LORA, MLA_HEADS, MLA_NOPE + MLA_V)
    w_uk, w_uv = w_kv[..., :MLA_NOPE], w_kv[..., MLA_NOPE:]

    mod = _adaln(jnp.concatenate([c_prompt, c_sample], axis=0), w_ada[l].astype(BF16), b_ada[l])
    mod_p = [m.reshape(B, 1, D) for m in jnp.split(mod[:B], 6, axis=-1)]
    mod_s = jnp.split(mod[B:], 6, axis=-1)

    pw = _prep_proj_weights(w_in[l], g_q_lora[l], w_q_up[l], g_kv_lora[l], w_uk, g_mla_qn[l], g_mla_qr[l],
                            g_mla_kn[l], g_mla_kr[l], g_dsa_q[l], g_dsa_k[l], g_attn_norm[l])
    pp = _project(x_prompt.reshape(B * T, D), mod_p[0], mod_p[1], pw, _rope_tables(jnp.arange(T)), 256, T)
    ps = _project(x_sample.reshape(ns, D), mod_s[0], mod_s[1], pw,
                  _rope_tables(jnp.tile(past + jnp.arange(TS), DB)), ns, 0)

    r3 = lambda a: a.reshape(B, T, a.shape[-1])
    o_mla_p = _mla_prompt(r3(pp["qall"]), r3(pp["kall"]), r3(pp["ckvb"]), _pad_wuv(w_uv))
    o_dsa_p = _dsa_prompt(rel_bias, r3(pp["qi"]), r3(pp["wib"]), r3(pp["kid"]), r3(pp["qd"]), r3(pp["kdb"]),
                          r3(pp["vdb"]))

    o_mla_s = _mla_sample(page_table, ps["qall"].reshape(ns, MLA_HEADS, LANES), ps["kall"].reshape(ns, 1, -1),
                          ps["ckvb"].reshape(ns, 1, KV_LORA), pw["wuk"], pw["gk"],
                          w_uv.reshape(KV_LORA, MLA_OUT).astype(BF16), cache_mla_latent[l],
                          jnp.transpose(cache_mla_krope[l], (0, 2, 1)))
    heads3 = lambda a: a.reshape(ns, DSA_HEADS, DSA_HEAD_DIM)
    o_dsa_s = _dsa_sample(page_table, rel_bias, ps["qi"].reshape(ns, IDX_HEADS, IDX_DIM),
                          ps["misc"][:, _M_WI:_M_WI + IDX_HEADS].reshape(ns, IDX_HEADS, 1),
                          ps["kid"][:, :IDX_DIM].reshape(ns, 1, IDX_DIM),
                          heads3(ps["qd"]), heads3(ps["kd"]), heads3(ps["vd"]),
                          jnp.transpose(cache_idx_k[l], (0, 2, 1)),
                          jnp.transpose(cache_dsa_k[l], (0, 2, 3, 1)), jnp.transpose(cache_dsa_v[l], (0, 2, 3, 1)))

    moe_w = (g_ffn_norm[l], w_out[l].astype(BF16), w_router[l].T.astype(BF16), b_router[l],
             w_s_gu[l].astype(BF16), w_s_down[l].astype(BF16),
             w_e_gu[l].astype(BF16), w_e_down[l].astype(BF16))
    xp = _moe(x_prompt.reshape(B * T, D), o_mla_p.reshape(B * T, MLA_OUT), o_dsa_p.reshape(B * T, DSA_OUT),
              (mod_p[2], mod_p[3], mod_p[4], mod_p[5]), *moe_w, min(1024, T), T).reshape(B, T, D)
    ns_pad = -(-ns // LANES) * LANES
    pad_rows = lambda a: jnp.pad(a, ((0, ns_pad - ns), (0, 0)))
    xs = _moe(pad_rows(x_sample.reshape(ns, D)), pad_rows(o_mla_s.reshape(ns, MLA_OUT)),
              pad_rows(o_dsa_s.reshape(ns, DSA_OUT)), tuple(pad_rows(mod_s[k]) for k in (2, 3, 4, 5)),
              *moe_w, ns_pad, 0)[:ns].reshape(DB, TS, D)

    def caches(p, nb, nt):
        return (p["ckv"].reshape(1, nb, nt, KV_LORA),
                p["misc"][:, :MLA_ROPE].reshape(1, nb, nt, MLA_ROPE),
                p["kd"].reshape(1, nb, nt, DSA_HEADS, DSA_HEAD_DIM),
                p["vd"].reshape(1, nb, nt, DSA_HEADS, DSA_HEAD_DIM),
                p["misc"][:, _M_KI:_M_KI + IDX_DIM].reshape(1, nb, nt, IDX_DIM))

    return (xp, xs) + caches(pp, B, T) + caches(ps, DB, TS)
```
